```python
import jax, jax.numpy as jnp
from jax import lax
import numpy as np

D_MODEL = 1024
BATCH = 16
SEQ = 2048
DEPTH = 1

HEAD_DIM = 64
RWKV_HEADS = 16
RWKV_W = RWKV_HEADS * HEAD_DIM
ATT_HEADS = 16
ATT_W = ATT_HEADS * HEAD_DIM
MIX_W = RWKV_W + ATT_W
DECAY_RANK = 64
ICLR_RANK = 64
SHIFT_COLS = 3 * RWKV_W + DECAY_RANK + ICLR_RANK
RWKV_GATE0 = SHIFT_COLS
ATT0 = RWKV_GATE0 + RWKV_W
ATT_GATE0 = ATT0 + 3 * ATT_W
N_IN = ATT_GATE0 + ATT_W
DILATED_PATTERNS = ((128, 1), (512, 4), (2048, 16))
BLK = 128
RMS_EPS = 1e-5
GN_EPS = 64e-5

kernel_name = 'hybrid_rwkv7_dilated_attn'


def rms_norm(x, w):
    xf = x.astype(jnp.float32)
    y = xf * lax.rsqrt(jnp.mean(xf * xf, axis=-1, keepdims=True) + RMS_EPS)
    return y * w.astype(jnp.float32)


def rwkv7_time_mix(p_shift, gate, w0, w_up, a0, a_up, k_k, k_a, r_k, ln_x_w, ln_x_b):
    B, S, _ = p_shift.shape
    r, k, v, w_lo, a_lo = jnp.split(
        p_shift, [RWKV_W, 2 * RWKV_W, 3 * RWKV_W, 3 * RWKV_W + DECAY_RANK], axis=-1)
    w = -jax.nn.softplus(-(w0 + jnp.tanh(w_lo) @ w_up)) - 0.5
    decay = jnp.exp(-jnp.exp(w))
    a = jax.nn.sigmoid(a0 + a_lo @ a_up)
    heads = lambda t: t.reshape(B, S, RWKV_HEADS, HEAD_DIM)
    kk = heads(k * k_k)
    kk = kk / jnp.maximum(jnp.linalg.norm(kk, axis=-1, keepdims=True), 1e-12)
    k = k * (1.0 + (a - 1.0) * k_a)
    r_h, k_h, v_h, w_h, a_h = heads(r), heads(k), heads(v), heads(decay), heads(a)

    def step(state, inp):
        r_t, w_t, k_t, v_t, av_t, bv_t = inp
        sa = jnp.einsum('bhvk,bhk->bhv', state, av_t)
        state = (state * w_t[:, :, None, :] + sa[..., None] * bv_t[:, :, None, :]
                 + v_t[..., None] * k_t[:, :, None, :])
        return state, jnp.einsum('bhvk,bhk->bhv', state, r_t)

    xs = tuple(t.astype(jnp.float32).transpose(1, 0, 2, 3)
               for t in (r_h, w_h, k_h, v_h, -kk, kk * a_h))
    s0 = jnp.zeros((B, RWKV_HEADS, HEAD_DIM, HEAD_DIM), jnp.float32)
    _, y = lax.scan(step, s0, xs)
    y = y.transpose(1, 0, 2, 3)
    mu = jnp.mean(y, axis=-1, keepdims=True)
    var = jnp.mean(jnp.square(y - mu), axis=-1, keepdims=True)
    y = ((y - mu) * lax.rsqrt(var + GN_EPS)).reshape(B, S, RWKV_W) * ln_x_w + ln_x_b
    bonus = jnp.sum(r_h * k_h * r_k, axis=-1, keepdims=True) * v_h
    return (y + bonus.reshape(B, S, RWKV_W)) * jax.nn.silu(gate)


def dilated_window_attention(q, k, v, window, dilation):
    B, S, H, E = q.shape
    L = S // dilation
    steps = window // dilation
    nb = -(-L // BLK)
    Lp = nb * BLK

    def to_sub(t):
        t = t.reshape(B, L, dilation, H, E).transpose(0, 2, 3, 1, 4)
        t = jnp.pad(t, ((0, 0), (0, 0), (0, 0), (0, Lp - L), (0, 0)))
        return t.reshape(B, dilation, H, nb, BLK, E)

    def band(t):
        prev = jnp.pad(t, ((0, 0), (0, 0), (0, 0), (1, 0), (0, 0), (0, 0)))[:, :, :, :-1]
        return jnp.concatenate([prev, t], axis=4)

    qb = to_sub(q)
    kc, vc = band(to_sub(k)), band(to_sub(v))
    s = jnp.einsum('bdhnqe,bdhnke->bdhnqk', qb, kc) * (E ** -0.5)
    qi = jnp.arange(nb)[:, None, None] * BLK + jnp.arange(BLK)[None, :, None]
    ki = (jnp.arange(nb)[:, None, None] - 1) * BLK + jnp.arange(2 * BLK)[None, None, :]
    dist = qi - ki
    valid = (dist >= 0) & (dist <= steps) & (ki >= 0)
    s = jnp.where(valid, s, -jnp.inf)
    m = jnp.max(s, axis=-1, keepdims=True)
    pexp = jnp.exp(s - m)
    l = jnp.sum(pexp, axis=-1, keepdims=True)
    o = jnp.einsum('bdhnqk,bdhnke->bdhnqe', pexp, vc) / l
    lse = (m + jnp.log(l))[..., 0]
    o = o.reshape(B, dilation, H, Lp, E)[:, :, :, :L].transpose(0, 3, 1, 2, 4).reshape(B, S, H, E)
    lse = lse.reshape(B, dilation, H, Lp)[..., :L].transpose(0, 3, 1, 2).reshape(B, S, H)
    return o, lse


def mixed_dilated_attention(p_att, gate):
    B, S, _ = p_att.shape
    q, k, v = (t.reshape(B, S, ATT_HEADS, HEAD_DIM) for t in jnp.split(p_att, 3, axis=-1))
    outs, lses = [], []
    for window, dilation in DILATED_PATTERNS:
        o, lse = dilated_window_attention(q, k, v, window, dilation)
        outs.append(o)
        lses.append(lse)
    wts = jax.nn.softmax(jnp.stack(lses, axis=0), axis=0)
    o = jnp.sum(wts[..., None] * jnp.stack(outs, axis=0), axis=0)
    return o.reshape(B, S, ATT_W) * jax.nn.silu(gate)


def setup_inputs(seed: int = 0) -> dict:
    key = jax.random.key(seed)
    ks = jax.random.split(key, 15)
    nrm = lambda k, shape: jax.random.normal(k, shape, jnp.float32)
    return {
        'x': nrm(ks[0], (BATCH, SEQ, D_MODEL)),
        'norm_w': 1.0 + 0.02 * nrm(ks[1], (D_MODEL,)),
        'w_in': nrm(ks[2], (D_MODEL, N_IN)) * D_MODEL ** -0.5,
        'mu_shift': jax.random.uniform(ks[3], (SHIFT_COLS,), jnp.float32),
        'w0': jax.random.uniform(ks[4], (RWKV_W,), jnp.float32, minval=-6.0, maxval=-1.0),
        'w_up': 0.1 * nrm(ks[5], (DECAY_RANK, RWKV_W)),
        'a0': 0.1 * nrm(ks[6], (RWKV_W,)),
        'a_up': 0.1 * nrm(ks[7], (ICLR_RANK, RWKV_W)),
        'k_k': 0.85 + 0.02 * nrm(ks[8], (RWKV_W,)),
        'k_a': 1.0 + 0.02 * nrm(ks[9], (RWKV_W,)),
        'r_k': 0.1 * nrm(ks[10], (RWKV_HEADS, HEAD_DIM)),
        'ln_x_w': 1.0 + 0.02 * nrm(ks[11], (RWKV_W,)),
        'ln_x_b': 0.02 * nrm(ks[12], (RWKV_W,)),
        'w_out': nrm(ks[13], (MIX_W, D_MODEL)) * MIX_W ** -0.5,
        'final_norm_w': 1.0 + 0.02 * nrm(ks[14], (D_MODEL,)),
    }


def reference(x, norm_w, w_in, mu_shift, w0, w_up, a0, a_up, k_k, k_a, r_k,
              ln_x_w, ln_x_b, w_out, final_norm_w):
    h = x.astype(jnp.float32)
    for _ in range(DEPTH):
        xn = rms_norm(h, norm_w)
        p = jnp.einsum('bsd,dn->bsn', xn, w_in.astype(jnp.float32))
        p_tm = p[..., :SHIFT_COLS]
        prev = jnp.pad(p_tm, ((0, 0), (1, 0), (0, 0)))[:, :-1]
        p_tm = p_tm + (prev - p_tm) * mu_shift
        y_rwkv = rwkv7_time_mix(p_tm, p[..., RWKV_GATE0:ATT0], w0, w_up, a0, a_up,
                                k_k, k_a, r_k, ln_x_w, ln_x_b)
        y_att = mixed_dilated_attention(p[..., ATT0:ATT_GATE0], p[..., ATT_GATE0:])
        mix = jnp.concatenate([y_rwkv, y_att], axis=-1)
        h = h + jnp.einsum('bsm,md->bsd', mix, w_out.astype(jnp.float32))
    return rms_norm(h, final_norm_w).astype(x.dtype)
```

```python
import functools

import jax
import jax.numpy as jnp
from jax import lax
from jax.experimental import pallas as pl
from jax.experimental.pallas import tpu as pltpu

D_MODEL = 1024
HEAD_DIM = 64
RWKV_W = 1024
ATT_W = 1024
MIX_W = RWKV_W + ATT_W
LORA_RANK = 64
SHIFT_COLS = 3 * RWKV_W + 2 * LORA_RANK
N_IN = SHIFT_COLS + RWKV_W + 3 * ATT_W + ATT_W
DILATIONS = (1, 4, 16)
ATT_BLK = 128
RMS_EPS = 1e-5
GN_EPS = 64e-5

LANES = 128
PAIRS = RWKV_W // LANES
CHUNK = 64
TT = 256
R_BLK0, K_BLK0, V_BLK0 = 0, RWKV_W // LANES, 2 * RWKV_W // LANES
LORA_BLK = 3 * RWKV_W // LANES
RGATE_BLK0 = SHIFT_COLS // LANES
AQ_BLK0 = (SHIFT_COLS + RWKV_W) // LANES
AK_BLK0 = AQ_BLK0 + ATT_W // LANES
AV_BLK0 = AK_BLK0 + ATT_W // LANES
AGATE_BLK0 = AV_BLK0 + ATT_W // LANES

F32 = jnp.float32
BF16 = jnp.bfloat16
HI = lax.Precision.HIGHEST
VMEM_LIMIT = 48 * 1024 * 1024

NT_DIMS = (((1,), (1,)), ((), ()))
TN_DIMS = (((0,), (0,)), ((), ()))


def _mm(a, b):
    return jnp.dot(a.astype(BF16), b.astype(BF16), preferred_element_type=F32)


def _mm_nt(a, b):
    return lax.dot_general(a.astype(BF16), b.astype(BF16), NT_DIMS, preferred_element_type=F32)


def _mm_tn(a, b):
    return lax.dot_general(a.astype(BF16), b.astype(BF16), TN_DIMS, preferred_element_type=F32)


def _mm_hi(a, b):
    return jnp.dot(a, b, precision=HI, preferred_element_type=F32)


def _sigmoid(x):
    return 1.0 / (1.0 + jnp.exp(-x))


def _proj_kernel(x_ref, nw_ref, w_ref, o_ref, xn_ref):
    @pl.when(pl.program_id(1) == 0)
    def _():
        x = x_ref[...]
        ms = jnp.mean(x * x, axis=-1, keepdims=True)
        xn_ref[...] = (x * lax.rsqrt(ms + RMS_EPS) * nw_ref[...]).astype(BF16)

    o_ref[...] = jnp.dot(xn_ref[...], w_ref[...], preferred_element_type=F32)


def _in_projection(x2, norm_w, w_in_bf16, tm, tn):
    tokens = x2.shape[0]
    return pl.pallas_call(
        _proj_kernel,
        grid=(tokens // tm, N_IN // tn),
        in_specs=[
            pl.BlockSpec((tm, D_MODEL), lambda i, j: (i, 0)),
            pl.BlockSpec((1, D_MODEL), lambda i, j: (0, 0)),
            pl.BlockSpec((D_MODEL, tn), lambda i, j: (0, j)),
        ],
        out_specs=pl.BlockSpec((tm, tn), lambda i, j: (i, j)),
        out_shape=jax.ShapeDtypeStruct((tokens, N_IN), F32),
        scratch_shapes=[pltpu.VMEM((tm, D_MODEL), BF16)],
        compiler_params=pltpu.CompilerParams(
            dimension_semantics=("arbitrary", "arbitrary"), vmem_limit_bytes=VMEM_LIMIT),
        name="in_projection",
    )(x2, norm_w.reshape(1, D_MODEL), w_in_bf16)


def _rwkv_kernel(r_ref, k_ref, v_ref, lo_ref, gate_ref,
                 mur_ref, muk_ref, muv_ref, mulo_ref,
                 w0_ref, a0_ref, kk_ref, ka_ref, rk_ref, lnw_ref, lnb_ref,
                 wup_ref, aup_ref,
                 y_ref, state_ref, prev_ref):
    t = pl.program_id(2)

    @pl.when(t == 0)
    def _():
        state_ref[...] = jnp.zeros_like(state_ref)
        prev_ref[...] = jnp.zeros_like(prev_ref)

    row = lax.broadcasted_iota(jnp.int32, (TT, LANES), 0)

    def shifted(x_ref, mu_ref, slot):
        x = x_ref[...]
        last = prev_ref[slot, 7:8, :]
        xprev = jnp.where(row == 0, last, pltpu.roll(x, 1, 0))
        prev_ref[slot] = x[TT - 8:, :]
        return x + (xprev - x) * mu_ref[...]

    r = shifted(r_ref, mur_ref, 0)
    k = shifted(k_ref, muk_ref, 1)
    v = shifted(v_ref, muv_ref, 2)
    lo = shifted(lo_ref, mulo_ref, 3)

    w_pre = w0_ref[...] + _mm_hi(jnp.tanh(lo), wup_ref[...])
    z = -w_pre
    w = -(jnp.maximum(z, 0.0) + jnp.log(1.0 + jnp.exp(-jnp.abs(z)))) - 0.5
    lw = -jnp.exp(w)
    a = _sigmoid(a0_ref[...] + _mm_hi(lo, aup_ref[...]))

    li = lax.broadcasted_iota(jnp.int32, (LANES, LANES), 0)
    lj = lax.broadcasted_iota(jnp.int32, (LANES, LANES), 1)
    head_ones = jnp.where((li // HEAD_DIM) == (lj // HEAD_DIM), 1.0, 0.0).astype(F32)

    def head_sum(x):
        return _mm_hi(x, head_ones)

    kk = k * kk_ref[...]
    kk = kk / jnp.maximum(jnp.sqrt(head_sum(kk * kk)), 1e-12)
    k2 = k * (1.0 + (a - 1.0) * ka_ref[...])
    b = kk * a
    bonus = head_sum(r * k2 * rk_ref[...]) * v

    ti = lax.broadcasted_iota(jnp.int32, (TT, TT), 0)
    tj = lax.broadcasted_iota(jnp.int32, (TT, TT), 1)
    same_chunk = (ti // CHUNK) == (tj // CHUNK)
    lg = _mm_hi(jnp.where(same_chunk & (tj <= ti), 1.0, 0.0).astype(F32), lw)
    lg_end = _mm_hi(jnp.where(same_chunk, 1.0, 0.0).astype(F32), lw)

    a_t = -kk * jnp.exp(lg - lw)
    g_inv = jnp.exp(-lg)
    b_t = b * g_inv
    k_t = k2 * g_inv
    r_t = r * jnp.exp(lg)
    to_end = jnp.exp(lg_end - lg)
    b_h = b * to_end
    k_h = k2 * to_end
    g_end = jnp.exp(lg_end)

    head0 = lax.broadcasted_iota(jnp.int32, (CHUNK, LANES), 1) < HEAD_DIM

    def stack(x):
        return jnp.concatenate([jnp.where(head0, x, 0.0), jnp.where(head0, 0.0, x)], axis=0)

    n = 2 * CHUNK
    si = lax.broadcasted_iota(jnp.int32, (n, n), 0)
    sj = lax.broadcasted_iota(jnp.int32, (n, n), 1)
    same_head = (si // CHUNK) == (sj // CHUNK)
    strict = same_head & (sj < si)
    incl = same_head & (sj <= si)
    same16 = (si // 16) == (sj // 16)
    same32 = (si // 32) == (sj // 32)
    eye = jnp.where(si == sj, 1.0, 0.0).astype(F32)

    state = state_ref[...]
    ys = []
    for c in range(TT // CHUNK):
        rows = slice(c * CHUNK, (c + 1) * CHUNK)
        a_s, b_s, k_s, r_s = stack(a_t[rows]), stack(b_t[rows]), stack(k_t[rows]), stack(r_t[rows])
        v_s, bh_s, kh_s = stack(v[rows]), stack(b_h[rows]), stack(k_h[rows])

        g = _mm_nt(jnp.concatenate([a_s, r_s], axis=0), jnp.concatenate([b_s, k_s], axis=0))
        a_ab = jnp.where(strict, g[:n, :n], 0.0)
        a_ak = jnp.where(strict, g[:n, n:], 0.0)
        a_rb = jnp.where(incl, g[n:, :n], 0.0)
        a_rk = jnp.where(incl, g[n:, n:], 0.0)

        nil = jnp.where(same16, a_ab, 0.0)
        inv = eye + nil
        for _ in range(3):
            nil = _mm(nil, nil)
            inv = inv + _mm(inv, nil)
        inv = inv + _mm(_mm(inv, jnp.where(same32 & ~same16, a_ab, 0.0)), inv)
        inv = inv + _mm(_mm(inv, jnp.where(~same32, a_ab, 0.0)), inv)

        tx = _mm(inv, jnp.concatenate([a_s, _mm(a_ak, v_s)], axis=1))
        z = _mm(a_rb, tx)
        r_p = r_s + z[:, :LANES]
        y0 = z[:, LANES:] + _mm(a_rk, v_s)

        lh = _mm_nt(jnp.concatenate([tx[:, :LANES], r_p], axis=0), state)
        u = lh[:n] + tx[:, LANES:]
        y = lh[n:] + y0
        state = state * g_end[c * CHUNK:c * CHUNK + 1, :] + _mm_tn(
            jnp.concatenate([u, v_s], axis=0), jnp.concatenate([bh_s, kh_s], axis=0))
        ys.append(y[:CHUNK] + y[CHUNK:])
    state_ref[...] = state

    y = jnp.concatenate(ys, axis=0)
    mu = head_sum(y) * (1.0 / HEAD_DIM)
    yc = y - mu
    var = head_sum(yc * yc) * (1.0 / HEAD_DIM)
    yn = yc * lax.rsqrt(var + GN_EPS) * lnw_ref[...] + lnb_ref[...]
    gate = gate_ref[...]
    y_ref[...] = (yn + bonus) * (gate * _sigmoid(gate))


def _rwkv_time_mix(p, batch, seq, mu_shift, w0, w_up, a0, a_up, k_k, k_a, r_k, ln_x_w, ln_x_b):
    nt = seq // TT
    row2 = lambda v: v.reshape(1, -1).astype(F32)
    zeros = jnp.zeros((LORA_RANK, RWKV_W), F32)
    wup_pad = jnp.concatenate([w_up.astype(F32), zeros], axis=0)
    aup_pad = jnp.concatenate([zeros, a_up.astype(F32)], axis=0)

    def tile(blk0):
        return pl.BlockSpec((TT, LANES), lambda b, g, t: (b * nt + t, blk0 + g))

    def lane_param(blk0=0):
        return pl.BlockSpec((1, LANES), lambda b, g, t: (0, blk0 + g))

    fixed = lambda blk: pl.BlockSpec((1, LANES), lambda b, g, t: (0, blk))
    lora_w = pl.BlockSpec((2 * LORA_RANK, LANES), lambda b, g, t: (0, g))
    mu2 = row2(mu_shift)
    return pl.pallas_call(
        _rwkv_kernel,
        grid=(batch, PAIRS, nt),
        in_specs=[
            tile(R_BLK0), tile(K_BLK0), tile(V_BLK0),
            pl.BlockSpec((TT, LANES), lambda b, g, t: (b * nt + t, LORA_BLK)),
            tile(RGATE_BLK0),
            lane_param(R_BLK0), lane_param(K_BLK0), lane_param(V_BLK0), fixed(LORA_BLK),
            lane_param(), lane_param(), lane_param(), lane_param(), lane_param(), lane_param(),
            lane_param(),
            lora_w, lora_w,
        ],
        out_specs=pl.BlockSpec((TT, LANES), lambda b, g, t: (b * nt + t, g)),
        out_shape=jax.ShapeDtypeStruct((batch * seq, RWKV_W), F32),
        scratch_shapes=[pltpu.VMEM((LANES, LANES), F32), pltpu.VMEM((4, 8, LANES), F32)],
        compiler_params=pltpu.CompilerParams(
            dimension_semantics=("arbitrary", "arbitrary", "arbitrary"), vmem_limit_bytes=VMEM_LIMIT),
        name="rwkv7_time_mix",
    )(p, p, p, p, p, mu2, mu2, mu2, mu2,
      row2(w0), row2(a0), row2(k_k), row2(k_a), row2(r_k), row2(ln_x_w), row2(ln_x_b),
      wup_pad, aup_pad)


def _attn_kernel(q_ref, k_ref, v_ref, gate_ref, y_ref, o_scr, lse_scr, *, seq):
    n_blocks = seq // ATT_BLK
    qi = lax.broadcasted_iota(jnp.int32, (ATT_BLK, ATT_BLK), 0)
    kj = lax.broadcasted_iota(jnp.int32, (ATT_BLK, ATT_BLK), 1)
    lane = lax.broadcasted_iota(jnp.int32, (ATT_BLK, LANES), 1)
    head0 = lane < HEAD_DIM
    scale = HEAD_DIM ** -0.5
    neg = -1e30

    for pi, d in enumerate(DILATIONS):
        nb = n_blocks // d

        def block(idx, carry, pi=pi, d=d, nb=nb):
            res, blk = idx // nb, idx % nb
            start = blk * (ATT_BLK * d) + res
            prev_start = jnp.maximum(start - ATT_BLK * d, res)
            rows = lambda s: pl.ds(s, ATT_BLK, stride=d) if d > 1 else pl.ds(s, ATT_BLK)
            q = q_ref[rows(start), :] * scale
            k_cur, v_cur = k_ref[rows(start), :], v_ref[rows(start), :]
            k_prev, v_prev = k_ref[rows(prev_start), :], v_ref[rows(prev_start), :]
            ok_prev = (kj >= qi) & (blk > 0)
            ok_cur = kj <= qi
            outs, lses = [], []
            for h in range(2):
                qh = jnp.where(head0 if h == 0 else ~head0, q, 0.0)
                s_prev = jnp.where(ok_prev, _mm_nt(qh, k_prev), neg)
                s_cur = jnp.where(ok_cur, _mm_nt(qh, k_cur), neg)
                m = jnp.maximum(jnp.max(s_prev, axis=-1, keepdims=True),
                                jnp.max(s_cur, axis=-1, keepdims=True))
                p_prev, p_cur = jnp.exp(s_prev - m), jnp.exp(s_cur - m)
                l = jnp.sum(p_prev, axis=-1, keepdims=True) + jnp.sum(p_cur, axis=-1, keepdims=True)
                outs.append((_mm(p_prev, v_prev) + _mm(p_cur, v_cur)) / l)
                lses.append(m + jnp.log(l))
            o_scr[pi, rows(start), :] = jnp.where(head0, outs[0], outs[1])
            lse_scr[pi, rows(start), :] = jnp.where(head0, lses[0], lses[1])
            return carry

        lax.fori_loop(0, n_blocks, block, 0)

    l0, l1, l2 = lse_scr[0], lse_scr[1], lse_scr[2]
    m = jnp.maximum(jnp.maximum(l0, l1), l2)
    w0, w1, w2 = jnp.exp(l0 - m), jnp.exp(l1 - m), jnp.exp(l2 - m)
    o = (w0 * o_scr[0] + w1 * o_scr[1] + w2 * o_scr[2]) / (w0 + w1 + w2)
    gate = gate_ref[...]
    y_ref[...] = o * (gate * _sigmoid(gate))


def _dilated_attention(p, batch, seq):
    def tile(blk0):
        return pl.BlockSpec((seq, LANES), lambda b, g: (b, blk0 + g))

    return pl.pallas_call(
        functools.partial(_attn_kernel, seq=seq),
        grid=(batch, PAIRS),
        in_specs=[tile(AQ_BLK0), tile(AK_BLK0), tile(AV_BLK0), tile(AGATE_BLK0)],
        out_specs=pl.BlockSpec((seq, LANES), lambda b, g: (b, g)),
        out_shape=jax.ShapeDtypeStruct((batch * seq, ATT_W), F32),
        scratch_shapes=[pltpu.VMEM((len(DILATIONS), seq, LANES), F32),
                        pltpu.VMEM((len(DILATIONS), seq, LANES), F32)],
        compiler_params=pltpu.CompilerParams(
            dimension_semantics=("arbitrary", "arbitrary"), vmem_limit_bytes=VMEM_LIMIT),
        name="dilated_attention",
    )(p, p, p, p)


def _out_kernel(yr_ref, ya_ref, x_ref, wr_ref, wa_ref, fw_ref, o_ref):
    h = (x_ref[...]
         + jnp.dot(yr_ref[...].astype(BF16), wr_ref[...], preferred_element_type=F32)
         + jnp.dot(ya_ref[...].astype(BF16), wa_ref[...], preferred_element_type=F32))
    ms = jnp.mean(h * h, axis=-1, keepdims=True)
    o_ref[...] = h * lax.rsqrt(ms + RMS_EPS) * fw_ref[...]


def _out_projection(y_rwkv, y_att, x2, w_out_bf16, final_norm_w, tm):
    tokens = x2.shape[0]
    tok = lambda width: pl.BlockSpec((tm, width), lambda i: (i, 0))
    return pl.pallas_call(
        _out_kernel,
        grid=(tokens // tm,),
        in_specs=[
            tok(RWKV_W), tok(ATT_W), tok(D_MODEL),
            pl.BlockSpec((RWKV_W, D_MODEL), lambda i: (0, 0)),
            pl.BlockSpec((ATT_W, D_MODEL), lambda i: (1, 0)),
            pl.BlockSpec((1, D_MODEL), lambda i: (0, 0)),
        ],
        out_specs=tok(D_MODEL),
        out_shape=jax.ShapeDtypeStruct((tokens, D_MODEL), F32),
        compiler_params=pltpu.CompilerParams(
            dimension_semantics=("arbitrary",), vmem_limit_bytes=VMEM_LIMIT),
        name="out_projection",
    )(y_rwkv, y_att, x2, w_out_bf16, w_out_bf16, final_norm_w.reshape(1, D_MODEL))


def kernel(x, norm_w, w_in, mu_shift, w0, w_up, a0, a_up, k_k, k_a, r_k, ln_x_w, ln_x_b, w_out,
           final_norm_w):
    batch, seq, _ = x.shape
    assert seq % (ATT_BLK * max(DILATIONS)) == 0 and seq % TT == 0
    x2 = x.astype(F32).reshape(batch * seq, D_MODEL)
    p = _in_projection(x2, norm_w.astype(F32), w_in.astype(BF16), tm=min(1024, batch * seq), tn=1664)
    y_rwkv = _rwkv_time_mix(p, batch, seq, mu_shift, w0, w_up, a0, a_up, k_k, k_a, r_k, ln_x_w, ln_x_b)
    y_att = _dilated_attention(p, batch, seq)
    out = _out_projection(y_rwkv, y_att, x2, w_out.astype(BF16), final_norm_w.astype(F32),
                          tm=min(512, batch * seq))
    return out.reshape(batch, seq, D_MODEL).astype(x.dtype)
```

```python
import functools

import jax
import jax.numpy as jnp
from jax import lax
from jax.experimental import pallas as pl
from jax.experimental.pallas import tpu as pltpu

D_MODEL = 1024
HEAD_DIM = 64
RWKV_W = 1024
ATT_W = 1024
MIX_W = RWKV_W + ATT_W
LORA_RANK = 64
SHIFT_COLS = 3 * RWKV_W + 2 * LORA_RANK
N_IN = SHIFT_COLS + RWKV_W + 3 * ATT_W + ATT_W
DILATIONS = (1, 4, 16)
ATT_BLK = 128
RMS_EPS = 1e-5
GN_EPS = 64e-5

LANES = 128
SUBLANES = 8
PAIRS = RWKV_W // LANES
CHUNK = 64
K_COL0, V_COL0, LORA_COL0 = RWKV_W, 2 * RWKV_W, 3 * RWKV_W
RGATE_COL0 = SHIFT_COLS
RWKV_COLS = SHIFT_COLS + RWKV_W
AQ_BLK0 = RWKV_COLS // LANES
AK_BLK0 = AQ_BLK0 + ATT_W // LANES
AV_BLK0 = AK_BLK0 + ATT_W // LANES
AGATE_BLK0 = AV_BLK0 + ATT_W // LANES

F32 = jnp.float32
BF16 = jnp.bfloat16
VMEM_LIMIT = 48 * 1024 * 1024

NT_DIMS = (((1,), (1,)), ((), ()))
TN_DIMS = (((0,), (0,)), ((), ()))


def _mm(a, b):
    return jnp.dot(a.astype(BF16), b.astype(BF16), preferred_element_type=F32)


def _mm_nt(a, b):
    return lax.dot_general(a.astype(BF16), b.astype(BF16), NT_DIMS, preferred_element_type=F32)


def _mm_tn(a, b):
    return lax.dot_general(a.astype(BF16), b.astype(BF16), TN_DIMS, preferred_element_type=F32)


def _split(x):
    hi = x.astype(BF16)
    return hi, (x - hi.astype(F32)).astype(BF16)


def _mm_split_lhs(x, w_bf16):
    hi, lo = _split(x)
    return (jnp.dot(hi, w_bf16, preferred_element_type=F32)
            + jnp.dot(lo, w_bf16, preferred_element_type=F32))


def _mm_x3(x, w_hi, w_lo):
    hi, lo = _split(x)
    return (jnp.dot(hi, w_hi, preferred_element_type=F32)
            + jnp.dot(lo, w_hi, preferred_element_type=F32)
            + jnp.dot(hi, w_lo, preferred_element_type=F32))


def _sigmoid(x):
    return 1.0 / (1.0 + jnp.exp(-x))


def _proj_kernel(x_ref, nw_ref, w_ref, o_ref, xn_ref):
    @pl.when(pl.program_id(1) == 0)
    def _():
        x = x_ref[...]
        ms = jnp.mean(x * x, axis=-1, keepdims=True)
        xn_ref[...] = (x * lax.rsqrt(ms + RMS_EPS) * nw_ref[...]).astype(BF16)

    o_ref[...] = jnp.dot(xn_ref[...], w_ref[...], preferred_element_type=F32)


def _in_projection(x2, norm_w, w_in_bf16, tm, tn):
    tokens = x2.shape[0]
    return pl.pallas_call(
        _proj_kernel,
        grid=(tokens // tm, N_IN // tn),
        in_specs=[
            pl.BlockSpec((tm, D_MODEL), lambda i, j: (i, 0)),
            pl.BlockSpec((1, D_MODEL), lambda i, j: (0, 0)),
            pl.BlockSpec((D_MODEL, tn), lambda i, j: (0, j)),
        ],
        out_specs=pl.BlockSpec((tm, tn), lambda i, j: (i, j)),
        out_shape=jax.ShapeDtypeStruct((tokens, N_IN), F32),
        scratch_shapes=[pltpu.VMEM((tm, D_MODEL), BF16)],
        compiler_params=pltpu.CompilerParams(
            dimension_semantics=("arbitrary", "arbitrary"), vmem_limit_bytes=VMEM_LIMIT),
        name="in_projection",
    )(x2, norm_w.reshape(1, D_MODEL), w_in_bf16)


def _rwkv_kernel(p_ref, mu_ref, w0_ref, a0_ref, kk_ref, ka_ref, rk_ref, lnw_ref, lnb_ref,
                 wuh_ref, wul_ref, auh_ref, aul_ref, y_ref, state_ref, prev_ref):
    @pl.when(pl.program_id(1) == 0)
    def _():
        state_ref[...] = jnp.zeros_like(state_ref)
        prev_ref[...] = jnp.zeros_like(prev_ref)

    row = lax.broadcasted_iota(jnp.int32, (CHUNK, LANES), 0)
    lane = lax.broadcasted_iota(jnp.int32, (CHUNK, LANES), 1)
    head0 = lane < HEAD_DIM
    tcol = lane % CHUNK
    strict, incl = tcol < row, tcol <= row
    same16 = (tcol // 16) == (row // 16)
    same32 = (tcol // 32) == (row // 32)
    eye = jnp.where(tcol == row, 1.0, 0.0).astype(F32)
    bi = lax.broadcasted_iota(jnp.int32, (LANES, LANES), 0)
    bj = lax.broadcasted_iota(jnp.int32, (LANES, LANES), 1)
    block_diag = (bi // HEAD_DIM) == (bj // HEAD_DIM)
    head_ones = jnp.where(block_diag, 1.0, 0.0).astype(BF16)
    ti = lax.broadcasted_iota(jnp.int32, (CHUNK, CHUNK), 0)
    tj = lax.broadcasted_iota(jnp.int32, (CHUNK, CHUNK), 1)
    tri = jnp.where(tj <= ti, 1.0, 0.0).astype(BF16)
    units = range(PAIRS)
    cols = lambda c0, g: slice(c0 + g * LANES, c0 + (g + 1) * LANES)

    def shifted(sl):
        x = p_ref[:, sl]
        xprev = jnp.where(row == 0, prev_ref[SUBLANES - 1:SUBLANES, sl], pltpu.roll(x, 1, 0))
        return x + (xprev - x) * mu_ref[:, sl]

    def stack(x):
        return jnp.concatenate([jnp.where(head0, x, 0.0), jnp.where(head0, 0.0, x)], axis=0).astype(BF16)

    def head_sums(xs):
        s = _mm_split_lhs(jnp.concatenate(xs, axis=0), head_ones)
        return [s[g * CHUNK:(g + 1) * CHUNK] for g in units]

    lo = shifted(slice(LORA_COL0, LORA_COL0 + LANES))
    z = -(w0_ref[...] + _mm_x3(jnp.tanh(lo), wuh_ref[...], wul_ref[...]))
    w = -(jnp.maximum(z, 0.0) + jnp.log(1.0 + jnp.exp(-jnp.abs(z)))) - 0.5
    lw = -jnp.exp(w)
    iclr = _sigmoid(a0_ref[...] + _mm_x3(lo, auh_ref[...], aul_ref[...]))
    lw_hi, lw_lo = _split(lw)
    lg = (jnp.dot(tri, lw_hi, preferred_element_type=F32)
          + jnp.dot(tri, lw_lo, preferred_element_type=F32))
    lg_end = lg[CHUNK - 1:CHUNK, :]
    e_prev = jnp.exp(lg - lw)
    e_inv = jnp.exp(-lg)
    e_fwd = jnp.exp(lg)
    e_end = jnp.exp(lg_end - lg)
    g_end = jnp.exp(lg_end)

    r = [shifted(cols(0, g)) for g in units]
    k = [shifted(cols(K_COL0, g)) for g in units]
    v = [shifted(cols(V_COL0, g)) for g in units]
    kk = [k[g] * kk_ref[:, cols(0, g)] for g in units]
    kk_n2 = head_sums([x * x for x in kk])
    kk = [kk[g] / jnp.maximum(jnp.sqrt(kk_n2[g]), 1e-12) for g in units]
    a = [iclr[:, cols(0, g)] for g in units]
    k2 = [k[g] * (1.0 + (a[g] - 1.0) * ka_ref[:, cols(0, g)]) for g in units]
    rk_sum = head_sums([r[g] * k2[g] * rk_ref[:, cols(0, g)] for g in units])
    bonus = [rk_sum[g] * v[g] for g in units]

    a_t = [-kk[g] * e_prev[:, cols(0, g)] for g in units]
    b_t = [kk[g] * a[g] * e_inv[:, cols(0, g)] for g in units]
    k_t = [k2[g] * e_inv[:, cols(0, g)] for g in units]
    r_t = [r[g] * e_fwd[:, cols(0, g)] for g in units]
    b_h = [kk[g] * a[g] * e_end[:, cols(0, g)] for g in units]
    k_h = [k2[g] * e_end[:, cols(0, g)] for g in units]

    v_s = [stack(x) for x in v]
    gram = [_mm_nt(jnp.concatenate([a_t[g], r_t[g]], axis=0),
                   jnp.concatenate([stack(b_t[g]), stack(k_t[g])], axis=0)) for g in units]
    a_ab = [jnp.where(strict, x[:CHUNK, :LANES], 0.0) for x in gram]
    a_ak = [jnp.where(strict, x[:CHUNK, LANES:], 0.0) for x in gram]
    a_rb = [jnp.where(incl, x[CHUNK:, :LANES], 0.0) for x in gram]
    a_rk = [jnp.where(incl, x[CHUNK:, LANES:], 0.0) for x in gram]

    nil = [jnp.where(same16, x, 0.0) for x in a_ab]
    inv = [eye + x for x in nil]
    for _ in range(3):
        nil = [_mm(x, stack(x)) for x in nil]
        inv = [inv[g] + _mm(inv[g], stack(nil[g])) for g in units]
    for off_diag in (same32 & ~same16, ~same32):
        low = [_mm(inv[g], stack(jnp.where(off_diag, a_ab[g], 0.0))) for g in units]
        inv = [inv[g] + _mm(low[g], stack(inv[g])) for g in units]

    akv = [_mm(a_ak[g], v_s[g]) for g in units]
    tx = [_mm(inv[g], jnp.concatenate([stack(a_t[g]), stack(akv[g])], axis=1)) for g in units]
    zz = [_mm(a_rb[g], jnp.concatenate([stack(tx[g][:, :LANES]), stack(tx[g][:, LANES:])], axis=1))
          for g in units]
    r_p = [r_t[g] + zz[g][:, :LANES] for g in units]
    y0 = [zz[g][:, LANES:] + _mm(a_rk[g], v_s[g]) for g in units]
    m_t = [jnp.where(block_diag, _mm_tn(tx[g][:, :LANES], b_h[g]), 0.0) for g in units]
    n_t = [jnp.where(block_diag, _mm_tn(jnp.concatenate([tx[g][:, LANES:], v[g]], axis=0),
                                        jnp.concatenate([b_h[g], k_h[g]], axis=0)), 0.0) for g in units]
    state = [state_ref[g] for g in units]
    y = [_mm_nt(r_p[g], state[g]) + y0[g] for g in units]
    for g in units:
        state_ref[g] = state[g] * g_end[:, cols(0, g)] + _mm(state[g], m_t[g]) + n_t[g]

    mean = head_sums(y)
    yc = [y[g] - mean[g] * (1.0 / HEAD_DIM) for g in units]
    var = head_sums([x * x for x in yc])
    for g in units:
        yn = yc[g] * lax.rsqrt(var[g] * (1.0 / HEAD_DIM) + GN_EPS)
        yn = yn * lnw_ref[:, cols(0, g)] + lnb_ref[:, cols(0, g)]
        gate = p_ref[:, cols(RGATE_COL0, g)]
        y_ref[:, cols(0, g)] = (yn + bonus[g]) * (gate * _sigmoid(gate))
    prev_ref[...] = p_ref[CHUNK - SUBLANES:, :SHIFT_COLS]


def _rwkv_time_mix(p, batch, seq, mu_shift, w0, w_up, a0, a_up, k_k, k_a, r_k, ln_x_w, ln_x_b):
    nt = seq // CHUNK
    row2 = lambda v: v.reshape(1, -1).astype(F32)
    zeros = jnp.zeros((LORA_RANK, RWKV_W), F32)
    wup_pad = jnp.concatenate([w_up.astype(F32), zeros], axis=0)
    aup_pad = jnp.concatenate([zeros, a_up.astype(F32)], axis=0)
    wu_hi, wu_lo = _split(wup_pad)
    au_hi, au_lo = _split(aup_pad)
    whole = lambda shape: pl.BlockSpec(shape, lambda b, t: (0, 0))
    chan = whole((1, RWKV_W))
    lora = whole((2 * LORA_RANK, RWKV_W))
    return pl.pallas_call(
        _rwkv_kernel,
        grid=(batch, nt),
        in_specs=[pl.BlockSpec((CHUNK, RWKV_COLS), lambda b, t: (b * nt + t, 0)),
                  whole((1, SHIFT_COLS)), chan, chan, chan, chan, chan, chan, chan,
                  lora, lora, lora, lora],
        out_specs=pl.BlockSpec((CHUNK, RWKV_W), lambda b, t: (b * nt + t, 0)),
        out_shape=jax.ShapeDtypeStruct((batch * seq, RWKV_W), F32),
        scratch_shapes=[pltpu.VMEM((PAIRS, LANES, LANES), F32), pltpu.VMEM((SUBLANES, SHIFT_COLS), F32)],
        compiler_params=pltpu.CompilerParams(
            dimension_semantics=("arbitrary", "arbitrary"), vmem_limit_bytes=VMEM_LIMIT),
        name="rwkv7_time_mix",
    )(p, row2(mu_shift), row2(w0), row2(a0), row2(k_k), row2(k_a), row2(r_k), row2(ln_x_w), row2(ln_x_b),
      wu_hi, wu_lo, au_hi, au_lo)


def _attn_kernel(q_ref, k_ref, v_ref, gate_ref, y_ref, o_scr, lse_scr, *, seq):
    n_blocks = seq // ATT_BLK
    qi = lax.broadcasted_iota(jnp.int32, (ATT_BLK, ATT_BLK), 0)
    kj = lax.broadcasted_iota(jnp.int32, (ATT_BLK, ATT_BLK), 1)
    lane = lax.broadcasted_iota(jnp.int32, (ATT_BLK, LANES), 1)
    head0 = lane < HEAD_DIM
    scale = HEAD_DIM ** -0.5
    neg = -1e30

    for pi, d in enumerate(DILATIONS):
        nb = n_blocks // d

        def block(idx, carry, pi=pi, d=d, nb=nb):
            res, blk = idx // nb, idx % nb
            start = blk * (ATT_BLK * d) + res
            prev_start = jnp.maximum(start - ATT_BLK * d, res)
            rows = lambda s: pl.ds(s, ATT_BLK, stride=d) if d > 1 else pl.ds(s, ATT_BLK)
            q = q_ref[rows(start), :] * scale
            k_cur, v_cur = k_ref[rows(start), :], v_ref[rows(start), :]
            k_prev, v_prev = k_ref[rows(prev_start), :], v_ref[rows(prev_start), :]
            ok_prev = (kj >= qi) & (blk > 0)
            ok_cur = kj <= qi
            outs, lses = [], []
            for h in range(2):
                qh = jnp.where(head0 if h == 0 else ~head0, q, 0.0)
                s_prev = jnp.where(ok_prev, _mm_nt(qh, k_prev), neg)
                s_cur = jnp.where(ok_cur, _mm_nt(qh, k_cur), neg)
                m = jnp.maximum(jnp.max(s_prev, axis=-1, keepdims=True),
                                jnp.max(s_cur, axis=-1, keepdims=True))
                p_prev, p_cur = jnp.exp(s_prev - m), jnp.exp(s_cur - m)
                l = jnp.sum(p_prev, axis=-1, keepdims=True) + jnp.sum(p_cur, axis=-1, keepdims=True)
                outs.append((_mm(p_prev, v_prev) + _mm(p_cur, v_cur)) / l)
                lses.append(m + jnp.log(l))
            o_scr[pi, rows(start), :] = jnp.where(head0, outs[0], outs[1])
            lse_scr[pi, rows(start), :] = jnp.where(head0, lses[0], lses[1])
            return carry

        lax.fori_loop(0, n_blocks, block, 0)

    l0, l1, l2 = lse_scr[0], lse_scr[1], lse_scr[2]
    m = jnp.maximum(jnp.maximum(l0, l1), l2)
    w0, w1, w2 = jnp.exp(l0 - m), jnp.exp(l1 - m), jnp.exp(l2 - m)
    o = (w0 * o_scr[0] + w1 * o_scr[1] + w2 * o_scr[2]) / (w0 + w1 + w2)
    gate = gate_ref[...]
    y_ref[...] = o * (gate * _sigmoid(gate))


def _dilated_attention(p, batch, seq):
    def tile(blk0):
        return pl.BlockSpec((seq, LANES), lambda b, g: (b, blk0 + g))

    return pl.pallas_call(
        functools.partial(_attn_kernel, seq=seq),
        grid=(batch, PAIRS),
        in_specs=[tile(AQ_BLK0), tile(AK_BLK0), tile(AV_BLK0), tile(AGATE_BLK0)],
        out_specs=pl.BlockSpec((seq, LANES), lambda b, g: (b, g)),
        out_shape=jax.ShapeDtypeStruct((batch * seq, ATT_W), F32),
        scratch_shapes=[pltpu.VMEM((len(DILATIONS), seq, LANES), F32),
                        pltpu.VMEM((len(DILATIONS), seq, LANES), F32)],
        compiler_params=pltpu.CompilerParams(
            dimension_semantics=("arbitrary", "arbitrary"), vmem_limit_bytes=VMEM_LIMIT),
        name="dilated_attention",
    )(p, p, p, p)


def _out_kernel(yr_ref, ya_ref, x_ref, wr_ref, wa_ref, fw_ref, o_ref):
    h = (x_ref[...]
         + jnp.dot(yr_ref[...].astype(BF16), wr_ref[...], preferred_element_type=F32)
         + jnp.dot(ya_ref[...].astype(BF16), wa_ref[...], preferred_element_type=F32))
    ms = jnp.mean(h * h, axis=-1, keepdims=True)
    o_ref[...] = h * lax.rsqrt(ms + RMS_EPS) * fw_ref[...]


def _out_projection(y_rwkv, y_att, x2, w_out_bf16, final_norm_w, tm):
    tokens = x2.shape[0]
    tok = lambda width: pl.BlockSpec((tm, width), lambda i: (i, 0))
    return pl.pallas_call(
        _out_kernel,
        grid=(tokens // tm,),
        in_specs=[
            tok(RWKV_W), tok(ATT_W), tok(D_MODEL),
            pl.BlockSpec((RWKV_W, D_MODEL), lambda i: (0, 0)),
            pl.BlockSpec((ATT_W, D_MODEL), lambda i: (1, 0)),
            pl.BlockSpec((1, D_MODEL), lambda i: (0, 0)),
        ],
        out_specs=tok(D_MODEL),
        out_shape=jax.ShapeDtypeStruct((tokens, D_MODEL), F32),
        compiler_params=pltpu.CompilerParams(
            dimension_semantics=("arbitrary",), vmem_limit_bytes=VMEM_LIMIT),
        name="out_projection",
    )(y_rwkv, y_att, x2, w_out_bf16, w_out_bf16, final_norm_w.reshape(1, D_MODEL))


def kernel(x, norm_w, w_in, mu_shift, w0, w_up, a0, a_up, k_k, k_a, r_k, ln_x_w, ln_x_b, w_out,
           final_norm_w):
    batch, seq, _ = x.shape
    assert seq % (ATT_BLK * max(DILATIONS)) == 0 and seq % CHUNK == 0
    x2 = x.astype(F32).reshape(batch * seq, D_MODEL)
    p = _in_projection(x2, norm_w.astype(F32), w_in.astype(BF16), tm=min(1024, batch * seq), tn=1664)
    y_rwkv = _rwkv_time_mix(p, batch, seq, mu_shift, w0, w_up, a0, a_up, k_k, k_a, r_k, ln_x_w, ln_x_b)
    y_att = _dilated_attention(p, batch, seq)
    out = _out_projection(y_rwkv, y_att, x2, w_out.astype(BF16), final_norm_w.astype(F32),
                          tm=min(512, batch * seq))
    return out.reshape(batch, seq, D_MODEL).astype(x.dtype)
```

```python
import functools

import jax
import jax.numpy as jnp
from jax import lax
from jax.experimental import pallas as pl
from jax.experimental.pallas import tpu as pltpu

D_MODEL = 1024
HEAD_DIM = 64
RWKV_W = 1024
ATT_W = 1024
MIX_W = RWKV_W + ATT_W
LORA_RANK = 64
SHIFT_COLS = 3 * RWKV_W + 2 * LORA_RANK
N_IN = SHIFT_COLS + RWKV_W + 3 * ATT_W + ATT_W
DILATIONS = (1, 4, 16)
ATT_BLK = 128
ATT_GROUP = 4
RMS_EPS = 1e-5
GN_EPS = 64e-5

LANES = 128
SUBLANES = 8
PAIRS = RWKV_W // LANES
CHUNK = 64
K_COL0, V_COL0, LORA_COL0 = RWKV_W, 2 * RWKV_W, 3 * RWKV_W
RGATE_COL0 = SHIFT_COLS
RWKV_COLS = SHIFT_COLS + RWKV_W
AQ_BLK0 = RWKV_COLS // LANES
AK_BLK0 = AQ_BLK0 + ATT_W // LANES
AV_BLK0 = AK_BLK0 + ATT_W // LANES
AGATE_BLK0 = AV_BLK0 + ATT_W // LANES

F32 = jnp.float32
BF16 = jnp.bfloat16
VMEM_LIMIT = 48 * 1024 * 1024

NT_DIMS = (((1,), (1,)), ((), ()))
TN_DIMS = (((0,), (0,)), ((), ()))


def _mm(a, b):
    return jnp.dot(a.astype(BF16), b.astype(BF16), preferred_element_type=F32)


def _mm_nt(a, b):
    return lax.dot_general(a.astype(BF16), b.astype(BF16), NT_DIMS, preferred_element_type=F32)


def _mm_tn(a, b):
    return lax.dot_general(a.astype(BF16), b.astype(BF16), TN_DIMS, preferred_element_type=F32)


def _split(x):
    hi = x.astype(BF16)
    return hi, (x - hi.astype(F32)).astype(BF16)


def _mm_split_lhs(x, w_bf16):
    hi, lo = _split(x)
    return (jnp.dot(hi, w_bf16, preferred_element_type=F32)
            + jnp.dot(lo, w_bf16, preferred_element_type=F32))


def _mm_x3(x, w_hi, w_lo):
    hi, lo = _split(x)
    return (jnp.dot(hi, w_hi, preferred_element_type=F32)
            + jnp.dot(lo, w_hi, preferred_element_type=F32)
            + jnp.dot(hi, w_lo, preferred_element_type=F32))


def _sigmoid(x):
    return 1.0 / (1.0 + jnp.exp(-x))


def _proj_kernel(x_ref, nw_ref, w_ref, o_ref, xn_ref):
    @pl.when(pl.program_id(1) == 0)
    def _():
        x = x_ref[...]
        ms = jnp.mean(x * x, axis=-1, keepdims=True)
        xn_ref[...] = (x * lax.rsqrt(ms + RMS_EPS) * nw_ref[...]).astype(BF16)

    o_ref[...] = jnp.dot(xn_ref[...], w_ref[...], preferred_element_type=F32)


def _in_projection(x2, norm_w, w_in_bf16, tm, tn):
    tokens = x2.shape[0]
    return pl.pallas_call(
        _proj_kernel,
        grid=(tokens // tm, N_IN // tn),
        in_specs=[
            pl.BlockSpec((tm, D_MODEL), lambda i, j: (i, 0)),
            pl.BlockSpec((1, D_MODEL), lambda i, j: (0, 0)),
            pl.BlockSpec((D_MODEL, tn), lambda i, j: (0, j)),
        ],
        out_specs=pl.BlockSpec((tm, tn), lambda i, j: (i, j)),
        out_shape=jax.ShapeDtypeStruct((tokens, N_IN), F32),
        scratch_shapes=[pltpu.VMEM((tm, D_MODEL), BF16)],
        compiler_params=pltpu.CompilerParams(
            dimension_semantics=("arbitrary", "arbitrary"), vmem_limit_bytes=VMEM_LIMIT),
        name="in_projection",
    )(x2, norm_w.reshape(1, D_MODEL), w_in_bf16)


def _rwkv_kernel(p_ref, mu_ref, w0_ref, a0_ref, kk_ref, ka_ref, rk_ref, lnw_ref, lnb_ref,
                 wuh_ref, wul_ref, auh_ref, aul_ref, y_ref, state_ref, prev_ref):
    @pl.when(pl.program_id(1) == 0)
    def _():
        state_ref[...] = jnp.zeros_like(state_ref)
        prev_ref[...] = jnp.zeros_like(prev_ref)

    row = lax.broadcasted_iota(jnp.int32, (CHUNK, LANES), 0)
    lane = lax.broadcasted_iota(jnp.int32, (CHUNK, LANES), 1)
    head0 = lane < HEAD_DIM
    tcol = lane % CHUNK
    strict, incl = tcol < row, tcol <= row
    same16 = (tcol // 16) == (row // 16)
    same32 = (tcol // 32) == (row // 32)
    eye = jnp.where(tcol == row, 1.0, 0.0).astype(F32)
    bi = lax.broadcasted_iota(jnp.int32, (LANES, LANES), 0)
    bj = lax.broadcasted_iota(jnp.int32, (LANES, LANES), 1)
    block_diag = (bi // HEAD_DIM) == (bj // HEAD_DIM)
    head_ones = jnp.where(block_diag, 1.0, 0.0).astype(BF16)
    ti = lax.broadcasted_iota(jnp.int32, (CHUNK, CHUNK), 0)
    tj = lax.broadcasted_iota(jnp.int32, (CHUNK, CHUNK), 1)
    tri = jnp.where(tj <= ti, 1.0, 0.0).astype(BF16)
    units = range(PAIRS)
    cols = lambda c0, g: slice(c0 + g * LANES, c0 + (g + 1) * LANES)

    def shifted(sl):
        x = p_ref[:, sl]
        xprev = jnp.where(row == 0, prev_ref[SUBLANES - 1:SUBLANES, sl], pltpu.roll(x, 1, 0))
        return x + (xprev - x) * mu_ref[:, sl]

    def stack(x):
        return jnp.concatenate([jnp.where(head0, x, 0.0), jnp.where(head0, 0.0, x)], axis=0).astype(BF16)

    def head_sums(xs):
        s = _mm_split_lhs(jnp.concatenate(xs, axis=0), head_ones)
        return [s[g * CHUNK:(g + 1) * CHUNK] for g in units]

    lo = shifted(slice(LORA_COL0, LORA_COL0 + LANES))
    z = -(w0_ref[...] + _mm_x3(jnp.tanh(lo), wuh_ref[...], wul_ref[...]))
    w = -(jnp.maximum(z, 0.0) + jnp.log(1.0 + jnp.exp(-jnp.abs(z)))) - 0.5
    lw = -jnp.exp(w)
    iclr = _sigmoid(a0_ref[...] + _mm_x3(lo, auh_ref[...], aul_ref[...]))
    lw_hi, lw_lo = _split(lw)
    lg = (jnp.dot(tri, lw_hi, preferred_element_type=F32)
          + jnp.dot(tri, lw_lo, preferred_element_type=F32))
    lg_end = lg[CHUNK - 1:CHUNK, :]
    e_prev = jnp.exp(lg - lw)
    e_inv = jnp.exp(-lg)
    e_fwd = jnp.exp(lg)
    e_end = jnp.exp(lg_end - lg)
    g_end = jnp.exp(lg_end)

    r = [shifted(cols(0, g)) for g in units]
    k = [shifted(cols(K_COL0, g)) for g in units]
    v = [shifted(cols(V_COL0, g)) for g in units]
    kk = [k[g] * kk_ref[:, cols(0, g)] for g in units]
    kk_n2 = head_sums([x * x for x in kk])
    kk = [kk[g] / jnp.maximum(jnp.sqrt(kk_n2[g]), 1e-12) for g in units]
    a = [iclr[:, cols(0, g)] for g in units]
    k2 = [k[g] * (1.0 + (a[g] - 1.0) * ka_ref[:, cols(0, g)]) for g in units]
    rk_sum = head_sums([r[g] * k2[g] * rk_ref[:, cols(0, g)] for g in units])
    bonus = [rk_sum[g] * v[g] for g in units]

    a_t = [-kk[g] * e_prev[:, cols(0, g)] for g in units]
    b_t = [kk[g] * a[g] * e_inv[:, cols(0, g)] for g in units]
    k_t = [k2[g] * e_inv[:, cols(0, g)] for g in units]
    r_t = [r[g] * e_fwd[:, cols(0, g)] for g in units]
    b_h = [kk[g] * a[g] * e_end[:, cols(0, g)] for g in units]
    k_h = [k2[g] * e_end[:, cols(0, g)] for g in units]

    v_s = [stack(x) for x in v]
    gram = [_mm_nt(jnp.concatenate([a_t[g], r_t[g]], axis=0),
                   jnp.concatenate([stack(b_t[g]), stack(k_t[g])], axis=0)) for g in units]
    a_ab = [jnp.where(strict, x[:CHUNK, :LANES], 0.0) for x in gram]
    a_ak = [jnp.where(strict, x[:CHUNK, LANES:], 0.0) for x in gram]
    a_rb = [jnp.where(incl, x[CHUNK:, :LANES], 0.0) for x in gram]
    a_rk = [jnp.where(incl, x[CHUNK:, LANES:], 0.0) for x in gram]

    nil = [jnp.where(same16, x, 0.0) for x in a_ab]
    inv = [eye + x for x in nil]
    for _ in range(3):
        nil = [_mm(x, stack(x)) for x in nil]
        inv = [inv[g] + _mm(inv[g], stack(nil[g])) for g in units]
    for off_diag in (same32 & ~same16, ~same32):
        low = [_mm(inv[g], stack(jnp.where(off_diag, a_ab[g], 0.0))) for g in units]
        inv = [inv[g] + _mm(low[g], stack(inv[g])) for g in units]

    akv = [_mm(a_ak[g], v_s[g]) for g in units]
    tx = [_mm(inv[g], jnp.concatenate([stack(a_t[g]), stack(akv[g])], axis=1)) for g in units]
    zz = [_mm(a_rb[g], jnp.concatenate([stack(tx[g][:, :LANES]), stack(tx[g][:, LANES:])], axis=1))
          for g in units]
    r_p = [r_t[g] + zz[g][:, :LANES] for g in units]
    y0 = [zz[g][:, LANES:] + _mm(a_rk[g], v_s[g]) for g in units]
    m_t = [jnp.where(block_diag, _mm_tn(tx[g][:, :LANES], b_h[g]), 0.0) for g in units]
    n_t = [jnp.where(block_diag, _mm_tn(jnp.concatenate([tx[g][:, LANES:], v[g]], axis=0),
                                        jnp.concatenate([b_h[g], k_h[g]], axis=0)), 0.0) for g in units]
    state = [state_ref[g] for g in units]
    y = [_mm_nt(r_p[g], state[g]) + y0[g] for g in units]
    for g in units:
        state_ref[g] = state[g] * g_end[:, cols(0, g)] + _mm(state[g], m_t[g]) + n_t[g]

    mean = head_sums(y)
    yc = [y[g] - mean[g] * (1.0 / HEAD_DIM) for g in units]
    var = head_sums([x * x for x in yc])
    for g in units:
        yn = yc[g] * lax.rsqrt(var[g] * (1.0 / HEAD_DIM) + GN_EPS)
        yn = yn * lnw_ref[:, cols(0, g)] + lnb_ref[:, cols(0, g)]
        gate = p_ref[:, cols(RGATE_COL0, g)]
        y_ref[:, cols(0, g)] = (yn + bonus[g]) * (gate * _sigmoid(gate))
    prev_ref[...] = p_ref[CHUNK - SUBLANES:, :SHIFT_COLS]


def _rwkv_time_mix(p, batch, seq, mu_shift, w0, w_up, a0, a_up, k_k, k_a, r_k, ln_x_w, ln_x_b):
    nt = seq // CHUNK
    row2 = lambda v: v.reshape(1, -1).astype(F32)
    zeros = jnp.zeros((LORA_RANK, RWKV_W), F32)
    wup_pad = jnp.concatenate([w_up.astype(F32), zeros], axis=0)
    aup_pad = jnp.concatenate([zeros, a_up.astype(F32)], axis=0)
    wu_hi, wu_lo = _split(wup_pad)
    au_hi, au_lo = _split(aup_pad)
    whole = lambda shape: pl.BlockSpec(shape, lambda b, t: (0, 0))
    chan = whole((1, RWKV_W))
    lora = whole((2 * LORA_RANK, RWKV_W))
    return pl.pallas_call(
        _rwkv_kernel,
        grid=(batch, nt),
        in_specs=[pl.BlockSpec((CHUNK, RWKV_COLS), lambda b, t: (b * nt + t, 0)),
                  whole((1, SHIFT_COLS)), chan, chan, chan, chan, chan, chan, chan,
                  lora, lora, lora, lora],
        out_specs=pl.BlockSpec((CHUNK, RWKV_W), lambda b, t: (b * nt + t, 0)),
        out_shape=jax.ShapeDtypeStruct((batch * seq, RWKV_W), F32),
        scratch_shapes=[pltpu.VMEM((PAIRS, LANES, LANES), F32), pltpu.VMEM((SUBLANES, SHIFT_COLS), F32)],
        compiler_params=pltpu.CompilerParams(
            dimension_semantics=("arbitrary", "arbitrary"), vmem_limit_bytes=VMEM_LIMIT),
        name="rwkv7_time_mix",
    )(p, row2(mu_shift), row2(w0), row2(a0), row2(k_k), row2(k_a), row2(r_k), row2(ln_x_w), row2(ln_x_b),
      wu_hi, wu_lo, au_hi, au_lo)


def _attn_kernel(q_ref, k_ref, v_ref, gate_ref, y_ref, o_scr, lse_scr, *, seq):
    n_blocks = seq // ATT_BLK
    qi = lax.broadcasted_iota(jnp.int32, (2 * ATT_BLK, ATT_BLK), 0) % ATT_BLK
    kj = lax.broadcasted_iota(jnp.int32, (2 * ATT_BLK, ATT_BLK), 1)
    ok_prev, ok_cur = kj >= qi, kj <= qi
    head0 = lax.broadcasted_iota(jnp.int32, (ATT_BLK, LANES), 1) < HEAD_DIM
    ones = jnp.ones((ATT_BLK, LANES), BF16)
    scale = HEAD_DIM ** -0.5
    neg = -1e30

    def attend(pi, d, kv_starts, blocks):
        rows = lambda s: pl.ds(s, ATT_BLK, stride=d) if d > 1 else pl.ds(s, ATT_BLK)
        kb = [k_ref[rows(s), :].astype(BF16) for s in kv_starts]
        vb = [jnp.concatenate([v_ref[rows(s), :].astype(BF16), ones], axis=1) for s in kv_starts]
        qs = []
        for cur, _, _ in blocks:
            q = q_ref[rows(kv_starts[cur]), :] * scale
            qs.append(jnp.concatenate([jnp.where(head0, q, 0.0), jnp.where(head0, 0.0, q)],
                                      axis=0).astype(BF16))
        scores = []
        for (cur, prev, prev_ok), q in zip(blocks, qs):
            if prev is None:
                scores.append(jnp.where(ok_cur, _mm_nt(q, kb[cur]), neg))
            else:
                s = _mm_nt(q, jnp.concatenate([kb[prev], kb[cur]], axis=0))
                okp = ok_prev if prev_ok is None else ok_prev & prev_ok
                scores.append(jnp.concatenate([jnp.where(okp, s[:, :ATT_BLK], neg),
                                               jnp.where(ok_cur, s[:, ATT_BLK:], neg)], axis=1))
        mx = [jnp.max(s, axis=-1, keepdims=True) for s in scores]
        probs = [jnp.exp(s - m).astype(BF16) for s, m in zip(scores, mx)]
        pv = [_mm(p, vb[cur] if prev is None else jnp.concatenate([vb[prev], vb[cur]], axis=0))
              for (cur, prev, _), p in zip(blocks, probs)]
        for (cur, _, _), m, x in zip(blocks, mx, pv):
            denom = x[:, LANES:]
            o = x[:, :LANES] / denom
            lse = m + jnp.log(denom)
            o_scr[pi, rows(kv_starts[cur]), :] = jnp.where(head0, o[:ATT_BLK], o[ATT_BLK:])
            lse_scr[pi, rows(kv_starts[cur]), :] = jnp.where(head0, lse[:ATT_BLK], lse[ATT_BLK:])

    for pi, d in enumerate(DILATIONS):
        nb = n_blocks // d
        span = ATT_BLK * d
        if nb == 1:
            def body(it, carry, pi=pi, d=d):
                attend(pi, d, [it * ATT_GROUP + u for u in range(ATT_GROUP)],
                       [(u, None, None) for u in range(ATT_GROUP)])
                return carry
            trips = d // ATT_GROUP
        elif nb == ATT_GROUP:
            def body(res, carry, pi=pi, d=d, span=span):
                attend(pi, d, [res + u * span for u in range(ATT_GROUP)],
                       [(0, None, None)] + [(u, u - 1, None) for u in range(1, ATT_GROUP)])
                return carry
            trips = d
        else:
            per_res = nb // ATT_GROUP
            def body(it, carry, pi=pi, d=d, span=span, per_res=per_res):
                res, blk0 = it // per_res, (it % per_res) * ATT_GROUP
                starts = [res + jnp.maximum(blk0 - 1, 0) * span]
                starts += [res + (blk0 + u) * span for u in range(ATT_GROUP)]
                attend(pi, d, starts,
                       [(1, 0, blk0 > 0)] + [(u + 1, u, None) for u in range(1, ATT_GROUP)])
                return carry
            trips = d * per_res
        lax.fori_loop(0, trips, body, 0)

    l0, l1, l2 = lse_scr[0], lse_scr[1], lse_scr[2]
    m = jnp.maximum(jnp.maximum(l0, l1), l2)
    w0, w1, w2 = jnp.exp(l0 - m), jnp.exp(l1 - m), jnp.exp(l2 - m)
    o = (w0 * o_scr[0] + w1 * o_scr[1] + w2 * o_scr[2]) / (w0 + w1 + w2)
    gate = gate_ref[...]
    y_ref[...] = o * (gate * _sigmoid(gate))


def _dilated_attention(p, batch, seq):
    def tile(blk0):
        return pl.BlockSpec((seq, LANES), lambda b, g: (b, blk0 + g))

    return pl.pallas_call(
        functools.partial(_attn_kernel, seq=seq),
        grid=(batch, PAIRS),
        in_specs=[tile(AQ_BLK0), tile(AK_BLK0), tile(AV_BLK0), tile(AGATE_BLK0)],
        out_specs=pl.BlockSpec((seq, LANES), lambda b, g: (b, g)),
        out_shape=jax.ShapeDtypeStruct((batch * seq, ATT_W), F32),
        scratch_shapes=[pltpu.VMEM((len(DILATIONS), seq, LANES), F32),
                        pltpu.VMEM((len(DILATIONS), seq, LANES), F32)],
        compiler_params=pltpu.CompilerParams(
            dimension_semantics=("arbitrary", "arbitrary"), vmem_limit_bytes=VMEM_LIMIT),
        name="dilated_attention",
    )(p, p, p, p)


def _out_kernel(yr_ref, ya_ref, x_ref, wr_ref, wa_ref, fw_ref, o_ref):
    h = (x_ref[...]
         + jnp.dot(yr_ref[...].astype(BF16), wr_ref[...], preferred_element_type=F32)
         + jnp.dot(ya_ref[...].astype(BF16), wa_ref[...], preferred_element_type=F32))
    ms = jnp.mean(h * h, axis=-1, keepdims=True)
    o_ref[...] = h * lax.rsqrt(ms + RMS_EPS) * fw_ref[...]


def _out_projection(y_rwkv, y_att, x2, w_out_bf16, final_norm_w, tm):
    tokens = x2.shape[0]
    tok = lambda width: pl.BlockSpec((tm, width), lambda i: (i, 0))
    return pl.pallas_call(
        _out_kernel,
        grid=(tokens // tm,),
        in_specs=[
            tok(RWKV_W), tok(ATT_W), tok(D_MODEL),
            pl.BlockSpec((RWKV_W, D_MODEL), lambda i: (0, 0)),
            pl.BlockSpec((ATT_W, D_MODEL), lambda i: (1, 0)),
            pl.BlockSpec((1, D_MODEL), lambda i: (0, 0)),
        ],
        out_specs=tok(D_MODEL),
        out_shape=jax.ShapeDtypeStruct((tokens, D_MODEL), F32),
        compiler_params=pltpu.CompilerParams(
            dimension_semantics=("arbitrary",), vmem_limit_bytes=VMEM_LIMIT),
        name="out_projection",
    )(y_rwkv, y_att, x2, w_out_bf16, w_out_bf16, final_norm_w.reshape(1, D_MODEL))


def kernel(x, norm_w, w_in, mu_shift, w0, w_up, a0, a_up, k_k, k_a, r_k, ln_x_w, ln_x_b, w_out,
           final_norm_w):
    batch, seq, _ = x.shape
    assert seq % (ATT_BLK * max(DILATIONS)) == 0 and seq % CHUNK == 0
    x2 = x.astype(F32).reshape(batch * seq, D_MODEL)
    p = _in_projection(x2, norm_w.astype(F32), w_in.astype(BF16), tm=min(1024, batch * seq), tn=1664)
    y_rwkv = _rwkv_time_mix(p, batch, seq, mu_shift, w0, w_up, a0, a_up, k_k, k_a, r_k, ln_x_w, ln_x_b)
    y_att = _dilated_attention(p, batch, seq)
    out = _out_projection(y_rwkv, y_att, x2, w_out.astype(BF16), final_norm_w.astype(F32),
                          tm=min(512, batch * seq))
    return out.reshape(batch, seq, D_MODEL).astype(x.dtype)
```

```python
import functools

import jax
import jax.numpy as jnp
from jax import lax
from jax.experimental import pallas as pl
from jax.experimental.pallas import tpu as pltpu

D_MODEL = 1024
HEAD_DIM = 64
RWKV_W = 1024
ATT_W = 1024
MIX_W = RWKV_W + ATT_W
LORA_RANK = 64
SHIFT_COLS = 3 * RWKV_W + 2 * LORA_RANK
RWKV_GATE0 = SHIFT_COLS
ATT0 = RWKV_GATE0 + RWKV_W
ATT_GATE0 = ATT0 + 3 * ATT_W
DILATIONS = (1, 4, 16)
ATT_BLK = 128
ATT_GROUP = 4
RMS_EPS = 1e-5
GN_EPS = 64e-5

LANES = 128
SUBLANES = 8
PAIRS = RWKV_W // LANES
CHUNK = 64
K_COL0, V_COL0, LORA_COL0 = RWKV_W, 2 * RWKV_W, 3 * RWKV_W
PROJ_TM = 512

F32 = jnp.float32
BF16 = jnp.bfloat16
VMEM_LIMIT = 56 * 1024 * 1024

NT_DIMS = (((1,), (1,)), ((), ()))
TN_DIMS = (((0,), (0,)), ((), ()))


def _mm(a, b):
    return jnp.dot(a.astype(BF16), b.astype(BF16), preferred_element_type=F32)


def _mm_nt(a, b):
    return lax.dot_general(a.astype(BF16), b.astype(BF16), NT_DIMS, preferred_element_type=F32)


def _mm_tn(a, b):
    return lax.dot_general(a.astype(BF16), b.astype(BF16), TN_DIMS, preferred_element_type=F32)


def _split(x):
    hi = x.astype(BF16)
    return hi, (x - hi.astype(F32)).astype(BF16)


def _mm_split_lhs(x, w_bf16):
    hi, lo = _split(x)
    return (jnp.dot(hi, w_bf16, preferred_element_type=F32)
            + jnp.dot(lo, w_bf16, preferred_element_type=F32))


def _mm_split_rhs(w_bf16, x):
    hi, lo = _split(x)
    return (jnp.dot(w_bf16, hi, preferred_element_type=F32)
            + jnp.dot(w_bf16, lo, preferred_element_type=F32))


def _mm_x3(x, w_hi, w_lo):
    hi, lo = _split(x)
    return (jnp.dot(hi, w_hi, preferred_element_type=F32)
            + jnp.dot(lo, w_hi, preferred_element_type=F32)
            + jnp.dot(hi, w_lo, preferred_element_type=F32))


def _sigmoid(x):
    return 1.0 / (1.0 + jnp.exp(-x))


def _proj_kernel(x_ref, nw_ref, w_ref, *rest, epilogue, tiles_per_seq):
    x = x_ref[...]
    ms = jnp.mean(x * x, axis=-1, keepdims=True)
    xn = (x * lax.rsqrt(ms + RMS_EPS) * nw_ref[...]).astype(BF16)
    o = jnp.dot(xn, w_ref[...], preferred_element_type=F32)
    if epilogue == "shift":
        mu_ref, o_ref, prev_ref = rest
        tm = o.shape[0]
        first = pl.program_id(0) % tiles_per_seq == 0
        last = jnp.where(first, 0.0, prev_ref[SUBLANES - 1:SUBLANES, :])
        row = lax.broadcasted_iota(jnp.int32, (tm, 1), 0)
        oprev = jnp.where(row == 0, last, pltpu.roll(o, 1, 0))
        prev_ref[...] = o[tm - SUBLANES:, :]
        o_ref[...] = o + (oprev - o) * mu_ref[...]
    elif epilogue == "silu":
        (o_ref,) = rest
        o_ref[...] = o * _sigmoid(o)
    else:
        (o_ref,) = rest
        o_ref[...] = o


def _in_projection(x2, norm_w, w_bf16, seq, epilogue, mu=None):
    tokens, n = x2.shape[0], w_bf16.shape[1]
    tm = min(PROJ_TM, seq)
    in_specs = [pl.BlockSpec((tm, D_MODEL), lambda i: (i, 0)),
                pl.BlockSpec((1, D_MODEL), lambda i: (0, 0)),
                pl.BlockSpec((D_MODEL, n), lambda i: (0, 0))]
    args = [x2, norm_w.reshape(1, D_MODEL), w_bf16]
    scratch = []
    if epilogue == "shift":
        in_specs.append(pl.BlockSpec((1, n), lambda i: (0, 0)))
        args.append(mu.reshape(1, n).astype(F32))
        scratch.append(pltpu.VMEM((SUBLANES, n), F32))
    return pl.pallas_call(
        functools.partial(_proj_kernel, epilogue=epilogue, tiles_per_seq=seq // tm),
        grid=(tokens // tm,),
        in_specs=in_specs,
        out_specs=pl.BlockSpec((tm, n), lambda i: (i, 0)),
        out_shape=jax.ShapeDtypeStruct((tokens, n), F32),
        scratch_shapes=scratch,
        compiler_params=pltpu.CompilerParams(
            dimension_semantics=("arbitrary",), vmem_limit_bytes=VMEM_LIMIT),
        name="in_projection_" + epilogue,
    )(*args)


OPERANDS = 7


def _rwkv_kernel(p_ref, gate_ref, w0_ref, a0_ref, kk_ref, ka_ref, rk_ref, lnw_ref, lnb_ref,
                 wuh_ref, wul_ref, auh_ref, aul_ref, y_ref, state_ref, ops_ref, gend_ref, bonus_ref):
    t = pl.program_id(1)

    @pl.when((pl.program_id(0) == 0) & (t == 0))
    def _():
        ops_ref[...] = jnp.zeros_like(ops_ref)
        gend_ref[...] = jnp.zeros_like(gend_ref)
        bonus_ref[...] = jnp.zeros_like(bonus_ref)
        state_ref[...] = jnp.zeros_like(state_ref)

    fill, use = t % 2, (t + 1) % 2
    row = lax.broadcasted_iota(jnp.int32, (CHUNK, LANES), 0)
    lane = lax.broadcasted_iota(jnp.int32, (CHUNK, LANES), 1)
    head0 = lane < HEAD_DIM
    tcol = lane % CHUNK
    strict, incl = tcol < row, tcol <= row
    same16 = (tcol // 16) == (row // 16)
    same32 = (tcol // 32) == (row // 32)
    eye = jnp.where(tcol == row, 1.0, 0.0).astype(F32)
    bi = lax.broadcasted_iota(jnp.int32, (LANES, LANES), 0)
    bj = lax.broadcasted_iota(jnp.int32, (LANES, LANES), 1)
    block_diag = (bi // HEAD_DIM) == (bj // HEAD_DIM)
    head_ones = jnp.where(block_diag, 1.0, 0.0).astype(BF16)
    ti = lax.broadcasted_iota(jnp.int32, (CHUNK, CHUNK), 0)
    tj = lax.broadcasted_iota(jnp.int32, (CHUNK, CHUNK), 1)
    tri = jnp.where(tj <= ti, 1.0, 0.0).astype(BF16)
    units = range(PAIRS)
    cols = lambda c0, g: slice(c0 + g * LANES, c0 + (g + 1) * LANES)

    def stack(x):
        return jnp.concatenate([jnp.where(head0, x, 0.0), jnp.where(head0, 0.0, x)], axis=0).astype(BF16)

    def head_sums(xs):
        s = _mm_split_lhs(jnp.concatenate(xs, axis=0), head_ones)
        return [s[g * CHUNK:(g + 1) * CHUNK] for g in units]

    def chain():
        ops = lambda i, g: ops_ref[use, i, :, cols(0, g)]
        a_t, b_t, k_t, r_t, v, b_h, k_h = ([ops(i, g) for g in units] for i in range(OPERANDS))
        v_s = [stack(x) for x in v]
        gram = [_mm_nt(jnp.concatenate([a_t[g], r_t[g]], axis=0),
                       jnp.concatenate([stack(b_t[g]), stack(k_t[g])], axis=0)) for g in units]
        yield
        a_ab = [jnp.where(strict, x[:CHUNK, :LANES], 0.0) for x in gram]
        a_ak = [jnp.where(strict, x[:CHUNK, LANES:], 0.0) for x in gram]
        a_rb = [jnp.where(incl, x[CHUNK:, :LANES], 0.0) for x in gram]
        a_rk = [jnp.where(incl, x[CHUNK:, LANES:], 0.0) for x in gram]
        nil = [jnp.where(same16, x, 0.0) for x in a_ab]
        inv = [eye + x for x in nil]
        nil = [_mm(x, stack(x)) for x in nil]
        yield
        av = [_mm(jnp.concatenate([a_ak[g], a_rk[g]], axis=0), v_s[g]) for g in units]
        yield
        for _ in range(2):
            both = [_mm(jnp.concatenate([nil[g], inv[g]], axis=0), stack(nil[g])) for g in units]
            nil = [x[:CHUNK] for x in both]
            inv = [inv[g] + both[g][CHUNK:] for g in units]
            yield
        vk = [_mm_tn(v[g], k_h[g]) for g in units]
        yield
        inv = [inv[g] + _mm(inv[g], stack(nil[g])) for g in units]
        yield
        for off_diag in (same32 & ~same16, ~same32):
            low = [_mm(inv[g], stack(jnp.where(off_diag, a_ab[g], 0.0))) for g in units]
            yield
            inv = [inv[g] + _mm(low[g], stack(inv[g])) for g in units]
            yield
        tx = [_mm(inv[g], jnp.concatenate([stack(a_t[g]), stack(av[g][:CHUNK])], axis=1))
              for g in units]
        yield
        zz = [_mm(a_rb[g], jnp.concatenate([stack(tx[g][:, :LANES]), stack(tx[g][:, LANES:])], axis=1))
              for g in units]
        yield
        r_p = [r_t[g] + zz[g][:, :LANES] for g in units]
        y0 = [zz[g][:, LANES:] + av[g][CHUNK:] for g in units]
        mn = [_mm_tn(tx[g], b_h[g]) for g in units]
        yield
        first_chunk = t <= 1
        state = [jnp.where(first_chunk, 0.0, state_ref[g]) for g in units]
        y = [_mm_nt(r_p[g], state[g]) + y0[g] for g in units]
        yield
        for g in units:
            m_t = jnp.where(block_diag, mn[g][:LANES], 0.0)
            n_t = jnp.where(block_diag, mn[g][LANES:] + vk[g], 0.0)
            state_ref[g] = state[g] * gend_ref[use, 0:1, cols(0, g)] + _mm(state[g], m_t) + n_t
        yield
        mean = head_sums(y)
        yc = [y[g] - mean[g] * (1.0 / HEAD_DIM) for g in units]
        var = head_sums([x * x for x in yc])
        for g in units:
            yn = yc[g] * lax.rsqrt(var[g] * (1.0 / HEAD_DIM) + GN_EPS)
            yn = yn * lnw_ref[:, cols(0, g)] + lnb_ref[:, cols(0, g)]
            y_ref[:, cols(0, g)] = (yn + bonus_ref[use, :, cols(0, g)]) * gate_ref[:, cols(0, g)]

    def prepare():
        lo = p_ref[:, LORA_COL0:LORA_COL0 + LANES]
        z_all = -(w0_ref[...] + _mm_x3(jnp.tanh(lo), wuh_ref[...], wul_ref[...]))
        a_all = a0_ref[...] + _mm_x3(lo, auh_ref[...], aul_ref[...])
        yield
        for g in units:
            c = cols(0, g)
            z = z_all[:, c]
            w = -(jnp.maximum(z, 0.0) + jnp.log(1.0 + jnp.exp(-jnp.abs(z)))) - 0.5
            lw = -jnp.exp(w)
            lg = _mm_split_rhs(tri, lw)
            lg_end = lg[CHUNK - 1:CHUNK, :]
            e_inv = jnp.exp(-lg)
            e_end = jnp.exp(lg_end - lg)
            gend_ref[fill, :, c] = jnp.broadcast_to(jnp.exp(lg_end), (SUBLANES, LANES))
            a = _sigmoid(a_all[:, c])
            r, k, v = p_ref[:, c], p_ref[:, cols(K_COL0, g)], p_ref[:, cols(V_COL0, g)]
            kk = k * kk_ref[:, c]
            k2 = k * (1.0 + (a - 1.0) * ka_ref[:, c])
            sums = _mm_split_lhs(jnp.concatenate([kk * kk, r * k2 * rk_ref[:, c]], axis=0), head_ones)
            kk = kk / jnp.maximum(jnp.sqrt(sums[:CHUNK]), 1e-12)
            bonus_ref[fill, :, c] = sums[CHUNK:] * v
            ops_ref[fill, 0, :, c] = -kk * jnp.exp(lg - lw)
            ops_ref[fill, 1, :, c] = kk * a * e_inv
            ops_ref[fill, 2, :, c] = k2 * e_inv
            ops_ref[fill, 3, :, c] = r * jnp.exp(lg)
            ops_ref[fill, 4, :, c] = v
            ops_ref[fill, 5, :, c] = kk * a * e_end
            ops_ref[fill, 6, :, c] = k2 * e_end
            yield

    done = object()
    streams = [chain(), prepare()]
    while streams:
        streams = [s for s in streams if next(s, done) is not done]


def _rwkv_time_mix(p_shift, p_gates, batch, seq, w0, w_up, a0, a_up, k_k, k_a, r_k, ln_x_w, ln_x_b):
    nt = seq // CHUNK
    row2 = lambda v: v.reshape(1, -1).astype(F32)
    zeros = jnp.zeros((LORA_RANK, RWKV_W), F32)
    wup_pad = jnp.concatenate([w_up.astype(F32), zeros], axis=0)
    aup_pad = jnp.concatenate([zeros, a_up.astype(F32)], axis=0)
    wu_hi, wu_lo = _split(wup_pad)
    au_hi, au_lo = _split(aup_pad)
    whole = lambda shape: pl.BlockSpec(shape, lambda b, t: (0, 0))
    chan = whole((1, RWKV_W))
    lora = whole((2 * LORA_RANK, RWKV_W))
    prepared = lambda b, t: (b * nt + jnp.minimum(t, nt - 1), 0)
    chained = lambda b, t: (b * nt + jnp.maximum(t - 1, 0), 0)
    return pl.pallas_call(
        _rwkv_kernel,
        grid=(batch, nt + 1),
        in_specs=[pl.BlockSpec((CHUNK, SHIFT_COLS), prepared),
                  pl.BlockSpec((CHUNK, RWKV_W), chained),
                  chan, chan, chan, chan, chan, chan, chan, lora, lora, lora, lora],
        out_specs=pl.BlockSpec((CHUNK, RWKV_W), chained),
        out_shape=jax.ShapeDtypeStruct((batch * seq, RWKV_W), F32),
        scratch_shapes=[pltpu.VMEM((PAIRS, LANES, LANES), F32),
                        pltpu.VMEM((2, OPERANDS, CHUNK, RWKV_W), F32),
                        pltpu.VMEM((2, SUBLANES, RWKV_W), F32),
                        pltpu.VMEM((2, CHUNK, RWKV_W), F32)],
        compiler_params=pltpu.CompilerParams(
            dimension_semantics=("arbitrary", "arbitrary"), vmem_limit_bytes=VMEM_LIMIT),
        name="rwkv7_time_mix",
    )(p_shift, p_gates, row2(w0), row2(a0), row2(k_k), row2(k_a), row2(r_k), row2(ln_x_w), row2(ln_x_b),
      wu_hi, wu_lo, au_hi, au_lo)


def _attn_kernel(q_ref, k_ref, v_ref, gate_ref, y_ref, o_scr, lse_scr, *, seq):
    n_blocks = seq // ATT_BLK
    qi = lax.broadcasted_iota(jnp.int32, (2 * ATT_BLK, ATT_BLK), 0) % ATT_BLK
    kj = lax.broadcasted_iota(jnp.int32, (2 * ATT_BLK, ATT_BLK), 1)
    ok_prev, ok_cur = kj >= qi, kj <= qi
    head0 = lax.broadcasted_iota(jnp.int32, (ATT_BLK, LANES), 1) < HEAD_DIM
    ones = jnp.ones((ATT_BLK, LANES), BF16)
    scale = HEAD_DIM ** -0.5
    neg = -1e30

    def attend(pi, d, kv_starts, blocks):
        rows = lambda s: pl.ds(s, ATT_BLK, stride=d) if d > 1 else pl.ds(s, ATT_BLK)
        kb = [k_ref[rows(s), :].astype(BF16) for s in kv_starts]
        vb = [jnp.concatenate([v_ref[rows(s), :].astype(BF16), ones], axis=1) for s in kv_starts]
        qs = []
        for cur, _, _ in blocks:
            q = q_ref[rows(kv_starts[cur]), :] * scale
            qs.append(jnp.concatenate([jnp.where(head0, q, 0.0), jnp.where(head0, 0.0, q)],
                                      axis=0).astype(BF16))
        scores = []
        for (cur, prev, prev_ok), q in zip(blocks, qs):
            if prev is None:
                scores.append(jnp.where(ok_cur, _mm_nt(q, kb[cur]), neg))
            else:
                s = _mm_nt(q, jnp.concatenate([kb[prev], kb[cur]], axis=0))
                okp = ok_prev if prev_ok is None else ok_prev & prev_ok
                scores.append(jnp.concatenate([jnp.where(okp, s[:, :ATT_BLK], neg),
                                               jnp.where(ok_cur, s[:, ATT_BLK:], neg)], axis=1))
        mx = [jnp.max(s, axis=-1, keepdims=True) for s in scores]
        probs = [jnp.exp(s - m).astype(BF16) for s, m in zip(scores, mx)]
        pv = [_mm(p, vb[cur] if prev is None else jnp.concatenate([vb[prev], vb[cur]], axis=0))
              for (cur, prev, _), p in zip(blocks, probs)]
        for (cur, _, _), m, x in zip(blocks, mx, pv):
            denom = x[:, LANES:]
            o = x[:, :LANES] / denom
            lse = m + jnp.log(denom)
            o_scr[pi, rows(kv_starts[cur]), :] = jnp.where(head0, o[:ATT_BLK], o[ATT_BLK:])
            lse_scr[pi, rows(kv_starts[cur]), :] = jnp.where(head0, lse[:ATT_BLK], lse[ATT_BLK:])

    for pi, d in enumerate(DILATIONS):
        nb = n_blocks // d
        span = ATT_BLK * d
        if nb == 1:
            def body(it, carry, pi=pi, d=d):
                attend(pi, d, [it * ATT_GROUP + u for u in range(ATT_GROUP)],
                       [(u, None, None) for u in range(ATT_GROUP)])
                return carry
            trips = d // ATT_GROUP
        elif nb == ATT_GROUP:
            def body(res, carry, pi=pi, d=d, span=span):
                attend(pi, d, [res + u * span for u in range(ATT_GROUP)],
                       [(0, None, None)] + [(u, u - 1, None) for u in range(1, ATT_GROUP)])
                return carry
            trips = d
        else:
            per_res = nb // ATT_GROUP
            def body(it, carry, pi=pi, d=d, span=span, per_res=per_res):
                res, blk0 = it // per_res, (it % per_res) * ATT_GROUP
                starts = [res + jnp.maximum(blk0 - 1, 0) * span]
                starts += [res + (blk0 + u) * span for u in range(ATT_GROUP)]
                attend(pi, d, starts,
                       [(1, 0, blk0 > 0)] + [(u + 1, u, None) for u in range(1, ATT_GROUP)])
                return carry
            trips = d * per_res
        lax.fori_loop(0, trips, body, 0)

    l0, l1, l2 = lse_scr[0], lse_scr[1], lse_scr[2]
    m = jnp.maximum(jnp.maximum(l0, l1), l2)
    w0, w1, w2 = jnp.exp(l0 - m), jnp.exp(l1 - m), jnp.exp(l2 - m)
    o = (w0 * o_scr[0] + w1 * o_scr[1] + w2 * o_scr[2]) / (w0 + w1 + w2)
    y_ref[...] = o * gate_ref[...]


def _dilated_attention(p_att, p_gates, batch, seq):
    def tile(blk0):
        return pl.BlockSpec((seq, LANES), lambda b, g: (b, blk0 + g))

    return pl.pallas_call(
        functools.partial(_attn_kernel, seq=seq),
        grid=(batch, PAIRS),
        in_specs=[tile(0), tile(PAIRS), tile(2 * PAIRS), tile(PAIRS)],
        out_specs=pl.BlockSpec((seq, LANES), lambda b, g: (b, g)),
        out_shape=jax.ShapeDtypeStruct((batch * seq, ATT_W), F32),
        scratch_shapes=[pltpu.VMEM((len(DILATIONS), seq, LANES), F32),
                        pltpu.VMEM((len(DILATIONS), seq, LANES), F32)],
        compiler_params=pltpu.CompilerParams(
            dimension_semantics=("arbitrary", "arbitrary"), vmem_limit_bytes=VMEM_LIMIT),
        name="dilated_attention",
    )(p_att, p_att, p_att, p_gates)


def _out_kernel(yr_ref, ya_ref, x_ref, wr_ref, wa_ref, fw_ref, o_ref):
    h = (x_ref[...]
         + jnp.dot(yr_ref[...].astype(BF16), wr_ref[...], preferred_element_type=F32)
         + jnp.dot(ya_ref[...].astype(BF16), wa_ref[...], preferred_element_type=F32))
    ms = jnp.mean(h * h, axis=-1, keepdims=True)
    o_ref[...] = h * lax.rsqrt(ms + RMS_EPS) * fw_ref[...]


def _out_projection(y_rwkv, y_att, x2, w_out_bf16, final_norm_w, tm):
    tokens = x2.shape[0]
    tok = lambda width: pl.BlockSpec((tm, width), lambda i: (i, 0))
    return pl.pallas_call(
        _out_kernel,
        grid=(tokens // tm,),
        in_specs=[
            tok(RWKV_W), tok(ATT_W), tok(D_MODEL),
            pl.BlockSpec((RWKV_W, D_MODEL), lambda i: (0, 0)),
            pl.BlockSpec((ATT_W, D_MODEL), lambda i: (1, 0)),
            pl.BlockSpec((1, D_MODEL), lambda i: (0, 0)),
        ],
        out_specs=tok(D_MODEL),
        out_shape=jax.ShapeDtypeStruct((tokens, D_MODEL), F32),
        compiler_params=pltpu.CompilerParams(
            dimension_semantics=("arbitrary",), vmem_limit_bytes=VMEM_LIMIT),
        name="out_projection",
    )(y_rwkv, y_att, x2, w_out_bf16, w_out_bf16, final_norm_w.reshape(1, D_MODEL))


def kernel(x, norm_w, w_in, mu_shift, w0, w_up, a0, a_up, k_k, k_a, r_k, ln_x_w, ln_x_b, w_out,
           final_norm_w):
    batch, seq, _ = x.shape
    assert seq % (ATT_BLK * max(DILATIONS)) == 0 and seq % CHUNK == 0
    x2 = x.astype(F32).reshape(batch * seq, D_MODEL)
    nw = norm_w.astype(F32)
    w_bf16 = w_in.astype(BF16)
    w_gates = jnp.concatenate([w_bf16[:, RWKV_GATE0:ATT0], w_bf16[:, ATT_GATE0:]], axis=1)
    p_shift = _in_projection(x2, nw, w_bf16[:, :SHIFT_COLS], seq, "shift", mu_shift)
    p_gates = _in_projection(x2, nw, w_gates, seq, "silu")
    p_att = _in_projection(x2, nw, w_bf16[:, ATT0:ATT_GATE0], seq, "plain")
    y_rwkv = _rwkv_time_mix(p_shift, p_gates, batch, seq, w0, w_up, a0, a_up, k_k, k_a, r_k,
                            ln_x_w, ln_x_b)
    y_att = _dilated_attention(p_att, p_gates, batch, seq)
    out = _out_projection(y_rwkv, y_att, x2, w_out.astype(BF16), final_norm_w.astype(F32),
                          tm=min(512, batch * seq))
    return out.reshape(batch, seq, D_MODEL).astype(x.dtype)
```

```python
import functools

import jax
import jax.numpy as jnp
from jax import lax
from jax.experimental import pallas as pl
from jax.experimental.pallas import tpu as pltpu

D_MODEL = 1024
HEAD_DIM = 64
RWKV_W = 1024
ATT_W = 1024
MIX_W = RWKV_W + ATT_W
LORA_RANK = 64
SHIFT_COLS = 3 * RWKV_W + 2 * LORA_RANK
RWKV_GATE0 = SHIFT_COLS
ATT0 = RWKV_GATE0 + RWKV_W
ATT_GATE0 = ATT0 + 3 * ATT_W
DILATIONS = (1, 4, 16)
ATT_BLK = 128
ATT_GROUP = 8
LOG2_E = 1.4426950408889634
RMS_EPS = 1e-5
GN_EPS = 64e-5

LANES = 128
SUBLANES = 8
PAIRS = RWKV_W // LANES
CHUNK = 64
K_COL0, V_COL0, LORA_COL0 = RWKV_W, 2 * RWKV_W, 3 * RWKV_W
PROJ_TM = 512

F32 = jnp.float32
BF16 = jnp.bfloat16
VMEM_LIMIT = 56 * 1024 * 1024

NT_DIMS = (((1,), (1,)), ((), ()))
TN_DIMS = (((0,), (0,)), ((), ()))


def _mm(a, b):
    return jnp.dot(a.astype(BF16), b.astype(BF16), preferred_element_type=F32)


def _mm_nt(a, b):
    return lax.dot_general(a.astype(BF16), b.astype(BF16), NT_DIMS, preferred_element_type=F32)


def _mm_tn(a, b):
    return lax.dot_general(a.astype(BF16), b.astype(BF16), TN_DIMS, preferred_element_type=F32)


def _split(x):
    hi = x.astype(BF16)
    return hi, (x - hi.astype(F32)).astype(BF16)


def _mm_split_lhs(x, w_bf16):
    hi, lo = _split(x)
    return (jnp.dot(hi, w_bf16, preferred_element_type=F32)
            + jnp.dot(lo, w_bf16, preferred_element_type=F32))


def _mm_split_rhs(w_bf16, x):
    hi, lo = _split(x)
    return (jnp.dot(w_bf16, hi, preferred_element_type=F32)
            + jnp.dot(w_bf16, lo, preferred_element_type=F32))


def _mm_x3(x, w_hi, w_lo):
    hi, lo = _split(x)
    return (jnp.dot(hi, w_hi, preferred_element_type=F32)
            + jnp.dot(lo, w_hi, preferred_element_type=F32)
            + jnp.dot(hi, w_lo, preferred_element_type=F32))


def _sigmoid(x):
    return 1.0 / (1.0 + jnp.exp(-x))


def _proj_kernel(x_ref, nw_ref, w_ref, *rest, epilogue, tiles_per_seq):
    x = x_ref[...]
    ms = jnp.mean(x * x, axis=-1, keepdims=True)
    xn = (x * lax.rsqrt(ms + RMS_EPS) * nw_ref[...]).astype(BF16)
    o = jnp.dot(xn, w_ref[...], preferred_element_type=F32)
    if epilogue == "shift":
        mu_ref, o_ref, prev_ref = rest
        tm = o.shape[0]
        first = pl.program_id(0) % tiles_per_seq == 0
        last = jnp.where(first, 0.0, prev_ref[SUBLANES - 1:SUBLANES, :])
        row = lax.broadcasted_iota(jnp.int32, (tm, 1), 0)
        oprev = jnp.where(row == 0, last, pltpu.roll(o, 1, 0))
        prev_ref[...] = o[tm - SUBLANES:, :]
        o_ref[...] = o + (oprev - o) * mu_ref[...]
    elif epilogue == "silu":
        (o_ref,) = rest
        o_ref[...] = o * _sigmoid(o)
    else:
        (o_ref,) = rest
        o_ref[...] = o


def _in_projection(x2, norm_w, w_bf16, seq, epilogue, mu=None):
    tokens, n = x2.shape[0], w_bf16.shape[1]
    tm = min(PROJ_TM, seq)
    in_specs = [pl.BlockSpec((tm, D_MODEL), lambda i: (i, 0)),
                pl.BlockSpec((1, D_MODEL), lambda i: (0, 0)),
                pl.BlockSpec((D_MODEL, n), lambda i: (0, 0))]
    args = [x2, norm_w.reshape(1, D_MODEL), w_bf16]
    scratch = []
    if epilogue == "shift":
        in_specs.append(pl.BlockSpec((1, n), lambda i: (0, 0)))
        args.append(mu.reshape(1, n).astype(F32))
        scratch.append(pltpu.VMEM((SUBLANES, n), F32))
    return pl.pallas_call(
        functools.partial(_proj_kernel, epilogue=epilogue, tiles_per_seq=seq // tm),
        grid=(tokens // tm,),
        in_specs=in_specs,
        out_specs=pl.BlockSpec((tm, n), lambda i: (i, 0)),
        out_shape=jax.ShapeDtypeStruct((tokens, n), F32),
        scratch_shapes=scratch,
        compiler_params=pltpu.CompilerParams(
            dimension_semantics=("arbitrary",), vmem_limit_bytes=VMEM_LIMIT),
        name="in_projection_" + epilogue,
    )(*args)


OPERANDS = 6
STEP_CHUNKS = 2
STEP_ROWS = STEP_CHUNKS * CHUNK


def _rwkv_kernel(p_ref, gate_ref, w0_ref, a0_ref, kk_ref, ka_ref, rk_ref, lnw_ref, lnb_ref,
                 wuh_ref, wul_ref, auh_ref, aul_ref, y_ref, state_ref, ops_ref, rt_ref, gend_ref, bonus_ref):
    t = pl.program_id(1)

    @pl.when((pl.program_id(0) == 0) & (t == 0))
    def _():
        ops_ref[...] = jnp.zeros_like(ops_ref)
        rt_ref[...] = jnp.zeros_like(rt_ref)
        gend_ref[...] = jnp.zeros_like(gend_ref)
        bonus_ref[...] = jnp.zeros_like(bonus_ref)
        state_ref[...] = jnp.zeros_like(state_ref)

    fill, use = t % 2, (t + 1) % 2
    row = lax.broadcasted_iota(jnp.int32, (CHUNK, LANES), 0)
    lane = lax.broadcasted_iota(jnp.int32, (CHUNK, LANES), 1)
    head0 = lane < HEAD_DIM
    tcol = lane % CHUNK
    strict, incl = tcol < row, tcol <= row
    same16 = (tcol // 16) == (row // 16)
    same32 = (tcol // 32) == (row // 32)
    eye = jnp.where(tcol == row, 1.0, 0.0).astype(F32)
    bi = lax.broadcasted_iota(jnp.int32, (LANES, LANES), 0)
    bj = lax.broadcasted_iota(jnp.int32, (LANES, LANES), 1)
    block_diag = (bi // HEAD_DIM) == (bj // HEAD_DIM)
    head_ones = jnp.where(block_diag, 1.0, 0.0).astype(BF16)
    ti = lax.broadcasted_iota(jnp.int32, (STEP_ROWS, STEP_ROWS), 0)
    tj = lax.broadcasted_iota(jnp.int32, (STEP_ROWS, STEP_ROWS), 1)
    tri = jnp.where((tj <= ti) & (tj // CHUNK == ti // CHUNK), 1.0, 0.0).astype(BF16)
    pairs = range(PAIRS)
    chunks = range(STEP_CHUNKS)
    units = [(c, g) for c in chunks for g in pairs]
    cols = lambda c0, g: slice(c0 + g * LANES, c0 + (g + 1) * LANES)
    rows = lambda c: slice(c * CHUNK, (c + 1) * CHUNK)

    def stack(x):
        x = x.astype(BF16)
        zero = jnp.zeros_like(x)
        return jnp.concatenate([jnp.where(head0, x, zero), jnp.where(head0, zero, x)], axis=0)

    def head_sums(xs):
        s = _mm_split_lhs(jnp.concatenate(xs, axis=0), head_ones)
        return [s[i * CHUNK:(i + 1) * CHUNK] for i in range(len(xs))]

    def chain():
        ops = lambda i, c, g: ops_ref[use, i, rows(c), cols(0, g)]
        a_t, b_t, k_t, v, b_h, k_h = ([ops(i, c, g) for c, g in units] for i in range(OPERANDS))
        r_t = [rt_ref[use, rows(c), cols(0, g)] for c, g in units]
        n = range(len(units))
        v_s = [stack(x) for x in v]
        gram = [_mm_nt(jnp.concatenate([a_t[u], r_t[u]], axis=0),
                       jnp.concatenate([stack(b_t[u]), stack(k_t[u])], axis=0)) for u in n]
        yield
        a_ab = [jnp.where(strict, x[:CHUNK, :LANES], 0.0) for x in gram]
        a_ak = [jnp.where(strict, x[:CHUNK, LANES:], 0.0) for x in gram]
        a_rb = [jnp.where(incl, x[CHUNK:, :LANES], 0.0) for x in gram]
        a_rk = [jnp.where(incl, x[CHUNK:, LANES:], 0.0) for x in gram]
        nil = [jnp.where(same16, x, 0.0) for x in a_ab]
        inv = [eye + x for x in nil]
        nil = [_mm(x, stack(x)) for x in nil]
        yield
        av = [_mm(jnp.concatenate([a_ak[u], a_rk[u]], axis=0), v_s[u]) for u in n]
        yield
        for _ in range(2):
            both = [_mm(jnp.concatenate([nil[u], inv[u]], axis=0), stack(nil[u])) for u in n]
            nil = [x[:CHUNK] for x in both]
            inv = [inv[u] + both[u][CHUNK:] for u in n]
            yield
        vk = [_mm_tn(v[u], k_h[u]) for u in n]
        yield
        inv = [inv[u] + _mm(inv[u], stack(nil[u])) for u in n]
        yield
        for off_diag in (same32 & ~same16, ~same32):
            low = [_mm(inv[u], stack(jnp.where(off_diag, a_ab[u], 0.0))) for u in n]
            yield
            inv = [inv[u] + _mm(low[u], stack(inv[u])) for u in n]
            yield
        tx = [_mm(inv[u], jnp.concatenate([stack(a_t[u]), stack(av[u][:CHUNK])], axis=1))
              for u in n]
        yield
        zz = [_mm(a_rb[u], jnp.concatenate([stack(tx[u][:, :LANES]), stack(tx[u][:, LANES:])], axis=1))
              for u in n]
        yield
        r_p = [r_t[u] + zz[u][:, :LANES] for u in n]
        y0 = [zz[u][:, LANES:] + av[u][CHUNK:] for u in n]
        mn = [_mm_tn(tx[u], b_h[u]) for u in n]
        yield
        state = [jnp.where(t <= 1, 0.0, state_ref[g]) for g in pairs]
        y = []
        for u, (c, g) in enumerate(units):
            y.append(_mm_nt(r_p[u], state[g]) + y0[u])
            m_t = jnp.where(block_diag, mn[u][:LANES], 0.0)
            n_t = jnp.where(block_diag, mn[u][LANES:] + vk[u], 0.0)
            g_end = gend_ref[use, c * SUBLANES:c * SUBLANES + 1, cols(0, g)]
            state[g] = state[g] * g_end + _mm(state[g], m_t) + n_t
            if g == PAIRS - 1:
                yield
        for g in pairs:
            state_ref[g] = state[g]
        mean = head_sums(y)
        yc = [y[u] - mean[u] * (1.0 / HEAD_DIM) for u in n]
        var = head_sums([x * x for x in yc])
        for u, (c, g) in enumerate(units):
            yn = yc[u] * lax.rsqrt(var[u] * (1.0 / HEAD_DIM) + GN_EPS)
            yn = yn * lnw_ref[:, cols(0, g)] + lnb_ref[:, cols(0, g)]
            y_ref[rows(c), cols(0, g)] = ((yn + bonus_ref[use, rows(c), cols(0, g)])
                                          * gate_ref[rows(c), cols(0, g)])

    def prepare():
        lo = p_ref[:, LORA_COL0:LORA_COL0 + LANES]
        z_all = -(w0_ref[...] + _mm_x3(jnp.tanh(lo), wuh_ref[...], wul_ref[...]))
        a_all = a0_ref[...] + _mm_x3(lo, auh_ref[...], aul_ref[...])
        yield
        for g in pairs:
            c = cols(0, g)
            z = z_all[:, c]
            w = -(jnp.maximum(z, 0.0) + jnp.log(1.0 + jnp.exp(-jnp.abs(z)))) - 0.5
            lw = -jnp.exp(w)
            lg = _mm_split_rhs(tri, lw)
            ends = [lg[(i + 1) * CHUNK - 1:(i + 1) * CHUNK, :] for i in chunks]
            lg_end = jnp.concatenate([jnp.broadcast_to(e, (CHUNK, LANES)) for e in ends], axis=0)
            for i in chunks:
                gend_ref[fill, i * SUBLANES:(i + 1) * SUBLANES, c] = jnp.broadcast_to(
                    jnp.exp(ends[i]), (SUBLANES, LANES))
            e_inv = jnp.exp(-lg)
            e_end = jnp.exp(lg_end - lg)
            a = _sigmoid(a_all[:, c])
            r, k, v = p_ref[:, c], p_ref[:, cols(K_COL0, g)], p_ref[:, cols(V_COL0, g)]
            kk = k * kk_ref[:, c]
            k2 = k * (1.0 + (a - 1.0) * ka_ref[:, c])
            sums = _mm_split_lhs(jnp.concatenate([kk * kk, r * k2 * rk_ref[:, c]], axis=0), head_ones)
            kk = kk / jnp.maximum(jnp.sqrt(sums[:STEP_ROWS]), 1e-12)
            bonus_ref[fill, :, c] = sums[STEP_ROWS:] * v
            rt_ref[fill, :, c] = r * jnp.exp(lg)
            ops_ref[fill, 0, :, c] = (-kk * jnp.exp(lg - lw)).astype(BF16)
            ops_ref[fill, 1, :, c] = (kk * a * e_inv).astype(BF16)
            ops_ref[fill, 2, :, c] = (k2 * e_inv).astype(BF16)
            ops_ref[fill, 3, :, c] = v.astype(BF16)
            ops_ref[fill, 4, :, c] = (kk * a * e_end).astype(BF16)
            ops_ref[fill, 5, :, c] = (k2 * e_end).astype(BF16)
            yield

    done = object()
    streams = [chain(), prepare()]
    while streams:
        streams = [s for s in streams if next(s, done) is not done]


def _rwkv_time_mix(p_shift, p_gates, batch, seq, w0, w_up, a0, a_up, k_k, k_a, r_k, ln_x_w, ln_x_b):
    nt = seq // STEP_ROWS
    row2 = lambda v: v.reshape(1, -1).astype(F32)
    zeros = jnp.zeros((LORA_RANK, RWKV_W), F32)
    wup_pad = jnp.concatenate([w_up.astype(F32), zeros], axis=0)
    aup_pad = jnp.concatenate([zeros, a_up.astype(F32)], axis=0)
    wu_hi, wu_lo = _split(wup_pad)
    au_hi, au_lo = _split(aup_pad)
    whole = lambda shape: pl.BlockSpec(shape, lambda b, t: (0, 0))
    chan = whole((1, RWKV_W))
    lora = whole((2 * LORA_RANK, RWKV_W))
    prepared = lambda b, t: (b * nt + jnp.minimum(t, nt - 1), 0)
    chained = lambda b, t: (b * nt + jnp.maximum(t - 1, 0), 0)
    return pl.pallas_call(
        _rwkv_kernel,
        grid=(batch, nt + 1),
        in_specs=[pl.BlockSpec((STEP_ROWS, SHIFT_COLS), prepared),
                  pl.BlockSpec((STEP_ROWS, RWKV_W), chained),
                  chan, chan, chan, chan, chan, chan, chan, lora, lora, lora, lora],
        out_specs=pl.BlockSpec((STEP_ROWS, RWKV_W), chained),
        out_shape=jax.ShapeDtypeStruct((batch * seq, RWKV_W), F32),
        scratch_shapes=[pltpu.VMEM((PAIRS, LANES, LANES), F32),
                        pltpu.VMEM((2, OPERANDS, STEP_ROWS, RWKV_W), BF16),
                        pltpu.VMEM((2, STEP_ROWS, RWKV_W), F32),
                        pltpu.VMEM((2, STEP_CHUNKS * SUBLANES, RWKV_W), F32),
                        pltpu.VMEM((2, STEP_ROWS, RWKV_W), F32)],
        compiler_params=pltpu.CompilerParams(
            dimension_semantics=("arbitrary", "arbitrary"), vmem_limit_bytes=VMEM_LIMIT),
        name="rwkv7_time_mix",
    )(p_shift, p_gates, row2(w0), row2(a0), row2(k_k), row2(k_a), row2(r_k), row2(ln_x_w), row2(ln_x_b),
      wu_hi, wu_lo, au_hi, au_lo)


def _attn_kernel(q_ref, k_ref, v_ref, gate_ref, y_ref, acc_scr, max_scr, sum_scr, *, seq):
    n_blocks = seq // ATT_BLK
    qi = lax.broadcasted_iota(jnp.int32, (2 * ATT_BLK, ATT_BLK), 0) % ATT_BLK
    kj = lax.broadcasted_iota(jnp.int32, (2 * ATT_BLK, ATT_BLK), 1)
    ok_prev, ok_cur = kj >= qi, kj <= qi
    head0 = lax.broadcasted_iota(jnp.int32, (ATT_BLK, LANES), 1) < HEAD_DIM
    ones = jnp.ones((ATT_BLK, LANES), BF16)
    scale = HEAD_DIM ** -0.5 * LOG2_E
    neg = -1e30

    def attend(pi, d, kv_starts, blocks):
        rows = lambda s: pl.ds(s, ATT_BLK, stride=d) if d > 1 else pl.ds(s, ATT_BLK)
        kb = [k_ref[rows(s), :].astype(BF16) for s in kv_starts]
        vb = [jnp.concatenate([v_ref[rows(s), :].astype(BF16), ones], axis=1) for s in kv_starts]
        qs = []
        for cur, _, _ in blocks:
            q = q_ref[rows(kv_starts[cur]), :] * scale
            qs.append(jnp.concatenate([jnp.where(head0, q, 0.0), jnp.where(head0, 0.0, q)],
                                      axis=0).astype(BF16))
        scores = []
        for (cur, prev, prev_ok), q in zip(blocks, qs):
            if prev is None:
                scores.append(jnp.where(ok_cur, _mm_nt(q, kb[cur]), neg))
            else:
                s = _mm_nt(q, jnp.concatenate([kb[prev], kb[cur]], axis=0))
                okp = ok_prev if prev_ok is None else ok_prev & prev_ok
                scores.append(jnp.concatenate([jnp.where(okp, s[:, :ATT_BLK], neg),
                                               jnp.where(ok_cur, s[:, ATT_BLK:], neg)], axis=1))
        mx = [jnp.max(s, axis=-1, keepdims=True) for s in scores]
        probs = [jnp.exp2(s - m).astype(BF16) for s, m in zip(scores, mx)]
        pv = [_mm(p, vb[cur] if prev is None else jnp.concatenate([vb[prev], vb[cur]], axis=0))
              for (cur, prev, _), p in zip(blocks, probs)]
        for (cur, _, _), m, x in zip(blocks, mx, pv):
            at = rows(kv_starts[cur])
            acc_scr[pi, at, :] = jnp.where(head0, x[:ATT_BLK, :LANES], x[ATT_BLK:, :LANES])
            sum_scr[pi, at, :] = jnp.where(head0, x[:ATT_BLK, LANES:], x[ATT_BLK:, LANES:])
            max_scr[pi, at, :] = jnp.where(head0, m[:ATT_BLK], m[ATT_BLK:])

    for pi, d in enumerate(DILATIONS):
        nb = n_blocks // d
        span = ATT_BLK * d
        if nb <= ATT_GROUP:
            n_res = ATT_GROUP // nb
            def body(it, carry, pi=pi, d=d, span=span, nb=nb, n_res=n_res):
                starts = [it * n_res + r + u * span for r in range(n_res) for u in range(nb)]
                blocks = [(r * nb + u, r * nb + u - 1 if u else None, None)
                          for r in range(n_res) for u in range(nb)]
                attend(pi, d, starts, blocks)
                return carry
            trips = d // n_res
        else:
            per_res = nb // ATT_GROUP
            def body(it, carry, pi=pi, d=d, span=span, per_res=per_res):
                res, blk0 = it // per_res, (it % per_res) * ATT_GROUP
                starts = [res + jnp.maximum(blk0 - 1, 0) * span]
                starts += [res + (blk0 + u) * span for u in range(ATT_GROUP)]
                attend(pi, d, starts,
                       [(1, 0, blk0 > 0)] + [(u + 1, u, None) for u in range(1, ATT_GROUP)])
                return carry
            trips = d * per_res
        lax.fori_loop(0, trips, body, 0)

    m = jnp.maximum(jnp.maximum(max_scr[0], max_scr[1]), max_scr[2])
    num = den = None
    for pi in range(len(DILATIONS)):
        w = jnp.exp2(max_scr[pi] - m)
        num = w * acc_scr[pi] if num is None else num + w * acc_scr[pi]
        den = w * sum_scr[pi] if den is None else den + w * sum_scr[pi]
    y_ref[...] = num / den * gate_ref[...]


def _dilated_attention(p_att, p_gates, batch, seq):
    def tile(blk0):
        return pl.BlockSpec((seq, LANES), lambda b, g: (b, blk0 + g))

    return pl.pallas_call(
        functools.partial(_attn_kernel, seq=seq),
        grid=(batch, PAIRS),
        in_specs=[tile(0), tile(PAIRS), tile(2 * PAIRS), tile(PAIRS)],
        out_specs=pl.BlockSpec((seq, LANES), lambda b, g: (b, g)),
        out_shape=jax.ShapeDtypeStruct((batch * seq, ATT_W), F32),
        scratch_shapes=[pltpu.VMEM((len(DILATIONS), seq, LANES), F32)] * 3,
        compiler_params=pltpu.CompilerParams(
            dimension_semantics=("arbitrary", "arbitrary"), vmem_limit_bytes=VMEM_LIMIT),
        name="dilated_attention",
    )(p_att, p_att, p_att, p_gates)


def _out_kernel(yr_ref, ya_ref, x_ref, wr_ref, wa_ref, fw_ref, o_ref):
    h = (x_ref[...]
         + jnp.dot(yr_ref[...].astype(BF16), wr_ref[...], preferred_element_type=F32)
         + jnp.dot(ya_ref[...].astype(BF16), wa_ref[...], preferred_element_type=F32))
    ms = jnp.mean(h * h, axis=-1, keepdims=True)
    o_ref[...] = h * lax.rsqrt(ms + RMS_EPS) * fw_ref[...]


def _out_projection(y_rwkv, y_att, x2, w_out_bf16, final_norm_w, tm):
    tokens = x2.shape[0]
    tok = lambda width: pl.BlockSpec((tm, width), lambda i: (i, 0))
    return pl.pallas_call(
        _out_kernel,
        grid=(tokens // tm,),
        in_specs=[
            tok(RWKV_W), tok(ATT_W), tok(D_MODEL),
            pl.BlockSpec((RWKV_W, D_MODEL), lambda i: (0, 0)),
            pl.BlockSpec((ATT_W, D_MODEL), lambda i: (1, 0)),
            pl.BlockSpec((1, D_MODEL), lambda i: (0, 0)),
        ],
        out_specs=tok(D_MODEL),
        out_shape=jax.ShapeDtypeStruct((tokens, D_MODEL), F32),
        compiler_params=pltpu.CompilerParams(
            dimension_semantics=("arbitrary",), vmem_limit_bytes=VMEM_LIMIT),
        name="out_projection",
    )(y_rwkv, y_att, x2, w_out_bf16, w_out_bf16, final_norm_w.reshape(1, D_MODEL))


def kernel(x, norm_w, w_in, mu_shift, w0, w_up, a0, a_up, k_k, k_a, r_k, ln_x_w, ln_x_b, w_out,
           final_norm_w):
    batch, seq, _ = x.shape
    assert seq % (ATT_BLK * max(DILATIONS)) == 0 and seq % STEP_ROWS == 0
    x2 = x.astype(F32).reshape(batch * seq, D_MODEL)
    nw = norm_w.astype(F32)
    w_bf16 = w_in.astype(BF16)
    w_gates = jnp.concatenate([w_bf16[:, RWKV_GATE0:ATT0], w_bf16[:, ATT_GATE0:]], axis=1)
    p_shift = _in_projection(x2, nw, w_bf16[:, :SHIFT_COLS], seq, "shift", mu_shift)
    p_gates = _in_projection(x2, nw, w_gates, seq, "silu")
    p_att = _in_projection(x2, nw, w_bf16[:, ATT0:ATT_GATE0], seq, "plain")
    y_rwkv = _rwkv_time_mix(p_shift, p_gates, batch, seq, w0, w_up, a0, a_up, k_k, k_a, r_k,
                            ln_x_w, ln_x_b)
    y_att = _dilated_attention(p_att, p_gates, batch, seq)
    out = _out_projection(y_rwkv, y_att, x2, w_out.astype(BF16), final_norm_w.astype(F32),
                          tm=min(512, batch * seq))
    return out.reshape(batch, seq, D_MODEL).astype(x.dtype)
```

```python
import functools

import jax
import jax.numpy as jnp
from jax import lax
from jax.experimental import pallas as pl
from jax.experimental.pallas import tpu as pltpu

D_MODEL = 1024
HEAD_DIM = 64
RWKV_W = 1024
ATT_W = 1024
MIX_W = RWKV_W + ATT_W
LORA_RANK = 64
SHIFT_COLS = 3 * RWKV_W + 2 * LORA_RANK
RWKV_GATE0 = SHIFT_COLS
ATT0 = RWKV_GATE0 + RWKV_W
ATT_GATE0 = ATT0 + 3 * ATT_W
DILATIONS = (1, 4, 16)
ATT_BLK = 128
ATT_GROUP = 8
LOG2_E = 1.4426950408889634
RMS_EPS = 1e-5
GN_EPS = 64e-5

LANES = 128
SUBLANES = 8
PAIRS = RWKV_W // LANES
CHUNK = 64
K_COL0, V_COL0, LORA_COL0 = RWKV_W, 2 * RWKV_W, 3 * RWKV_W
PROJ_TM = 512

F32 = jnp.float32
BF16 = jnp.bfloat16
VMEM_LIMIT = 56 * 1024 * 1024

NT_DIMS = (((1,), (1,)), ((), ()))
TN_DIMS = (((0,), (0,)), ((), ()))


def _mm(a, b):
    return jnp.dot(a.astype(BF16), b.astype(BF16), preferred_element_type=F32)


def _mm_nt(a, b):
    return lax.dot_general(a.astype(BF16), b.astype(BF16), NT_DIMS, preferred_element_type=F32)


def _mm_tn(a, b):
    return lax.dot_general(a.astype(BF16), b.astype(BF16), TN_DIMS, preferred_element_type=F32)


def _split(x):
    hi = x.astype(BF16)
    return hi, (x - hi.astype(F32)).astype(BF16)


def _mm_split_lhs(x, w_bf16):
    hi, lo = _split(x)
    return (jnp.dot(hi, w_bf16, preferred_element_type=F32)
            + jnp.dot(lo, w_bf16, preferred_element_type=F32))


def _mm_split_rhs(w_bf16, x):
    hi, lo = _split(x)
    return (jnp.dot(w_bf16, hi, preferred_element_type=F32)
            + jnp.dot(w_bf16, lo, preferred_element_type=F32))


def _mm_x3(x, w_hi, w_lo):
    hi, lo = _split(x)
    return (jnp.dot(hi, w_hi, preferred_element_type=F32)
            + jnp.dot(lo, w_hi, preferred_element_type=F32)
            + jnp.dot(hi, w_lo, preferred_element_type=F32))


def _sigmoid(x):
    return 1.0 / (1.0 + jnp.exp(-x))


def _proj_kernel(x_ref, nw_ref, w_ref, *rest, epilogue, tiles_per_seq):
    x = x_ref[...]
    ms = jnp.mean(x * x, axis=-1, keepdims=True)
    xn = (x * lax.rsqrt(ms + RMS_EPS) * nw_ref[...]).astype(BF16)
    o = jnp.dot(xn, w_ref[...], preferred_element_type=F32)
    if epilogue == "shift":
        mu_ref, o_ref, prev_ref = rest
        tm = o.shape[0]
        first = pl.program_id(0) % tiles_per_seq == 0
        last = jnp.where(first, 0.0, prev_ref[SUBLANES - 1:SUBLANES, :])
        row = lax.broadcasted_iota(jnp.int32, (tm, 1), 0)
        oprev = jnp.where(row == 0, last, pltpu.roll(o, 1, 0))
        prev_ref[...] = o[tm - SUBLANES:, :]
        o_ref[...] = o + (oprev - o) * mu_ref[...]
    elif epilogue == "silu":
        (o_ref,) = rest
        o_ref[...] = (o * _sigmoid(o)).astype(o_ref.dtype)
    else:
        (o_ref,) = rest
        o_ref[...] = o


def _in_projection(x2, norm_w, w_bf16, seq, epilogue, mu=None, out_dtype=F32):
    tokens, n = x2.shape[0], w_bf16.shape[1]
    tm = min(PROJ_TM, seq)
    in_specs = [pl.BlockSpec((tm, D_MODEL), lambda i: (i, 0)),
                pl.BlockSpec((1, D_MODEL), lambda i: (0, 0)),
                pl.BlockSpec((D_MODEL, n), lambda i: (0, 0))]
    args = [x2, norm_w.reshape(1, D_MODEL), w_bf16]
    scratch = []
    if epilogue == "shift":
        in_specs.append(pl.BlockSpec((1, n), lambda i: (0, 0)))
        args.append(mu.reshape(1, n).astype(F32))
        scratch.append(pltpu.VMEM((SUBLANES, n), F32))
    return pl.pallas_call(
        functools.partial(_proj_kernel, epilogue=epilogue, tiles_per_seq=seq // tm),
        grid=(tokens // tm,),
        in_specs=in_specs,
        out_specs=pl.BlockSpec((tm, n), lambda i: (i, 0)),
        out_shape=jax.ShapeDtypeStruct((tokens, n), out_dtype),
        scratch_shapes=scratch,
        compiler_params=pltpu.CompilerParams(
            dimension_semantics=("arbitrary",), vmem_limit_bytes=VMEM_LIMIT),
        name="in_projection_" + epilogue,
    )(*args)


OPERANDS = 6
STEP_CHUNKS = 2
STEP_ROWS = STEP_CHUNKS * CHUNK


def _rwkv_kernel(p_ref, gate_ref, w0_ref, a0_ref, kk_ref, ka_ref, rk_ref, lnw_ref, lnb_ref,
                 wuh_ref, wul_ref, auh_ref, aul_ref, y_ref, state_ref, ops_ref, rt_ref, gend_ref, bonus_ref,
                 *, steps_per_seq):
    t = pl.program_id(0)

    @pl.when(t == 0)
    def _():
        ops_ref[...] = jnp.zeros_like(ops_ref)
        rt_ref[...] = jnp.zeros_like(rt_ref)
        gend_ref[...] = jnp.zeros_like(gend_ref)
        bonus_ref[...] = jnp.zeros_like(bonus_ref)
        state_ref[...] = jnp.zeros_like(state_ref)

    fill, use = t % 2, (t + 1) % 2
    row = lax.broadcasted_iota(jnp.int32, (CHUNK, LANES), 0)
    lane = lax.broadcasted_iota(jnp.int32, (CHUNK, LANES), 1)
    head0 = lane < HEAD_DIM
    tcol = lane % CHUNK
    strict, incl = tcol < row, tcol <= row
    same16 = (tcol // 16) == (row // 16)
    same32 = (tcol // 32) == (row // 32)
    eye = jnp.where(tcol == row, 1.0, 0.0).astype(F32)
    bi = lax.broadcasted_iota(jnp.int32, (LANES, LANES), 0)
    bj = lax.broadcasted_iota(jnp.int32, (LANES, LANES), 1)
    block_diag = (bi // HEAD_DIM) == (bj // HEAD_DIM)
    head_ones = jnp.where(block_diag, 1.0, 0.0).astype(BF16)
    ti = lax.broadcasted_iota(jnp.int32, (STEP_ROWS, STEP_ROWS), 0)
    tj = lax.broadcasted_iota(jnp.int32, (STEP_ROWS, STEP_ROWS), 1)
    tri = jnp.where((tj <= ti) & (tj // CHUNK == ti // CHUNK), 1.0, 0.0).astype(BF16)
    pairs = range(PAIRS)
    chunks = range(STEP_CHUNKS)
    units = [(c, g) for c in chunks for g in pairs]
    cols = lambda c0, g: slice(c0 + g * LANES, c0 + (g + 1) * LANES)
    rows = lambda c: slice(c * CHUNK, (c + 1) * CHUNK)

    def stack(x):
        x = x.astype(BF16)
        zero = jnp.zeros_like(x)
        return jnp.concatenate([jnp.where(head0, x, zero), jnp.where(head0, zero, x)], axis=0)

    def head_sums(xs):
        s = _mm_split_lhs(jnp.concatenate(xs, axis=0), head_ones)
        return [s[i * CHUNK:(i + 1) * CHUNK] for i in range(len(xs))]

    def chain():
        ops = lambda i, c, g: ops_ref[use, i, rows(c), cols(0, g)]
        a_t, b_t, k_t, v, b_h, k_h = ([ops(i, c, g) for c, g in units] for i in range(OPERANDS))
        r_t = [rt_ref[use, rows(c), cols(0, g)] for c, g in units]
        n = range(len(units))
        v_s = [stack(x) for x in v]
        gram = [_mm_nt(jnp.concatenate([a_t[u], r_t[u]], axis=0),
                       jnp.concatenate([stack(b_t[u]), stack(k_t[u])], axis=0)) for u in n]
        yield
        a_ab = [jnp.where(strict, x[:CHUNK, :LANES], 0.0) for x in gram]
        a_ak = [jnp.where(strict, x[:CHUNK, LANES:], 0.0) for x in gram]
        a_rb = [jnp.where(incl, x[CHUNK:, :LANES], 0.0) for x in gram]
        a_rk = [jnp.where(incl, x[CHUNK:, LANES:], 0.0) for x in gram]
        nil = [jnp.where(same16, x, 0.0) for x in a_ab]
        inv = [eye + x for x in nil]
        nil = [_mm(x, stack(x)) for x in nil]
        yield
        av = [_mm(jnp.concatenate([a_ak[u], a_rk[u]], axis=0), v_s[u]) for u in n]
        yield
        for _ in range(2):
            both = [_mm(jnp.concatenate([nil[u], inv[u]], axis=0), stack(nil[u])) for u in n]
            nil = [x[:CHUNK] for x in both]
            inv = [inv[u] + both[u][CHUNK:] for u in n]
            yield
        vk = [_mm_tn(v[u], k_h[u]) for u in n]
        yield
        inv = [inv[u] + _mm(inv[u], stack(nil[u])) for u in n]
        yield
        for off_diag in (same32 & ~same16, ~same32):
            low = [_mm(inv[u], stack(jnp.where(off_diag, a_ab[u], 0.0))) for u in n]
            yield
            inv = [inv[u] + _mm(low[u], stack(inv[u])) for u in n]
            yield
        tx = [_mm(inv[u], jnp.concatenate([stack(a_t[u]), stack(av[u][:CHUNK])], axis=1))
              for u in n]
        yield
        zz = [_mm(a_rb[u], jnp.concatenate([stack(tx[u][:, :LANES]), stack(tx[u][:, LANES:])], axis=1))
              for u in n]
        yield
        r_p = [r_t[u] + zz[u][:, :LANES] for u in n]
        y0 = [zz[u][:, LANES:] + av[u][CHUNK:] for u in n]
        mn = [_mm_tn(tx[u], b_h[u]) for u in n]
        yield
        fresh = lax.rem(t - 1, steps_per_seq) == 0
        state = [jnp.where(fresh, 0.0, state_ref[g]) for g in pairs]
        y = []
        for u, (c, g) in enumerate(units):
            y.append(_mm_nt(r_p[u], state[g]) + y0[u])
            m_t = jnp.where(block_diag, mn[u][:LANES], 0.0)
            n_t = jnp.where(block_diag, mn[u][LANES:] + vk[u], 0.0)
            g_end = gend_ref[use, c * SUBLANES:c * SUBLANES + 1, cols(0, g)]
            state[g] = state[g] * g_end + _mm(state[g], m_t) + n_t
            if g == PAIRS - 1:
                yield
        for g in pairs:
            state_ref[g] = state[g]
        mean = head_sums(y)
        yc = [y[u] - mean[u] * (1.0 / HEAD_DIM) for u in n]
        var = head_sums([x * x for x in yc])
        for u, (c, g) in enumerate(units):
            yn = yc[u] * lax.rsqrt(var[u] * (1.0 / HEAD_DIM) + GN_EPS)
            yn = yn * lnw_ref[:, cols(0, g)] + lnb_ref[:, cols(0, g)]
            y_ref[rows(c), cols(0, g)] = ((yn + bonus_ref[use, rows(c), cols(0, g)])
                                          * gate_ref[rows(c), cols(0, g)].astype(F32)).astype(y_ref.dtype)

    def prepare():
        lo = p_ref[:, LORA_COL0:LORA_COL0 + LANES]
        z_all = -(w0_ref[...] + _mm_x3(jnp.tanh(lo), wuh_ref[...], wul_ref[...]))
        a_all = a0_ref[...] + _mm_x3(lo, auh_ref[...], aul_ref[...])
        yield
        for g in pairs:
            c = cols(0, g)
            z = z_all[:, c]
            w = -(jnp.maximum(z, 0.0) + jnp.log(1.0 + jnp.exp(-jnp.abs(z)))) - 0.5
            lw = -jnp.exp(w)
            lg = _mm_split_rhs(tri, lw)
            ends = [lg[(i + 1) * CHUNK - 1:(i + 1) * CHUNK, :] for i in chunks]
            lg_end = jnp.concatenate([jnp.broadcast_to(e, (CHUNK, LANES)) for e in ends], axis=0)
            for i in chunks:
                gend_ref[fill, i * SUBLANES:(i + 1) * SUBLANES, c] = jnp.broadcast_to(
                    jnp.exp(ends[i]), (SUBLANES, LANES))
            e_inv = jnp.exp(-lg)
            e_end = jnp.exp(lg_end - lg)
            a = _sigmoid(a_all[:, c])
            r, k, v = p_ref[:, c], p_ref[:, cols(K_COL0, g)], p_ref[:, cols(V_COL0, g)]
            kk = k * kk_ref[:, c]
            k2 = k * (1.0 + (a - 1.0) * ka_ref[:, c])
            sums = _mm_split_lhs(jnp.concatenate([kk * kk, r * k2 * rk_ref[:, c]], axis=0), head_ones)
            kk = kk / jnp.maximum(jnp.sqrt(sums[:STEP_ROWS]), 1e-12)
            bonus_ref[fill, :, c] = sums[STEP_ROWS:] * v
            rt_ref[fill, :, c] = r * jnp.exp(lg)
            ops_ref[fill, 0, :, c] = (-kk * jnp.exp(lg - lw)).astype(BF16)
            ops_ref[fill, 1, :, c] = (kk * a * e_inv).astype(BF16)
            ops_ref[fill, 2, :, c] = (k2 * e_inv).astype(BF16)
            ops_ref[fill, 3, :, c] = v.astype(BF16)
            ops_ref[fill, 4, :, c] = (kk * a * e_end).astype(BF16)
            ops_ref[fill, 5, :, c] = (k2 * e_end).astype(BF16)
            yield

    done = object()
    streams = [chain(), prepare()]
    while streams:
        streams = [s for s in streams if next(s, done) is not done]


def _rwkv_time_mix(p_shift, p_gates, batch, seq, w0, w_up, a0, a_up, k_k, k_a, r_k, ln_x_w, ln_x_b):
    nt = batch * seq // STEP_ROWS
    row2 = lambda v: v.reshape(1, -1).astype(F32)
    zeros = jnp.zeros((LORA_RANK, RWKV_W), F32)
    wup_pad = jnp.concatenate([w_up.astype(F32), zeros], axis=0)
    aup_pad = jnp.concatenate([zeros, a_up.astype(F32)], axis=0)
    wu_hi, wu_lo = _split(wup_pad)
    au_hi, au_lo = _split(aup_pad)
    whole = lambda shape: pl.BlockSpec(shape, lambda t: (0, 0))
    chan = whole((1, RWKV_W))
    lora = whole((2 * LORA_RANK, RWKV_W))
    prepared = lambda t: (jnp.minimum(t, nt - 1), 0)
    chained = lambda t: (jnp.maximum(t - 1, 0), 0)
    return pl.pallas_call(
        functools.partial(_rwkv_kernel, steps_per_seq=seq // STEP_ROWS),
        grid=(nt + 1,),
        in_specs=[pl.BlockSpec((STEP_ROWS, SHIFT_COLS), prepared),
                  pl.BlockSpec((STEP_ROWS, RWKV_W), chained),
                  chan, chan, chan, chan, chan, chan, chan, lora, lora, lora, lora],
        out_specs=pl.BlockSpec((STEP_ROWS, RWKV_W), chained),
        out_shape=jax.ShapeDtypeStruct((batch * seq, RWKV_W), BF16),
        scratch_shapes=[pltpu.VMEM((PAIRS, LANES, LANES), F32),
                        pltpu.VMEM((2, OPERANDS, STEP_ROWS, RWKV_W), BF16),
                        pltpu.VMEM((2, STEP_ROWS, RWKV_W), F32),
                        pltpu.VMEM((2, STEP_CHUNKS * SUBLANES, RWKV_W), F32),
                        pltpu.VMEM((2, STEP_ROWS, RWKV_W), F32)],
        compiler_params=pltpu.CompilerParams(
            dimension_semantics=("arbitrary",), vmem_limit_bytes=VMEM_LIMIT),
        name="rwkv7_time_mix",
    )(p_shift, p_gates, row2(w0), row2(a0), row2(k_k), row2(k_a), row2(r_k), row2(ln_x_w), row2(ln_x_b),
      wu_hi, wu_lo, au_hi, au_lo)


def _attn_kernel(q_ref, k_ref, v_ref, gate_ref, y_ref, acc_scr, max_scr, sum_scr, *, seq):
    n_blocks = seq // ATT_BLK
    qi = lax.broadcasted_iota(jnp.int32, (2 * ATT_BLK, ATT_BLK), 0) % ATT_BLK
    kj = lax.broadcasted_iota(jnp.int32, (2 * ATT_BLK, ATT_BLK), 1)
    ok_prev, ok_cur = kj >= qi, kj <= qi
    head0 = lax.broadcasted_iota(jnp.int32, (ATT_BLK, LANES), 1) < HEAD_DIM
    ones = jnp.ones((ATT_BLK, LANES), BF16)
    scale = HEAD_DIM ** -0.5 * LOG2_E
    neg = -1e30

    def attend(pi, d, kv_starts, blocks):
        rows = lambda s: pl.ds(s, ATT_BLK, stride=d) if d > 1 else pl.ds(s, ATT_BLK)
        kb = [k_ref[rows(s), :].astype(BF16) for s in kv_starts]
        vb = [jnp.concatenate([v_ref[rows(s), :].astype(BF16), ones], axis=1) for s in kv_starts]
        qs = []
        for cur, _, _ in blocks:
            q = q_ref[rows(kv_starts[cur]), :] * scale
            qs.append(jnp.concatenate([jnp.where(head0, q, 0.0), jnp.where(head0, 0.0, q)],
                                      axis=0).astype(BF16))
        scores = []
        for (cur, prev, prev_ok), q in zip(blocks, qs):
            if prev is None:
                scores.append(jnp.where(ok_cur, _mm_nt(q, kb[cur]), neg))
            else:
                s = _mm_nt(q, jnp.concatenate([kb[prev], kb[cur]], axis=0))
                okp = ok_prev if prev_ok is None else ok_prev & prev_ok
                scores.append(jnp.concatenate([jnp.where(okp, s[:, :ATT_BLK], neg),
                                               jnp.where(ok_cur, s[:, ATT_BLK:], neg)], axis=1))
        mx = [jnp.max(s, axis=-1, keepdims=True) for s in scores]
        probs = [jnp.exp2(s - m).astype(BF16) for s, m in zip(scores, mx)]
        pv = [_mm(p, vb[cur] if prev is None else jnp.concatenate([vb[prev], vb[cur]], axis=0))
              for (cur, prev, _), p in zip(blocks, probs)]
        for (cur, _, _), m, x in zip(blocks, mx, pv):
            at = rows(kv_starts[cur])
            acc_scr[pi, at, :] = jnp.where(head0, x[:ATT_BLK, :LANES], x[ATT_BLK:, :LANES])
            sum_scr[pi, at, :] = jnp.where(head0, x[:ATT_BLK, LANES:], x[ATT_BLK:, LANES:])
            max_scr[pi, at, :] = jnp.where(head0, m[:ATT_BLK], m[ATT_BLK:])

    for pi, d in enumerate(DILATIONS):
        nb = n_blocks // d
        span = ATT_BLK * d
        if nb <= ATT_GROUP:
            n_res = ATT_GROUP // nb
            def body(it, carry, pi=pi, d=d, span=span, nb=nb, n_res=n_res):
                starts = [it * n_res + r + u * span for r in range(n_res) for u in range(nb)]
                blocks = [(r * nb + u, r * nb + u - 1 if u else None, None)
                          for r in range(n_res) for u in range(nb)]
                attend(pi, d, starts, blocks)
                return carry
            trips = d // n_res
        else:
            per_res = nb // ATT_GROUP
            def body(it, carry, pi=pi, d=d, span=span, per_res=per_res):
                res, blk0 = it // per_res, (it % per_res) * ATT_GROUP
                starts = [res + jnp.maximum(blk0 - 1, 0) * span]
                starts += [res + (blk0 + u) * span for u in range(ATT_GROUP)]
                attend(pi, d, starts,
                       [(1, 0, blk0 > 0)] + [(u + 1, u, None) for u in range(1, ATT_GROUP)])
                return carry
            trips = d * per_res
        lax.fori_loop(0, trips, body, 0)

    m = jnp.maximum(jnp.maximum(max_scr[0], max_scr[1]), max_scr[2])
    num = den = None
    for pi in range(len(DILATIONS)):
        w = jnp.exp2(max_scr[pi] - m)
        num = w * acc_scr[pi] if num is None else num + w * acc_scr[pi]
        den = w * sum_scr[pi] if den is None else den + w * sum_scr[pi]
    y_ref[...] = (num / den * gate_ref[...].astype(F32)).astype(y_ref.dtype)


def _dilated_attention(p_att, p_gates, batch, seq):
    def tile(blk0):
        return pl.BlockSpec((seq, LANES), lambda b, g: (b, blk0 + g))

    return pl.pallas_call(
        functools.partial(_attn_kernel, seq=seq),
        grid=(batch, PAIRS),
        in_specs=[tile(0), tile(PAIRS), tile(2 * PAIRS), tile(PAIRS)],
        out_specs=pl.BlockSpec((seq, LANES), lambda b, g: (b, g)),
        out_shape=jax.ShapeDtypeStruct((batch * seq, ATT_W), BF16),
        scratch_shapes=[pltpu.VMEM((len(DILATIONS), seq, LANES), F32)] * 3,
        compiler_params=pltpu.CompilerParams(
            dimension_semantics=("arbitrary", "arbitrary"), vmem_limit_bytes=VMEM_LIMIT),
        name="dilated_attention",
    )(p_att, p_att, p_att, p_gates)


def _out_kernel(yr_ref, ya_ref, x_ref, wr_ref, wa_ref, fw_ref, o_ref):
    h = (x_ref[...]
         + jnp.dot(yr_ref[...], wr_ref[...], preferred_element_type=F32)
         + jnp.dot(ya_ref[...], wa_ref[...], preferred_element_type=F32))
    ms = jnp.mean(h * h, axis=-1, keepdims=True)
    o_ref[...] = h * lax.rsqrt(ms + RMS_EPS) * fw_ref[...]


def _out_projection(y_rwkv, y_att, x2, w_out_bf16, final_norm_w, tm):
    tokens = x2.shape[0]
    tok = lambda width: pl.BlockSpec((tm, width), lambda i: (i, 0))
    return pl.pallas_call(
        _out_kernel,
        grid=(tokens // tm,),
        in_specs=[
            tok(RWKV_W), tok(ATT_W), tok(D_MODEL),
            pl.BlockSpec((RWKV_W, D_MODEL), lambda i: (0, 0)),
            pl.BlockSpec((ATT_W, D_MODEL), lambda i: (1, 0)),
            pl.BlockSpec((1, D_MODEL), lambda i: (0, 0)),
        ],
        out_specs=tok(D_MODEL),
        out_shape=jax.ShapeDtypeStruct((tokens, D_MODEL), F32),
        compiler_params=pltpu.CompilerParams(
            dimension_semantics=("arbitrary",), vmem_limit_bytes=VMEM_LIMIT),
        name="out_projection",
    )(y_rwkv, y_att, x2, w_out_bf16, w_out_bf16, final_norm_w.reshape(1, D_MODEL))


def kernel(x, norm_w, w_in, mu_shift, w0, w_up, a0, a_up, k_k, k_a, r_k, ln_x_w, ln_x_b, w_out,
           final_norm_w):
    batch, seq, _ = x.shape
    assert seq % (ATT_BLK * max(DILATIONS)) == 0 and seq % STEP_ROWS == 0
    x2 = x.astype(F32).reshape(batch * seq, D_MODEL)
    nw = norm_w.astype(F32)
    w_bf16 = w_in.astype(BF16)
    w_gates = jnp.concatenate([w_bf16[:, RWKV_GATE0:ATT0], w_bf16[:, ATT_GATE0:]], axis=1)
    p_shift = _in_projection(x2, nw, w_bf16[:, :SHIFT_COLS], seq, "shift", mu_shift)
    p_gates = _in_projection(x2, nw, w_gates, seq, "silu", out_dtype=BF16)
    p_att = _in_projection(x2, nw, w_bf16[:, ATT0:ATT_GATE0], seq, "plain")
    y_rwkv = _rwkv_time_mix(p_shift, p_gates, batch, seq, w0, w_up, a0, a_up, k_k, k_a, r_k,
                            ln_x_w, ln_x_b)
    y_att = _dilated_attention(p_att, p_gates, batch, seq)
    out = _out_projection(y_rwkv, y_att, x2, w_out.astype(BF16), final_norm_w.astype(F32),
                          tm=min(512, batch * seq))
    return out.reshape(batch, seq, D_MODEL).astype(x.dtype)
```

```python
import functools

import jax
import jax.numpy as jnp
from jax import lax
from jax.experimental import pallas as pl
from jax.experimental.pallas import tpu as pltpu

D_MODEL = 1024
HEAD_DIM = 64
RWKV_W = 1024
ATT_W = 1024
MIX_W = RWKV_W + ATT_W
LORA_RANK = 64
SHIFT_COLS = 3 * RWKV_W + 2 * LORA_RANK
RWKV_GATE0 = SHIFT_COLS
ATT0 = RWKV_GATE0 + RWKV_W
ATT_GATE0 = ATT0 + 3 * ATT_W
DILATIONS = (1, 4, 16)
ATT_BLK = 128
ATT_GROUP = 8
LOG2_E = 1.4426950408889634
RMS_EPS = 1e-5
GN_EPS = 64e-5

LANES = 128
SUBLANES = 8
PAIRS = RWKV_W // LANES
CHUNK = 64
K_COL0, V_COL0, LORA_COL0 = RWKV_W, 2 * RWKV_W, 3 * RWKV_W
PROJ_TM = 512

F32 = jnp.float32
BF16 = jnp.bfloat16
VMEM_LIMIT = 56 * 1024 * 1024

NT_DIMS = (((1,), (1,)), ((), ()))
TN_DIMS = (((0,), (0,)), ((), ()))


def _mm(a, b):
    return jnp.dot(a.astype(BF16), b.astype(BF16), preferred_element_type=F32)


def _mm_nt(a, b):
    return lax.dot_general(a.astype(BF16), b.astype(BF16), NT_DIMS, preferred_element_type=F32)


def _mm_tn(a, b):
    return lax.dot_general(a.astype(BF16), b.astype(BF16), TN_DIMS, preferred_element_type=F32)


def _split(x):
    hi = x.astype(BF16)
    return hi, (x - hi.astype(F32)).astype(BF16)


def _mm_split_lhs(x, w2_bf16):
    return jnp.dot(jnp.concatenate(_split(x), axis=1), w2_bf16, preferred_element_type=F32)


def _mm_split_rhs(w_bf16, x):
    n = x.shape[1]
    both = jnp.dot(w_bf16, jnp.concatenate(_split(x), axis=1), preferred_element_type=F32)
    return both[:, :n] + both[:, n:]


def _mm_x3(x, w3):
    hi, lo = _split(x)
    return jnp.dot(jnp.concatenate([hi, lo, hi], axis=1), w3, preferred_element_type=F32)


def _stack_x3(w):
    hi, lo = _split(w.astype(F32))
    return jnp.concatenate([hi, hi, lo], axis=0)


def _sigmoid(x):
    return 1.0 / (1.0 + jnp.exp(-x))


def _proj_kernel(x_ref, nw_ref, w_ref, *rest, epilogue, tiles_per_seq):
    x = x_ref[...]
    ms = jnp.mean(x * x, axis=-1, keepdims=True)
    xn = (x * lax.rsqrt(ms + RMS_EPS) * nw_ref[...]).astype(BF16)
    o = jnp.dot(xn, w_ref[...], preferred_element_type=F32)
    if epilogue == "shift":
        mu_ref, o_ref, prev_ref = rest
        tm = o.shape[0]
        first = pl.program_id(0) % tiles_per_seq == 0
        last = jnp.where(first, 0.0, prev_ref[SUBLANES - 1:SUBLANES, :])
        row = lax.broadcasted_iota(jnp.int32, (tm, 1), 0)
        oprev = jnp.where(row == 0, last, pltpu.roll(o, 1, 0))
        prev_ref[...] = o[tm - SUBLANES:, :]
        o_ref[...] = o + (oprev - o) * mu_ref[...]
    elif epilogue == "silu":
        (o_ref,) = rest
        o_ref[...] = (o * _sigmoid(o)).astype(o_ref.dtype)
    else:
        (o_ref,) = rest
        o_ref[...] = o


def _in_projection(x2, norm_w, w_bf16, seq, epilogue, mu=None, out_dtype=F32):
    tokens, n = x2.shape[0], w_bf16.shape[1]
    tm = min(PROJ_TM, seq)
    in_specs = [pl.BlockSpec((tm, D_MODEL), lambda i: (i, 0)),
                pl.BlockSpec((1, D_MODEL), lambda i: (0, 0)),
                pl.BlockSpec((D_MODEL, n), lambda i: (0, 0))]
    args = [x2, norm_w.reshape(1, D_MODEL), w_bf16]
    scratch = []
    if epilogue == "shift":
        in_specs.append(pl.BlockSpec((1, n), lambda i: (0, 0)))
        args.append(mu.reshape(1, n).astype(F32))
        scratch.append(pltpu.VMEM((SUBLANES, n), F32))
    return pl.pallas_call(
        functools.partial(_proj_kernel, epilogue=epilogue, tiles_per_seq=seq // tm),
        grid=(tokens // tm,),
        in_specs=in_specs,
        out_specs=pl.BlockSpec((tm, n), lambda i: (i, 0)),
        out_shape=jax.ShapeDtypeStruct((tokens, n), out_dtype),
        scratch_shapes=scratch,
        compiler_params=pltpu.CompilerParams(
            dimension_semantics=("arbitrary",), vmem_limit_bytes=VMEM_LIMIT),
        name="in_projection_" + epilogue,
    )(*args)


OPERANDS = 6
STEP_CHUNKS = 2
STEP_ROWS = STEP_CHUNKS * CHUNK


def _rwkv_kernel(p_ref, gate_ref, w0_ref, a0_ref, kk_ref, ka_ref, rk_ref, lnw_ref, lnb_ref,
                 wu3_ref, au3_ref, y_ref, state_ref, ops_ref, rt_ref, gend_ref, bonus_ref,
                 *, steps_per_seq):
    t = pl.program_id(0)

    @pl.when(t == 0)
    def _():
        ops_ref[...] = jnp.zeros_like(ops_ref)
        rt_ref[...] = jnp.zeros_like(rt_ref)
        gend_ref[...] = jnp.zeros_like(gend_ref)
        bonus_ref[...] = jnp.zeros_like(bonus_ref)
        state_ref[...] = jnp.zeros_like(state_ref)

    fill, use = t % 2, (t + 1) % 2
    row = lax.broadcasted_iota(jnp.int32, (CHUNK, LANES), 0)
    lane = lax.broadcasted_iota(jnp.int32, (CHUNK, LANES), 1)
    head0 = lane < HEAD_DIM
    tcol = lane % CHUNK
    strict, incl = tcol < row, tcol <= row
    same16 = (tcol // 16) == (row // 16)
    same32 = (tcol // 32) == (row // 32)
    eye = jnp.where(tcol == row, 1.0, 0.0).astype(F32)
    bi = lax.broadcasted_iota(jnp.int32, (LANES, LANES), 0)
    bj = lax.broadcasted_iota(jnp.int32, (LANES, LANES), 1)
    block_diag = (bi // HEAD_DIM) == (bj // HEAD_DIM)
    head_ones = jnp.where(block_diag, 1.0, 0.0).astype(BF16)
    head_ones2 = jnp.concatenate([head_ones, head_ones], axis=0)
    ti = lax.broadcasted_iota(jnp.int32, (STEP_ROWS, STEP_ROWS), 0)
    tj = lax.broadcasted_iota(jnp.int32, (STEP_ROWS, STEP_ROWS), 1)
    tri = jnp.where((tj <= ti) & (tj // CHUNK == ti // CHUNK), 1.0, 0.0).astype(BF16)
    pairs = range(PAIRS)
    chunks = range(STEP_CHUNKS)
    units = [(c, g) for c in chunks for g in pairs]
    cols = lambda c0, g: slice(c0 + g * LANES, c0 + (g + 1) * LANES)
    rows = lambda c: slice(c * CHUNK, (c + 1) * CHUNK)

    def stack(x):
        x = x.astype(BF16)
        zero = jnp.zeros_like(x)
        return jnp.concatenate([jnp.where(head0, x, zero), jnp.where(head0, zero, x)], axis=0)

    def head_sums(xs):
        s = _mm_split_lhs(jnp.concatenate(xs, axis=0), head_ones2)
        return [s[i * CHUNK:(i + 1) * CHUNK] for i in range(len(xs))]

    def chain():
        ops = lambda i, c, g: ops_ref[use, i, rows(c), cols(0, g)]
        a_t, b_t, k_t, v, b_h, k_h = ([ops(i, c, g) for c, g in units] for i in range(OPERANDS))
        r_t = [rt_ref[use, rows(c), cols(0, g)] for c, g in units]
        n = range(len(units))
        v_s = [stack(x) for x in v]
        gram = [_mm_nt(jnp.concatenate([a_t[u], r_t[u]], axis=0),
                       jnp.concatenate([stack(b_t[u]), stack(k_t[u])], axis=0)) for u in n]
        yield
        a_ab = [jnp.where(strict, x[:CHUNK, :LANES], 0.0) for x in gram]
        a_ak = [jnp.where(strict, x[:CHUNK, LANES:], 0.0) for x in gram]
        a_rb = [jnp.where(incl, x[CHUNK:, :LANES], 0.0) for x in gram]
        a_rk = [jnp.where(incl, x[CHUNK:, LANES:], 0.0) for x in gram]
        nil = [jnp.where(same16, x, 0.0) for x in a_ab]
        inv = [eye + x for x in nil]
        nil = [_mm(x, stack(x)) for x in nil]
        yield
        av = [_mm(jnp.concatenate([a_ak[u], a_rk[u]], axis=0), v_s[u]) for u in n]
        yield
        for _ in range(2):
            both = [_mm(jnp.concatenate([nil[u], inv[u]], axis=0), stack(nil[u])) for u in n]
            nil = [x[:CHUNK] for x in both]
            inv = [inv[u] + both[u][CHUNK:] for u in n]
            yield
        vk = [_mm_tn(v[u], k_h[u]) for u in n]
        yield
        inv = [inv[u] + _mm(inv[u], stack(nil[u])) for u in n]
        yield
        for off_diag in (same32 & ~same16, ~same32):
            low = [_mm(inv[u], stack(jnp.where(off_diag, a_ab[u], 0.0))) for u in n]
            yield
            inv = [inv[u] + _mm(low[u], stack(inv[u])) for u in n]
            yield
        tx = [_mm(inv[u], jnp.concatenate([stack(a_t[u]), stack(av[u][:CHUNK])], axis=1))
              for u in n]
        yield
        zz = [_mm(a_rb[u], jnp.concatenate([stack(tx[u][:, :LANES]), stack(tx[u][:, LANES:])], axis=1))
              for u in n]
        yield
        r_p = [r_t[u] + zz[u][:, :LANES] for u in n]
        y0 = [zz[u][:, LANES:] + av[u][CHUNK:] for u in n]
        mn = [_mm_tn(tx[u], b_h[u]) for u in n]
        yield
        fresh = lax.rem(t - 1, steps_per_seq) == 0
        state = [jnp.where(fresh, 0.0, state_ref[g]) for g in pairs]
        y = []
        for u, (c, g) in enumerate(units):
            y.append(_mm_nt(r_p[u], state[g]) + y0[u])
            m_t = jnp.where(block_diag, mn[u][:LANES], 0.0)
            n_t = jnp.where(block_diag, mn[u][LANES:] + vk[u], 0.0)
            g_end = gend_ref[use, c * SUBLANES:c * SUBLANES + 1, cols(0, g)]
            state[g] = state[g] * g_end + _mm(state[g], m_t) + n_t
            if g == PAIRS - 1:
                yield
        for g in pairs:
            state_ref[g] = state[g]
        mean = head_sums(y)
        yc = [y[u] - mean[u] * (1.0 / HEAD_DIM) for u in n]
        var = head_sums([x * x for x in yc])
        for u, (c, g) in enumerate(units):
            yn = yc[u] * lax.rsqrt(var[u] * (1.0 / HEAD_DIM) + GN_EPS)
            yn = yn * lnw_ref[:, cols(0, g)] + lnb_ref[:, cols(0, g)]
            y_ref[rows(c), cols(0, g)] = ((yn + bonus_ref[use, rows(c), cols(0, g)])
                                          * gate_ref[rows(c), cols(0, g)].astype(F32)).astype(y_ref.dtype)

    def prepare():
        lo = p_ref[:, LORA_COL0:LORA_COL0 + LANES]
        z_all = -(w0_ref[...] + _mm_x3(jnp.tanh(lo[:, :LORA_RANK]), wu3_ref[...]))
        a_all = a0_ref[...] + _mm_x3(lo[:, LORA_RANK:], au3_ref[...])
        yield
        for g in pairs:
            c = cols(0, g)
            z = z_all[:, c]
            w = -(jnp.maximum(z, 0.0) + jnp.log(1.0 + jnp.exp(-jnp.abs(z)))) - 0.5
            lw = -jnp.exp(w)
            lg = _mm_split_rhs(tri, lw)
            ends = [lg[(i + 1) * CHUNK - 1:(i + 1) * CHUNK, :] for i in chunks]
            lg_end = jnp.concatenate([jnp.broadcast_to(e, (CHUNK, LANES)) for e in ends], axis=0)
            for i in chunks:
                gend_ref[fill, i * SUBLANES:(i + 1) * SUBLANES, c] = jnp.broadcast_to(
                    jnp.exp(ends[i]), (SUBLANES, LANES))
            e_inv = jnp.exp(-lg)
            e_end = jnp.exp(lg_end - lg)
            a = _sigmoid(a_all[:, c])
            r, k, v = p_ref[:, c], p_ref[:, cols(K_COL0, g)], p_ref[:, cols(V_COL0, g)]
            kk = k * kk_ref[:, c]
            k2 = k * (1.0 + (a - 1.0) * ka_ref[:, c])
            sums = _mm_split_lhs(jnp.concatenate([kk * kk, r * k2 * rk_ref[:, c]], axis=0), head_ones2)
            kk = kk / jnp.maximum(jnp.sqrt(sums[:STEP_ROWS]), 1e-12)
            bonus_ref[fill, :, c] = sums[STEP_ROWS:] * v
            rt_ref[fill, :, c] = r * jnp.exp(lg)
            ops_ref[fill, 0, :, c] = (-kk * jnp.exp(lg - lw)).astype(BF16)
            ops_ref[fill, 1, :, c] = (kk * a * e_inv).astype(BF16)
            ops_ref[fill, 2, :, c] = (k2 * e_inv).astype(BF16)
            ops_ref[fill, 3, :, c] = v.astype(BF16)
            ops_ref[fill, 4, :, c] = (kk * a * e_end).astype(BF16)
            ops_ref[fill, 5, :, c] = (k2 * e_end).astype(BF16)
            yield

    done = object()
    streams = [chain(), prepare()]
    while streams:
        streams = [s for s in streams if next(s, done) is not done]


def _rwkv_time_mix(p_shift, p_gates, batch, seq, w0, w_up, a0, a_up, k_k, k_a, r_k, ln_x_w, ln_x_b):
    nt = batch * seq // STEP_ROWS
    row2 = lambda v: v.reshape(1, -1).astype(F32)
    whole = lambda shape: pl.BlockSpec(shape, lambda t: (0, 0))
    chan = whole((1, RWKV_W))
    lora = whole((3 * LORA_RANK, RWKV_W))
    prepared = lambda t: (jnp.minimum(t, nt - 1), 0)
    chained = lambda t: (jnp.maximum(t - 1, 0), 0)
    return pl.pallas_call(
        functools.partial(_rwkv_kernel, steps_per_seq=seq // STEP_ROWS),
        grid=(nt + 1,),
        in_specs=[pl.BlockSpec((STEP_ROWS, SHIFT_COLS), prepared),
                  pl.BlockSpec((STEP_ROWS, RWKV_W), chained),
                  chan, chan, chan, chan, chan, chan, chan, lora, lora],
        out_specs=pl.BlockSpec((STEP_ROWS, RWKV_W), chained),
        out_shape=jax.ShapeDtypeStruct((batch * seq, RWKV_W), BF16),
        scratch_shapes=[pltpu.VMEM((PAIRS, LANES, LANES), F32),
                        pltpu.VMEM((2, OPERANDS, STEP_ROWS, RWKV_W), BF16),
                        pltpu.VMEM((2, STEP_ROWS, RWKV_W), F32),
                        pltpu.VMEM((2, STEP_CHUNKS * SUBLANES, RWKV_W), F32),
                        pltpu.VMEM((2, STEP_ROWS, RWKV_W), F32)],
        compiler_params=pltpu.CompilerParams(
            dimension_semantics=("arbitrary",), vmem_limit_bytes=VMEM_LIMIT),
        name="rwkv7_time_mix",
    )(p_shift, p_gates, row2(w0), row2(a0), row2(k_k), row2(k_a), row2(r_k), row2(ln_x_w), row2(ln_x_b),
      _stack_x3(w_up), _stack_x3(a_up))


def _attn_kernel(q_ref, k_ref, v_ref, gate_ref, y_ref, acc_scr, max_scr, sum_scr, *, seq):
    n_blocks = seq // ATT_BLK
    qi = lax.broadcasted_iota(jnp.int32, (2 * ATT_BLK, ATT_BLK), 0) % ATT_BLK
    kj = lax.broadcasted_iota(jnp.int32, (2 * ATT_BLK, ATT_BLK), 1)
    ok_prev, ok_cur = kj >= qi, kj <= qi
    head0 = lax.broadcasted_iota(jnp.int32, (ATT_BLK, LANES), 1) < HEAD_DIM
    ones = jnp.ones((ATT_BLK, LANES), BF16)
    scale = HEAD_DIM ** -0.5 * LOG2_E
    neg = -1e30

    def attend(pi, d, kv_starts, blocks):
        rows = lambda s: pl.ds(s, ATT_BLK, stride=d) if d > 1 else pl.ds(s, ATT_BLK)
        kb = [k_ref[rows(s), :].astype(BF16) for s in kv_starts]
        vb = [jnp.concatenate([v_ref[rows(s), :].astype(BF16), ones], axis=1) for s in kv_starts]
        qs = []
        for cur, _, _ in blocks:
            q = q_ref[rows(kv_starts[cur]), :] * scale
            qs.append(jnp.concatenate([jnp.where(head0, q, 0.0), jnp.where(head0, 0.0, q)],
                                      axis=0).astype(BF16))
        scores = []
        for (cur, prev, prev_ok), q in zip(blocks, qs):
            if prev is None:
                scores.append(jnp.where(ok_cur, _mm_nt(q, kb[cur]), neg))
            else:
                s = _mm_nt(q, jnp.concatenate([kb[prev], kb[cur]], axis=0))
                okp = ok_prev if prev_ok is None else ok_prev & prev_ok
                scores.append(jnp.concatenate([jnp.where(okp, s[:, :ATT_BLK], neg),
                                               jnp.where(ok_cur, s[:, ATT_BLK:], neg)], axis=1))
        mx = [jnp.max(s, axis=-1, keepdims=True) for s in scores]
        probs = [jnp.exp2(s - m).astype(BF16) for s, m in zip(scores, mx)]
        pv = [_mm(p, vb[cur] if prev is None else jnp.concatenate([vb[prev], vb[cur]], axis=0))
              for (cur, prev, _), p in zip(blocks, probs)]
        for (cur, _, _), m, x in zip(blocks, mx, pv):
            at = rows(kv_starts[cur])
            acc_scr[pi, at, :] = jnp.where(head0, x[:ATT_BLK, :LANES], x[ATT_BLK:, :LANES])
            sum_scr[pi, at, :] = jnp.where(head0, x[:ATT_BLK, LANES:], x[ATT_BLK:, LANES:])
            max_scr[pi, at, :] = jnp.where(head0, m[:ATT_BLK], m[ATT_BLK:])

    for pi, d in enumerate(DILATIONS):
        nb = n_blocks // d
        span = ATT_BLK * d
        if nb <= ATT_GROUP:
            n_res = ATT_GROUP // nb
            def body(it, carry, pi=pi, d=d, span=span, nb=nb, n_res=n_res):
                starts = [it * n_res + r + u * span for r in range(n_res) for u in range(nb)]
                blocks = [(r * nb + u, r * nb + u - 1 if u else None, None)
                          for r in range(n_res) for u in range(nb)]
                attend(pi, d, starts, blocks)
                return carry
            trips = d // n_res
        else:
            per_res = nb // ATT_GROUP
            def body(it, carry, pi=pi, d=d, span=span, per_res=per_res):
                res, blk0 = it // per_res, (it % per_res) * ATT_GROUP
                starts = [res + jnp.maximum(blk0 - 1, 0) * span]
                starts += [res + (blk0 + u) * span for u in range(ATT_GROUP)]
                attend(pi, d, starts,
                       [(1, 0, blk0 > 0)] + [(u + 1, u, None) for u in range(1, ATT_GROUP)])
                return carry
            trips = d * per_res
        lax.fori_loop(0, trips, body, 0)

    m = jnp.maximum(jnp.maximum(max_scr[0], max_scr[1]), max_scr[2])
    num = den = None
    for pi in range(len(DILATIONS)):
        w = jnp.exp2(max_scr[pi] - m)
        num = w * acc_scr[pi] if num is None else num + w * acc_scr[pi]
        den = w * sum_scr[pi] if den is None else den + w * sum_scr[pi]
    y_ref[...] = (num / den * gate_ref[...].astype(F32)).astype(y_ref.dtype)


def _dilated_attention(p_att, p_gates, batch, seq):
    def tile(blk0):
        return pl.BlockSpec((seq, LANES), lambda b, g: (b, blk0 + g))

    return pl.pallas_call(
        functools.partial(_attn_kernel, seq=seq),
        grid=(batch, PAIRS),
        in_specs=[tile(0), tile(PAIRS), tile(2 * PAIRS), tile(PAIRS)],
        out_specs=pl.BlockSpec((seq, LANES), lambda b, g: (b, g)),
        out_shape=jax.ShapeDtypeStruct((batch * seq, ATT_W), BF16),
        scratch_shapes=[pltpu.VMEM((len(DILATIONS), seq, LANES), F32)] * 3,
        compiler_params=pltpu.CompilerParams(
            dimension_semantics=("arbitrary", "arbitrary"), vmem_limit_bytes=VMEM_LIMIT),
        name="dilated_attention",
    )(p_att, p_att, p_att, p_gates)


def _out_kernel(yr_ref, ya_ref, x_ref, wr_ref, wa_ref, fw_ref, o_ref):
    h = (x_ref[...]
         + jnp.dot(yr_ref[...], wr_ref[...], preferred_element_type=F32)
         + jnp.dot(ya_ref[...], wa_ref[...], preferred_element_type=F32))
    ms = jnp.mean(h * h, axis=-1, keepdims=True)
    o_ref[...] = h * lax.rsqrt(ms + RMS_EPS) * fw_ref[...]


def _out_projection(y_rwkv, y_att, x2, w_out_bf16, final_norm_w, tm):
    tokens = x2.shape[0]
    tok = lambda width: pl.BlockSpec((tm, width), lambda i: (i, 0))
    return pl.pallas_call(
        _out_kernel,
        grid=(tokens // tm,),
        in_specs=[
            tok(RWKV_W), tok(ATT_W), tok(D_MODEL),
            pl.BlockSpec((RWKV_W, D_MODEL), lambda i: (0, 0)),
            pl.BlockSpec((ATT_W, D_MODEL), lambda i: (1, 0)),
            pl.BlockSpec((1, D_MODEL), lambda i: (0, 0)),
        ],
        out_specs=tok(D_MODEL),
        out_shape=jax.ShapeDtypeStruct((tokens, D_MODEL), F32),
        compiler_params=pltpu.CompilerParams(
            dimension_semantics=("arbitrary",), vmem_limit_bytes=VMEM_LIMIT),
        name="out_projection",
    )(y_rwkv, y_att, x2, w_out_bf16, w_out_bf16, final_norm_w.reshape(1, D_MODEL))


def kernel(x, norm_w, w_in, mu_shift, w0, w_up, a0, a_up, k_k, k_a, r_k, ln_x_w, ln_x_b, w_out,
           final_norm_w):
    batch, seq, _ = x.shape
    assert seq % (ATT_BLK * max(DILATIONS)) == 0 and seq % STEP_ROWS == 0
    x2 = x.astype(F32).reshape(batch * seq, D_MODEL)
    nw = norm_w.astype(F32)
    w_bf16 = w_in.astype(BF16)
    w_gates = jnp.concatenate([w_bf16[:, RWKV_GATE0:ATT0], w_bf16[:, ATT_GATE0:]], axis=1)
    p_shift = _in_projection(x2, nw, w_bf16[:, :SHIFT_COLS], seq, "shift", mu_shift)
    p_gates = _in_projection(x2, nw, w_gates, seq, "silu", out_dtype=BF16)
    p_att = _in_projection(x2, nw, w_bf16[:, ATT0:ATT_GATE0], seq, "plain")
    y_rwkv = _rwkv_time_mix(p_shift, p_gates, batch, seq, w0, w_up, a0, a_up, k_k, k_a, r_k,
                            ln_x_w, ln_x_b)
    y_att = _dilated_attention(p_att, p_gates, batch, seq)
    out = _out_projection(y_rwkv, y_att, x2, w_out.astype(BF16), final_norm_w.astype(F32),
                          tm=min(512, batch * seq))
    return out.reshape(batch, seq, D_MODEL).astype(x.dtype)
```

```python
import functools

import jax
import jax.numpy as jnp
from jax import lax
from jax.experimental import pallas as pl
from jax.experimental.pallas import tpu as pltpu

D_MODEL = 1024
HEAD_DIM = 64
RWKV_W = 1024
ATT_W = 1024
MIX_W = RWKV_W + ATT_W
LORA_RANK = 64
SHIFT_COLS = 3 * RWKV_W + 2 * LORA_RANK
RWKV_GATE0 = SHIFT_COLS
ATT0 = RWKV_GATE0 + RWKV_W
ATT_GATE0 = ATT0 + 3 * ATT_W
DILATIONS = (1, 4, 16)
ATT_BLK = 128
RESIDUES = max(DILATIONS)
ATT_GROUP = 8
LOG2_E = 1.4426950408889634
RMS_EPS = 1e-5
GN_EPS = 64e-5

LANES = 128
SUBLANES = 8
PAIRS = RWKV_W // LANES
CHUNK = 64
K_COL0, V_COL0, LORA_COL0 = RWKV_W, 2 * RWKV_W, 3 * RWKV_W
PROJ_TM = 512

F32 = jnp.float32
BF16 = jnp.bfloat16
VMEM_LIMIT = 56 * 1024 * 1024

NT_DIMS = (((1,), (1,)), ((), ()))
TN_DIMS = (((0,), (0,)), ((), ()))


def _mm(a, b):
    return jnp.dot(a.astype(BF16), b.astype(BF16), preferred_element_type=F32)


def _mm_nt(a, b):
    return lax.dot_general(a.astype(BF16), b.astype(BF16), NT_DIMS, preferred_element_type=F32)


def _mm_tn(a, b):
    return lax.dot_general(a.astype(BF16), b.astype(BF16), TN_DIMS, preferred_element_type=F32)


def _split(x):
    hi = x.astype(BF16)
    return hi, (x - hi.astype(F32)).astype(BF16)


def _mm_split_lhs(x, w2_bf16):
    return jnp.dot(jnp.concatenate(_split(x), axis=1), w2_bf16, preferred_element_type=F32)


def _mm_split_rhs(w_bf16, x):
    n = x.shape[1]
    both = jnp.dot(w_bf16, jnp.concatenate(_split(x), axis=1), preferred_element_type=F32)
    return both[:, :n] + both[:, n:]


def _mm_x3(x, w3):
    hi, lo = _split(x)
    return jnp.dot(jnp.concatenate([hi, lo, hi], axis=1), w3, preferred_element_type=F32)


def _stack_x3(w):
    hi, lo = _split(w.astype(F32))
    return jnp.concatenate([hi, hi, lo], axis=0)


def _sigmoid(x):
    return 1.0 / (1.0 + jnp.exp(-x))


def _proj_kernel(x_ref, nw_ref, w_ref, *rest, epilogue, tiles_per_seq):
    x = x_ref[...].reshape(-1, D_MODEL)
    ms = jnp.mean(x * x, axis=-1, keepdims=True)
    xn = (x * lax.rsqrt(ms + RMS_EPS) * nw_ref[...]).astype(BF16)
    o = jnp.dot(xn, w_ref[...], preferred_element_type=F32)
    if epilogue == "shift":
        mu_ref, o_ref, prev_ref = rest
        tm = o.shape[0]
        first = pl.program_id(0) % tiles_per_seq == 0
        last = jnp.where(first, 0.0, prev_ref[SUBLANES - 1:SUBLANES, :])
        row = lax.broadcasted_iota(jnp.int32, (tm, 1), 0)
        oprev = jnp.where(row == 0, last, pltpu.roll(o, 1, 0))
        prev_ref[...] = o[tm - SUBLANES:, :]
        o_ref[...] = o + (oprev - o) * mu_ref[...]
    elif epilogue == "silu":
        (o_ref,) = rest
        o_ref[...] = (o * _sigmoid(o)).astype(o_ref.dtype)
    elif epilogue == "qkv_gate":
        qkv_ref, gate_ref = rest
        qkv_ref[...] = o[:, :3 * ATT_W].reshape(qkv_ref.shape)
        gate = o[:, 3 * ATT_W:]
        gate_ref[...] = (gate * _sigmoid(gate)).astype(gate_ref.dtype).reshape(gate_ref.shape)
    else:
        (o_ref,) = rest
        o_ref[...] = o


def _in_projection(x2, norm_w, w_bf16, seq, epilogue, mu=None, out_dtype=F32):
    tokens, n = x2.shape[0], w_bf16.shape[1]
    tm = min(PROJ_TM, seq)
    in_specs = [pl.BlockSpec((tm, D_MODEL), lambda i: (i, 0)),
                pl.BlockSpec((1, D_MODEL), lambda i: (0, 0)),
                pl.BlockSpec((D_MODEL, n), lambda i: (0, 0))]
    args = [x2, norm_w.reshape(1, D_MODEL), w_bf16]
    scratch = []
    if epilogue == "shift":
        in_specs.append(pl.BlockSpec((1, n), lambda i: (0, 0)))
        args.append(mu.reshape(1, n).astype(F32))
        scratch.append(pltpu.VMEM((SUBLANES, n), F32))
    return pl.pallas_call(
        functools.partial(_proj_kernel, epilogue=epilogue, tiles_per_seq=seq // tm),
        grid=(tokens // tm,),
        in_specs=in_specs,
        out_specs=pl.BlockSpec((tm, n), lambda i: (i, 0)),
        out_shape=jax.ShapeDtypeStruct((tokens, n), out_dtype),
        scratch_shapes=scratch,
        compiler_params=pltpu.CompilerParams(
            dimension_semantics=("arbitrary",), vmem_limit_bytes=VMEM_LIMIT),
        name="in_projection_" + epilogue,
    )(*args)


def _rows_per_tile(batch, per_res):
    n = max(1, min(batch, PROJ_TM // per_res))
    while batch % n:
        n -= 1
    return n


def _att_projection(x2, norm_w, w_bf16, batch, seq):
    per_res = seq // RESIDUES
    nb = _rows_per_tile(batch, per_res)
    out_block = lambda width: pl.BlockSpec((nb, None, per_res, width), lambda i, r: (i, r, 0, 0))
    return pl.pallas_call(
        functools.partial(_proj_kernel, epilogue="qkv_gate", tiles_per_seq=1),
        grid=(batch // nb, RESIDUES),
        in_specs=[pl.BlockSpec((nb, per_res, D_MODEL), lambda i, r: (i, 0, r)),
                  pl.BlockSpec((1, D_MODEL), lambda i, r: (0, 0)),
                  pl.BlockSpec(w_bf16.shape, lambda i, r: (0, 0))],
        out_specs=[out_block(3 * ATT_W), out_block(ATT_W)],
        out_shape=[jax.ShapeDtypeStruct((batch, RESIDUES, per_res, 3 * ATT_W), F32),
                   jax.ShapeDtypeStruct((batch, RESIDUES, per_res, ATT_W), BF16)],
        compiler_params=pltpu.CompilerParams(
            dimension_semantics=("arbitrary", "arbitrary"), vmem_limit_bytes=VMEM_LIMIT),
        name="in_projection_attention",
    )(x2.reshape(batch, per_res, RESIDUES * D_MODEL), norm_w.reshape(1, D_MODEL), w_bf16)


OPERANDS = 6
STEP_CHUNKS = 2
STEP_ROWS = STEP_CHUNKS * CHUNK


def _rwkv_kernel(p_ref, gate_ref, w0_ref, a0_ref, kk_ref, ka_ref, rk_ref, lnw_ref, lnb_ref,
                 wu3_ref, au3_ref, y_ref, state_ref, ops_ref, rt_ref, gend_ref, bonus_ref,
                 *, steps_per_seq):
    t = pl.program_id(0)

    @pl.when(t == 0)
    def _():
        ops_ref[...] = jnp.zeros_like(ops_ref)
        rt_ref[...] = jnp.zeros_like(rt_ref)
        gend_ref[...] = jnp.zeros_like(gend_ref)
        bonus_ref[...] = jnp.zeros_like(bonus_ref)
        state_ref[...] = jnp.zeros_like(state_ref)

    fill, use = t % 2, (t + 1) % 2
    row = lax.broadcasted_iota(jnp.int32, (CHUNK, LANES), 0)
    lane = lax.broadcasted_iota(jnp.int32, (CHUNK, LANES), 1)
    head0 = lane < HEAD_DIM
    tcol = lane % CHUNK
    strict, incl = tcol < row, tcol <= row
    same16 = (tcol // 16) == (row // 16)
    same32 = (tcol // 32) == (row // 32)
    eye = jnp.where(tcol == row, 1.0, 0.0).astype(F32)
    bi = lax.broadcasted_iota(jnp.int32, (LANES, LANES), 0)
    bj = lax.broadcasted_iota(jnp.int32, (LANES, LANES), 1)
    block_diag = (bi // HEAD_DIM) == (bj // HEAD_DIM)
    head_ones = jnp.where(block_diag, 1.0, 0.0).astype(BF16)
    head_ones2 = jnp.concatenate([head_ones, head_ones], axis=0)
    ti = lax.broadcasted_iota(jnp.int32, (STEP_ROWS, STEP_ROWS), 0)
    tj = lax.broadcasted_iota(jnp.int32, (STEP_ROWS, STEP_ROWS), 1)
    tri = jnp.where((tj <= ti) & (tj // CHUNK == ti // CHUNK), 1.0, 0.0).astype(BF16)
    pairs = range(PAIRS)
    chunks = range(STEP_CHUNKS)
    units = [(c, g) for c in chunks for g in pairs]
    cols = lambda c0, g: slice(c0 + g * LANES, c0 + (g + 1) * LANES)
    rows = lambda c: slice(c * CHUNK, (c + 1) * CHUNK)

    def stack(x):
        x = x.astype(BF16)
        zero = jnp.zeros_like(x)
        return jnp.concatenate([jnp.where(head0, x, zero), jnp.where(head0, zero, x)], axis=0)

    def head_sums(xs):
        s = _mm_split_lhs(jnp.concatenate(xs, axis=0), head_ones2)
        return [s[i * CHUNK:(i + 1) * CHUNK] for i in range(len(xs))]

    def chain():
        ops = lambda i, c, g: ops_ref[use, i, rows(c), cols(0, g)]
        a_t, b_t, k_t, v, b_h, k_h = ([ops(i, c, g) for c, g in units] for i in range(OPERANDS))
        r_t = [rt_ref[use, rows(c), cols(0, g)] for c, g in units]
        n = range(len(units))
        v_s = [stack(x) for x in v]
        gram = [_mm_nt(jnp.concatenate([a_t[u], r_t[u]], axis=0),
                       jnp.concatenate([stack(b_t[u]), stack(k_t[u])], axis=0)) for u in n]
        yield
        a_ab = [jnp.where(strict, x[:CHUNK, :LANES], 0.0) for x in gram]
        a_ak = [jnp.where(strict, x[:CHUNK, LANES:], 0.0) for x in gram]
        a_rb = [jnp.where(incl, x[CHUNK:, :LANES], 0.0) for x in gram]
        a_rk = [jnp.where(incl, x[CHUNK:, LANES:], 0.0) for x in gram]
        nil = [jnp.where(same16, x, 0.0) for x in a_ab]
        inv = [eye + x for x in nil]
        nil = [_mm(x, stack(x)) for x in nil]
        yield
        av = [_mm(jnp.concatenate([a_ak[u], a_rk[u]], axis=0), v_s[u]) for u in n]
        yield
        for _ in range(2):
            both = [_mm(jnp.concatenate([nil[u], inv[u]], axis=0), stack(nil[u])) for u in n]
            nil = [x[:CHUNK] for x in both]
            inv = [inv[u] + both[u][CHUNK:] for u in n]
            yield
        vk = [_mm_tn(v[u], k_h[u]) for u in n]
        yield
        inv = [inv[u] + _mm(inv[u], stack(nil[u])) for u in n]
        yield
        for off_diag in (same32 & ~same16, ~same32):
            low = [_mm(inv[u], stack(jnp.where(off_diag, a_ab[u], 0.0))) for u in n]
            yield
            inv = [inv[u] + _mm(low[u], stack(inv[u])) for u in n]
            yield
        tx = [_mm(inv[u], jnp.concatenate([stack(a_t[u]), stack(av[u][:CHUNK])], axis=1))
              for u in n]
        yield
        zz = [_mm(a_rb[u], jnp.concatenate([stack(tx[u][:, :LANES]), stack(tx[u][:, LANES:])], axis=1))
              for u in n]
        yield
        r_p = [r_t[u] + zz[u][:, :LANES] for u in n]
        y0 = [zz[u][:, LANES:] + av[u][CHUNK:] for u in n]
        mn = [_mm_tn(tx[u], b_h[u]) for u in n]
        yield
        fresh = lax.rem(t - 1, steps_per_seq) == 0
        state = [jnp.where(fresh, 0.0, state_ref[g]) for g in pairs]
        y = []
        for u, (c, g) in enumerate(units):
            y.append(_mm_nt(r_p[u], state[g]) + y0[u])
            m_t = jnp.where(block_diag, mn[u][:LANES], 0.0)
            n_t = jnp.where(block_diag, mn[u][LANES:] + vk[u], 0.0)
            g_end = gend_ref[use, c * SUBLANES:c * SUBLANES + 1, cols(0, g)]
            state[g] = state[g] * g_end + _mm(state[g], m_t) + n_t
            if g == PAIRS - 1:
                yield
        for g in pairs:
            state_ref[g] = state[g]
        mean = head_sums(y)
        yc = [y[u] - mean[u] * (1.0 / HEAD_DIM) for u in n]
        var = head_sums([x * x for x in yc])
        for u, (c, g) in enumerate(units):
            yn = yc[u] * lax.rsqrt(var[u] * (1.0 / HEAD_DIM) + GN_EPS)
            yn = yn * lnw_ref[:, cols(0, g)] + lnb_ref[:, cols(0, g)]
            y_ref[rows(c), cols(0, g)] = ((yn + bonus_ref[use, rows(c), cols(0, g)])
                                          * gate_ref[rows(c), cols(0, g)].astype(F32)).astype(y_ref.dtype)

    def prepare():
        lo = p_ref[:, LORA_COL0:LORA_COL0 + LANES]
        z_all = -(w0_ref[...] + _mm_x3(jnp.tanh(lo[:, :LORA_RANK]), wu3_ref[...]))
        a_all = a0_ref[...] + _mm_x3(lo[:, LORA_RANK:], au3_ref[...])
        yield
        for g in pairs:
            c = cols(0, g)
            z = z_all[:, c]
            w = -(jnp.maximum(z, 0.0) + jnp.log(1.0 + jnp.exp(-jnp.abs(z)))) - 0.5
            lw = -jnp.exp(w)
            lg = _mm_split_rhs(tri, lw)
            ends = [lg[(i + 1) * CHUNK - 1:(i + 1) * CHUNK, :] for i in chunks]
            lg_end = jnp.concatenate([jnp.broadcast_to(e, (CHUNK, LANES)) for e in ends], axis=0)
            for i in chunks:
                gend_ref[fill, i * SUBLANES:(i + 1) * SUBLANES, c] = jnp.broadcast_to(
                    jnp.exp(ends[i]), (SUBLANES, LANES))
            e_inv = jnp.exp(-lg)
            e_end = jnp.exp(lg_end - lg)
            a = _sigmoid(a_all[:, c])
            r, k, v = p_ref[:, c], p_ref[:, cols(K_COL0, g)], p_ref[:, cols(V_COL0, g)]
            kk = k * kk_ref[:, c]
            k2 = k * (1.0 + (a - 1.0) * ka_ref[:, c])
            sums = _mm_split_lhs(jnp.concatenate([kk * kk, r * k2 * rk_ref[:, c]], axis=0), head_ones2)
            kk = kk / jnp.maximum(jnp.sqrt(sums[:STEP_ROWS]), 1e-12)
            bonus_ref[fill, :, c] = sums[STEP_ROWS:] * v
            rt_ref[fill, :, c] = r * jnp.exp(lg)
            ops_ref[fill, 0, :, c] = (-kk * jnp.exp(lg - lw)).astype(BF16)
            ops_ref[fill, 1, :, c] = (kk * a * e_inv).astype(BF16)
            ops_ref[fill, 2, :, c] = (k2 * e_inv).astype(BF16)
            ops_ref[fill, 3, :, c] = v.astype(BF16)
            ops_ref[fill, 4, :, c] = (kk * a * e_end).astype(BF16)
            ops_ref[fill, 5, :, c] = (k2 * e_end).astype(BF16)
            yield

    done = object()
    streams = [chain(), prepare()]
    while streams:
        streams = [s for s in streams if next(s, done) is not done]


def _rwkv_time_mix(p_shift, p_gates, batch, seq, w0, w_up, a0, a_up, k_k, k_a, r_k, ln_x_w, ln_x_b):
    nt = batch * seq // STEP_ROWS
    row2 = lambda v: v.reshape(1, -1).astype(F32)
    whole = lambda shape: pl.BlockSpec(shape, lambda t: (0, 0))
    chan = whole((1, RWKV_W))
    lora = whole((3 * LORA_RANK, RWKV_W))
    prepared = lambda t: (jnp.minimum(t, nt - 1), 0)
    chained = lambda t: (jnp.maximum(t - 1, 0), 0)
    return pl.pallas_call(
        functools.partial(_rwkv_kernel, steps_per_seq=seq // STEP_ROWS),
        grid=(nt + 1,),
        in_specs=[pl.BlockSpec((STEP_ROWS, SHIFT_COLS), prepared),
                  pl.BlockSpec((STEP_ROWS, RWKV_W), chained),
                  chan, chan, chan, chan, chan, chan, chan, lora, lora],
        out_specs=pl.BlockSpec((STEP_ROWS, RWKV_W), chained),
        out_shape=jax.ShapeDtypeStruct((batch * seq, RWKV_W), BF16),
        scratch_shapes=[pltpu.VMEM((PAIRS, LANES, LANES), F32),
                        pltpu.VMEM((2, OPERANDS, STEP_ROWS, RWKV_W), BF16),
                        pltpu.VMEM((2, STEP_ROWS, RWKV_W), F32),
                        pltpu.VMEM((2, STEP_CHUNKS * SUBLANES, RWKV_W), F32),
                        pltpu.VMEM((2, STEP_ROWS, RWKV_W), F32)],
        compiler_params=pltpu.CompilerParams(
            dimension_semantics=("arbitrary",), vmem_limit_bytes=VMEM_LIMIT),
        name="rwkv7_time_mix",
    )(p_shift, p_gates, row2(w0), row2(a0), row2(k_k), row2(k_a), row2(r_k), row2(ln_x_w), row2(ln_x_b),
      _stack_x3(w_up), _stack_x3(a_up))


def _attn_kernel(q_ref, k_ref, v_ref, gate_ref, y_ref, acc_scr, max_scr, sum_scr, *, seq):
    n_blocks = seq // ATT_BLK
    qi = lax.broadcasted_iota(jnp.int32, (2 * ATT_BLK, ATT_BLK), 0) % ATT_BLK
    kj = lax.broadcasted_iota(jnp.int32, (2 * ATT_BLK, ATT_BLK), 1)
    head0 = lax.broadcasted_iota(jnp.int32, (ATT_BLK, LANES), 1) < HEAD_DIM
    ones = jnp.ones((ATT_BLK, LANES), BF16)
    scale = HEAD_DIM ** -0.5 * LOG2_E
    neg = -1e30

    def attend(pi, d, kv_blocks, blocks):
        pieces = RESIDUES // d
        length = ATT_BLK // pieces
        pos = lambda i: pieces * (i % length) + i // length
        ok_prev, ok_cur = pos(kj) >= pos(qi), pos(kj) <= pos(qi)

        def at(res, blk, j):
            return j * d + res, pl.ds(pl.multiple_of(blk * length, length), length)

        def load(ref, res, blk):
            parts = []
            for j in range(pieces):
                r, rows = at(res, blk, j)
                parts.append(ref[r, rows, :])
            return jnp.concatenate(parts, axis=0) if pieces > 1 else parts[0]

        def store(scr, res, blk, val):
            for j in range(pieces):
                r, rows = at(res, blk, j)
                scr[pi, r, rows, :] = val[j * length:(j + 1) * length]

        kb = [load(k_ref, *b).astype(BF16) for b in kv_blocks]
        vb = [jnp.concatenate([load(v_ref, *b).astype(BF16), ones], axis=1) for b in kv_blocks]
        qs = []
        for cur, _, _ in blocks:
            q = load(q_ref, *kv_blocks[cur]) * scale
            qs.append(jnp.concatenate([jnp.where(head0, q, 0.0), jnp.where(head0, 0.0, q)],
                                      axis=0).astype(BF16))
        scores = []
        for (cur, prev, prev_ok), q in zip(blocks, qs):
            if prev is None:
                scores.append(jnp.where(ok_cur, _mm_nt(q, kb[cur]), neg))
            else:
                s = _mm_nt(q, jnp.concatenate([kb[prev], kb[cur]], axis=0))
                okp = ok_prev if prev_ok is None else ok_prev & prev_ok
                scores.append(jnp.concatenate([jnp.where(okp, s[:, :ATT_BLK], neg),
                                               jnp.where(ok_cur, s[:, ATT_BLK:], neg)], axis=1))
        mx = [jnp.max(s, axis=-1, keepdims=True) for s in scores]
        probs = [jnp.exp2(s - m).astype(BF16) for s, m in zip(scores, mx)]
        pv = [_mm(p, vb[cur] if prev is None else jnp.concatenate([vb[prev], vb[cur]], axis=0))
              for (cur, prev, _), p in zip(blocks, probs)]
        for (cur, _, _), m, x in zip(blocks, mx, pv):
            res, blk = kv_blocks[cur]
            store(acc_scr, res, blk, jnp.where(head0, x[:ATT_BLK, :LANES], x[ATT_BLK:, :LANES]))
            store(sum_scr, res, blk, jnp.where(head0, x[:ATT_BLK, LANES:], x[ATT_BLK:, LANES:]))
            store(max_scr, res, blk, jnp.where(head0, m[:ATT_BLK], m[ATT_BLK:]))

    for pi, d in enumerate(DILATIONS):
        nb = n_blocks // d
        if nb <= ATT_GROUP:
            n_res = ATT_GROUP // nb
            def body(it, carry, pi=pi, d=d, nb=nb, n_res=n_res):
                kv = [(it * n_res + r, u) for r in range(n_res) for u in range(nb)]
                blocks = [(r * nb + u, r * nb + u - 1 if u else None, None)
                          for r in range(n_res) for u in range(nb)]
                attend(pi, d, kv, blocks)
                return carry
            trips = d // n_res
        else:
            per_res = nb // ATT_GROUP
            def body(it, carry, pi=pi, d=d, per_res=per_res):
                res, blk0 = it // per_res, (it % per_res) * ATT_GROUP
                kv = [(res, jnp.maximum(blk0 - 1, 0))] + [(res, blk0 + u) for u in range(ATT_GROUP)]
                attend(pi, d, kv, [(1, 0, blk0 > 0)] + [(u + 1, u, None) for u in range(1, ATT_GROUP)])
                return carry
            trips = d * per_res
        lax.fori_loop(0, trips, body, 0)

    m = jnp.maximum(jnp.maximum(max_scr[0], max_scr[1]), max_scr[2])
    num = den = None
    for pi in range(len(DILATIONS)):
        w = jnp.exp2(max_scr[pi] - m)
        num = w * acc_scr[pi] if num is None else num + w * acc_scr[pi]
        den = w * sum_scr[pi] if den is None else den + w * sum_scr[pi]
    y_ref[...] = (num / den * gate_ref[...].astype(F32)).astype(y_ref.dtype)


def _dilated_attention(p_att, p_gate, batch, seq):
    per_res = seq // RESIDUES

    def tile(blk0):
        return pl.BlockSpec((RESIDUES, per_res, LANES), lambda b, g: (b, 0, blk0 + g))

    return pl.pallas_call(
        functools.partial(_attn_kernel, seq=seq),
        grid=(batch, PAIRS),
        in_specs=[tile(0), tile(PAIRS), tile(2 * PAIRS), tile(0)],
        out_specs=pl.BlockSpec((RESIDUES, per_res, LANES), lambda b, g: (b, 0, g)),
        out_shape=jax.ShapeDtypeStruct((batch * RESIDUES, per_res, ATT_W), BF16),
        scratch_shapes=[pltpu.VMEM((len(DILATIONS), RESIDUES, per_res, LANES), F32)] * 3,
        compiler_params=pltpu.CompilerParams(
            dimension_semantics=("arbitrary", "arbitrary"), vmem_limit_bytes=VMEM_LIMIT),
        name="dilated_attention",
    )(p_att, p_att, p_att, p_gate)


def _out_kernel(yr_ref, ya_ref, x_ref, wr_ref, wa_ref, fw_ref, o_ref):
    h = (x_ref[...].reshape(-1, D_MODEL)
         + jnp.dot(yr_ref[...].reshape(-1, RWKV_W), wr_ref[...], preferred_element_type=F32)
         + jnp.dot(ya_ref[...].reshape(-1, ATT_W), wa_ref[...], preferred_element_type=F32))
    ms = jnp.mean(h * h, axis=-1, keepdims=True)
    o_ref[...] = (h * lax.rsqrt(ms + RMS_EPS) * fw_ref[...]).reshape(o_ref.shape)


def _out_projection(y_rwkv, y_att, x2, w_out_bf16, final_norm_w, batch, seq):
    per_res = seq // RESIDUES
    nb = _rows_per_tile(batch, per_res)
    by_col = lambda width: pl.BlockSpec((nb, per_res, width), lambda i, r: (i, 0, r))
    out = pl.pallas_call(
        _out_kernel,
        grid=(batch // nb, RESIDUES),
        in_specs=[
            by_col(RWKV_W),
            pl.BlockSpec((nb, None, per_res, ATT_W), lambda i, r: (i, r, 0, 0)),
            by_col(D_MODEL),
            pl.BlockSpec((RWKV_W, D_MODEL), lambda i, r: (0, 0)),
            pl.BlockSpec((ATT_W, D_MODEL), lambda i, r: (1, 0)),
            pl.BlockSpec((1, D_MODEL), lambda i, r: (0, 0)),
        ],
        out_specs=by_col(D_MODEL),
        out_shape=jax.ShapeDtypeStruct((batch, per_res, RESIDUES * D_MODEL), F32),
        compiler_params=pltpu.CompilerParams(
            dimension_semantics=("arbitrary", "arbitrary"), vmem_limit_bytes=VMEM_LIMIT),
        name="out_projection",
    )(y_rwkv.reshape(batch, per_res, RESIDUES * RWKV_W), y_att.reshape(batch, RESIDUES, per_res, ATT_W),
      x2.reshape(batch, per_res, RESIDUES * D_MODEL), w_out_bf16, w_out_bf16,
      final_norm_w.reshape(1, D_MODEL))
    return out.reshape(batch, seq, D_MODEL)


def kernel(x, norm_w, w_in, mu_shift, w0, w_up, a0, a_up, k_k, k_a, r_k, ln_x_w, ln_x_b, w_out,
           final_norm_w):
    batch, seq, _ = x.shape
    assert seq % (ATT_BLK * max(DILATIONS)) == 0 and seq % STEP_ROWS == 0
    x2 = x.astype(F32).reshape(batch * seq, D_MODEL)
    nw = norm_w.astype(F32)
    w_bf16 = w_in.astype(BF16)
    w_att = jnp.concatenate([w_bf16[:, ATT0:ATT_GATE0], w_bf16[:, ATT_GATE0:]], axis=1)
    p_shift = _in_projection(x2, nw, w_bf16[:, :SHIFT_COLS], seq, "shift", mu_shift)
    gate_rwkv = _in_projection(x2, nw, w_bf16[:, RWKV_GATE0:ATT0], seq, "silu", out_dtype=BF16)
    p_att, gate_att = _att_projection(x2, nw, w_att, batch, seq)
    y_rwkv = _rwkv_time_mix(p_shift, gate_rwkv, batch, seq, w0, w_up, a0, a_up, k_k, k_a, r_k,
                            ln_x_w, ln_x_b)
    per_res = seq // RESIDUES
    y_att = _dilated_attention(p_att.reshape(batch * RESIDUES, per_res, 3 * ATT_W),
                               gate_att.reshape(batch * RESIDUES, per_res, ATT_W), batch, seq)
    out = _out_projection(y_rwkv, y_att, x2, w_out.astype(BF16), final_norm_w.astype(F32), batch, seq)
    return out.astype(x.dtype)
```

```python
import functools

import jax
import jax.numpy as jnp
from jax import lax
from jax.experimental import pallas as pl
from jax.experimental.pallas import tpu as pltpu

D_MODEL = 1024
HEAD_DIM = 64
RWKV_W = 1024
ATT_W = 1024
MIX_W = RWKV_W + ATT_W
LORA_RANK = 64
SHIFT_COLS = 3 * RWKV_W + 2 * LORA_RANK
RWKV_GATE0 = SHIFT_COLS
ATT0 = RWKV_GATE0 + RWKV_W
ATT_GATE0 = ATT0 + 3 * ATT_W
DILATIONS = (1, 4, 16)
ATT_BLK = 128
RESIDUES = max(DILATIONS)
ATT_GROUP = 8
LOG2_E = 1.4426950408889634
RMS_EPS = 1e-5
GN_EPS = 64e-5

LANES = 128
SUBLANES = 8
PAIRS = RWKV_W // LANES
CHUNK = 64
K_COL0, V_COL0, LORA_COL0 = RWKV_W, 2 * RWKV_W, 3 * RWKV_W
PROJ_TM = 512

F32 = jnp.float32
BF16 = jnp.bfloat16
VMEM_LIMIT = 56 * 1024 * 1024

NT_DIMS = (((1,), (1,)), ((), ()))
TN_DIMS = (((0,), (0,)), ((), ()))


def _mm(a, b):
    return jnp.dot(a.astype(BF16), b.astype(BF16), preferred_element_type=F32)


def _mm_nt(a, b):
    return lax.dot_general(a.astype(BF16), b.astype(BF16), NT_DIMS, preferred_element_type=F32)


def _mm_tn(a, b):
    return lax.dot_general(a.astype(BF16), b.astype(BF16), TN_DIMS, preferred_element_type=F32)


def _split(x):
    hi = x.astype(BF16)
    return hi, (x - hi.astype(F32)).astype(BF16)


def _mm_split_rhs(w_bf16, x):
    n = x.shape[1]
    both = jnp.dot(w_bf16, jnp.concatenate(_split(x), axis=1), preferred_element_type=F32)
    return both[:, :n] + both[:, n:]


def _mm_x3(x, w3):
    hi, lo = _split(x)
    return jnp.dot(jnp.concatenate([hi, lo, hi], axis=1), w3, preferred_element_type=F32)


def _stack_x3(w):
    hi, lo = _split(w.astype(F32))
    return jnp.concatenate([hi, hi, lo], axis=0)


def _sigmoid(x):
    return 1.0 / (1.0 + jnp.exp(-x))


def _regrouped_rows(x_refs):
    per_res = x_refs[0].shape[0] // RESIDUES
    return jnp.concatenate(
        [jnp.concatenate([ref[pl.ds(r, per_res, stride=RESIDUES), :] for r in range(RESIDUES)], axis=0)
         for ref in x_refs], axis=1)


def _proj_kernel(*refs, epilogue, tiles_per_seq):
    if epilogue == "qkv_gate":
        slabs = D_MODEL // LANES
        x, (nw_ref, w_ref, *rest) = _regrouped_rows(refs[:slabs]), refs[slabs:]
    else:
        x_ref, nw_ref, w_ref, *rest = refs
        x = x_ref[...]
    ms = jnp.mean(x * x, axis=-1, keepdims=True)
    xn = (x * lax.rsqrt(ms + RMS_EPS) * nw_ref[...]).astype(BF16)
    o = jnp.dot(xn, w_ref[...], preferred_element_type=F32)
    if epilogue == "shift":
        mu_ref, o_ref, prev_ref = rest
        tm = o.shape[0]
        first = pl.program_id(0) % tiles_per_seq == 0
        last = jnp.where(first, 0.0, prev_ref[SUBLANES - 1:SUBLANES, :])
        row = lax.broadcasted_iota(jnp.int32, (tm, 1), 0)
        oprev = jnp.where(row == 0, last, pltpu.roll(o, 1, 0))
        prev_ref[...] = o[tm - SUBLANES:, :]
        o_ref[...] = o + (oprev - o) * mu_ref[...]
    elif epilogue == "silu":
        (o_ref,) = rest
        o_ref[...] = (o * _sigmoid(o)).astype(o_ref.dtype)
    elif epilogue == "qkv_gate":
        qkv_ref, gate_ref = rest
        qkv_ref[...] = o[:, :3 * ATT_W].reshape(qkv_ref.shape)
        gate = o[:, 3 * ATT_W:]
        gate_ref[...] = (gate * _sigmoid(gate)).astype(gate_ref.dtype).reshape(gate_ref.shape)
    else:
        (o_ref,) = rest
        o_ref[...] = o


def _in_projection(x2, norm_w, w_bf16, seq, epilogue, mu=None, out_dtype=F32):
    tokens, n = x2.shape[0], w_bf16.shape[1]
    tm = min(PROJ_TM, seq)
    in_specs = [pl.BlockSpec((tm, D_MODEL), lambda i: (i, 0)),
                pl.BlockSpec((1, D_MODEL), lambda i: (0, 0)),
                pl.BlockSpec((D_MODEL, n), lambda i: (0, 0))]
    args = [x2, norm_w.reshape(1, D_MODEL), w_bf16]
    scratch = []
    if epilogue == "shift":
        in_specs.append(pl.BlockSpec((1, n), lambda i: (0, 0)))
        args.append(mu.reshape(1, n).astype(F32))
        scratch.append(pltpu.VMEM((SUBLANES, n), F32))
    return pl.pallas_call(
        functools.partial(_proj_kernel, epilogue=epilogue, tiles_per_seq=seq // tm),
        grid=(tokens // tm,),
        in_specs=in_specs,
        out_specs=pl.BlockSpec((tm, n), lambda i: (i, 0)),
        out_shape=jax.ShapeDtypeStruct((tokens, n), out_dtype),
        scratch_shapes=scratch,
        compiler_params=pltpu.CompilerParams(
            dimension_semantics=("arbitrary",), vmem_limit_bytes=VMEM_LIMIT),
        name="in_projection_" + epilogue,
    )(*args)


def _att_projection(x2, norm_w, w_bf16, batch, seq):
    tm = min(PROJ_TM, seq)
    tiles = seq // tm
    rows = tm // RESIDUES
    out_block = lambda width: pl.BlockSpec((None, RESIDUES, rows, width),
                                           lambda i: (i // tiles, 0, i % tiles, 0))
    return pl.pallas_call(
        functools.partial(_proj_kernel, epilogue="qkv_gate", tiles_per_seq=tiles),
        grid=(batch * tiles,),
        in_specs=[pl.BlockSpec((tm, LANES), functools.partial(lambda c, i: (i, c), c))
                  for c in range(D_MODEL // LANES)]
                 + [pl.BlockSpec((1, D_MODEL), lambda i: (0, 0)),
                    pl.BlockSpec(w_bf16.shape, lambda i: (0, 0))],
        out_specs=[out_block(3 * ATT_W), out_block(ATT_W)],
        out_shape=[jax.ShapeDtypeStruct((batch, RESIDUES, seq // RESIDUES, 3 * ATT_W), F32),
                   jax.ShapeDtypeStruct((batch, RESIDUES, seq // RESIDUES, ATT_W), BF16)],
        compiler_params=pltpu.CompilerParams(
            dimension_semantics=("arbitrary",), vmem_limit_bytes=VMEM_LIMIT),
        name="in_projection_attention",
    )(*[x2] * (D_MODEL // LANES), norm_w.reshape(1, D_MODEL), w_bf16)


OPERANDS = 6
STEP_CHUNKS = 2
STEP_ROWS = STEP_CHUNKS * CHUNK


def _rwkv_kernel(p_ref, gate_ref, w0_ref, a0_ref, kk_ref, ka_ref, rk_ref, lnw_ref, lnb_ref,
                 wu3_ref, au3_ref, y_ref, state_ref, ops_ref, rt_ref, gend_ref, bonus_ref,
                 *, steps_per_seq):
    t = pl.program_id(0)

    @pl.when(t == 0)
    def _():
        ops_ref[...] = jnp.zeros_like(ops_ref)
        rt_ref[...] = jnp.zeros_like(rt_ref)
        gend_ref[...] = jnp.zeros_like(gend_ref)
        bonus_ref[...] = jnp.zeros_like(bonus_ref)
        state_ref[...] = jnp.zeros_like(state_ref)

    fill, use = t % 2, (t + 1) % 2
    row = lax.broadcasted_iota(jnp.int32, (CHUNK, LANES), 0)
    lane = lax.broadcasted_iota(jnp.int32, (CHUNK, LANES), 1)
    head0 = lane < HEAD_DIM
    tcol = lane % CHUNK
    strict, incl = tcol < row, tcol <= row
    same16 = (tcol // 16) == (row // 16)
    same32 = (tcol // 32) == (row // 32)
    eye = jnp.where(tcol == row, 1.0, 0.0).astype(F32)
    bi = lax.broadcasted_iota(jnp.int32, (LANES, LANES), 0)
    bj = lax.broadcasted_iota(jnp.int32, (LANES, LANES), 1)
    block_diag = (bi // HEAD_DIM) == (bj // HEAD_DIM)
    head_ones = jnp.where(block_diag, 1.0, 0.0).astype(BF16)
    no_ones = jnp.zeros_like(head_ones)
    head_ones_diag = jnp.concatenate([jnp.concatenate([head_ones, no_ones], axis=1),
                                      jnp.concatenate([no_ones, head_ones], axis=1)], axis=0)
    ti = lax.broadcasted_iota(jnp.int32, (STEP_ROWS, STEP_ROWS), 0)
    tj = lax.broadcasted_iota(jnp.int32, (STEP_ROWS, STEP_ROWS), 1)
    tri = jnp.where((tj <= ti) & (tj // CHUNK == ti // CHUNK), 1.0, 0.0).astype(BF16)
    pairs = range(PAIRS)
    chunks = range(STEP_CHUNKS)
    units = [(c, g) for c in chunks for g in pairs]
    cols = lambda c0, g: slice(c0 + g * LANES, c0 + (g + 1) * LANES)
    rows = lambda c: slice(c * CHUNK, (c + 1) * CHUNK)

    def stack(x):
        x = x.astype(BF16)
        zero = jnp.zeros_like(x)
        return jnp.concatenate([jnp.where(head0, x, zero), jnp.where(head0, zero, x)], axis=0)

    def chain():
        ops = lambda i, c, g: ops_ref[use, i, rows(c), cols(0, g)]
        a_t, b_t, k_t, v, b_h, k_h = ([ops(i, c, g) for c, g in units] for i in range(OPERANDS))
        r_t = [rt_ref[use, rows(c), cols(0, g)] for c, g in units]
        n = range(len(units))
        v_s = [stack(x) for x in v]
        gram = [_mm_nt(jnp.concatenate([a_t[u], r_t[u]], axis=0),
                       jnp.concatenate([stack(b_t[u]), stack(k_t[u])], axis=0)) for u in n]
        yield
        a_ab = [jnp.where(strict, x[:CHUNK, :LANES], 0.0) for x in gram]
        a_ak = [jnp.where(strict, x[:CHUNK, LANES:], 0.0) for x in gram]
        a_rb = [jnp.where(incl, x[CHUNK:, :LANES], 0.0) for x in gram]
        a_rk = [jnp.where(incl, x[CHUNK:, LANES:], 0.0) for x in gram]
        nil = [jnp.where(same16, x, 0.0) for x in a_ab]
        inv = [eye + x for x in nil]
        nil = [_mm(x, stack(x)) for x in nil]
        yield
        av = [_mm(jnp.concatenate([a_ak[u], a_rk[u]], axis=0), v_s[u]) for u in n]
        yield
        for _ in range(2):
            both = [_mm(jnp.concatenate([nil[u], inv[u]], axis=0), stack(nil[u])) for u in n]
            nil = [x[:CHUNK] for x in both]
            inv = [inv[u] + both[u][CHUNK:] for u in n]
            yield
        vk = [_mm_tn(v[u], k_h[u]) for u in n]
        yield
        inv = [inv[u] + _mm(inv[u], stack(nil[u])) for u in n]
        yield
        for off_diag in (same32 & ~same16, ~same32):
            low = [_mm(inv[u], stack(jnp.where(off_diag, a_ab[u], 0.0))) for u in n]
            yield
            inv = [inv[u] + _mm(low[u], stack(inv[u])) for u in n]
            yield
        tx = [_mm(inv[u], jnp.concatenate([stack(a_t[u]), stack(av[u][:CHUNK])], axis=1))
              for u in n]
        yield
        zz = [_mm(a_rb[u], jnp.concatenate([stack(tx[u][:, :LANES]), stack(tx[u][:, LANES:])], axis=1))
              for u in n]
        yield
        r_p = [r_t[u] + zz[u][:, :LANES] for u in n]
        y0 = [zz[u][:, LANES:] + av[u][CHUNK:] for u in n]
        mn = [_mm_tn(tx[u], b_h[u]) for u in n]
        yield
        fresh = lax.rem(t - 1, steps_per_seq) == 0
        state = [jnp.where(fresh, 0.0, state_ref[g]) for g in pairs]
        y = []
        for u, (c, g) in enumerate(units):
            y.append(_mm_nt(r_p[u], state[g]) + y0[u])
            m_t = jnp.where(block_diag, mn[u][:LANES], 0.0)
            n_t = jnp.where(block_diag, mn[u][LANES:] + vk[u], 0.0)
            g_end = gend_ref[use, c * SUBLANES:c * SUBLANES + 1, cols(0, g)]
            state[g] = state[g] * g_end + _mm(state[g], m_t) + n_t
            if g == PAIRS - 1:
                yield
        for g in pairs:
            state_ref[g] = state[g]
        y_all = jnp.concatenate(y, axis=0)
        moments = jnp.dot(jnp.concatenate([y_all, y_all * y_all], axis=1).astype(BF16), head_ones_diag,
                          preferred_element_type=F32) * (1.0 / HEAD_DIM)
        for u, (c, g) in enumerate(units):
            mean = moments[u * CHUNK:(u + 1) * CHUNK, :LANES]
            var = moments[u * CHUNK:(u + 1) * CHUNK, LANES:] - mean * mean
            yn = (y[u] - mean) * lax.rsqrt(var + GN_EPS)
            yn = yn * lnw_ref[:, cols(0, g)] + lnb_ref[:, cols(0, g)]
            y_ref[rows(c), cols(0, g)] = ((yn + bonus_ref[use, rows(c), cols(0, g)])
                                          * gate_ref[rows(c), cols(0, g)].astype(F32)).astype(y_ref.dtype)

    def prepare():
        lo = p_ref[:, LORA_COL0:LORA_COL0 + LANES]
        z_all = -(w0_ref[...] + _mm_x3(jnp.tanh(lo[:, :LORA_RANK]), wu3_ref[...]))
        a_all = a0_ref[...] + _mm_x3(lo[:, LORA_RANK:], au3_ref[...])
        yield
        for g in pairs:
            c = cols(0, g)
            z = z_all[:, c]
            w = -(jnp.maximum(z, 0.0) + jnp.log(1.0 + jnp.exp(-jnp.abs(z)))) - 0.5
            lw = -jnp.exp(w)
            lg = _mm_split_rhs(tri, lw)
            ends = [lg[(i + 1) * CHUNK - 1:(i + 1) * CHUNK, :] for i in chunks]
            lg_end = jnp.concatenate([jnp.broadcast_to(e, (CHUNK, LANES)) for e in ends], axis=0)
            for i in chunks:
                gend_ref[fill, i * SUBLANES:(i + 1) * SUBLANES, c] = jnp.broadcast_to(
                    jnp.exp(ends[i]), (SUBLANES, LANES))
            e_inv = jnp.exp(-lg)
            e_end = jnp.exp(lg_end - lg)
            a = _sigmoid(a_all[:, c])
            r, k, v = p_ref[:, c], p_ref[:, cols(K_COL0, g)], p_ref[:, cols(V_COL0, g)]
            kk = k * kk_ref[:, c]
            k2 = k * (1.0 + (a - 1.0) * ka_ref[:, c])
            sums = jnp.dot(jnp.concatenate([kk * kk, r * k2 * rk_ref[:, c]], axis=1).astype(BF16),
                           head_ones_diag, preferred_element_type=F32)
            kk = kk / jnp.maximum(jnp.sqrt(sums[:, :LANES]), 1e-12)
            bonus_ref[fill, :, c] = sums[:, LANES:] * v
            rt_ref[fill, :, c] = r * jnp.exp(lg)
            ops_ref[fill, 0, :, c] = (-kk * jnp.exp(lg - lw)).astype(BF16)
            ops_ref[fill, 1, :, c] = (kk * a * e_inv).astype(BF16)
            ops_ref[fill, 2, :, c] = (k2 * e_inv).astype(BF16)
            ops_ref[fill, 3, :, c] = v.astype(BF16)
            ops_ref[fill, 4, :, c] = (kk * a * e_end).astype(BF16)
            ops_ref[fill, 5, :, c] = (k2 * e_end).astype(BF16)
            yield

    done = object()
    streams = [chain(), prepare()]
    while streams:
        streams = [s for s in streams if next(s, done) is not done]


def _rwkv_time_mix(p_shift, p_gates, batch, seq, w0, w_up, a0, a_up, k_k, k_a, r_k, ln_x_w, ln_x_b):
    nt = batch * seq // STEP_ROWS
    row2 = lambda v: v.reshape(1, -1).astype(F32)
    whole = lambda shape: pl.BlockSpec(shape, lambda t: (0, 0))
    chan = whole((1, RWKV_W))
    lora = whole((3 * LORA_RANK, RWKV_W))
    prepared = lambda t: (jnp.minimum(t, nt - 1), 0)
    chained = lambda t: (jnp.maximum(t - 1, 0), 0)
    return pl.pallas_call(
        functools.partial(_rwkv_kernel, steps_per_seq=seq // STEP_ROWS),
        grid=(nt + 1,),
        in_specs=[pl.BlockSpec((STEP_ROWS, SHIFT_COLS), prepared),
                  pl.BlockSpec((STEP_ROWS, RWKV_W), chained),
                  chan, chan, chan, chan, chan, chan, chan, lora, lora],
        out_specs=pl.BlockSpec((STEP_ROWS, RWKV_W), chained),
        out_shape=jax.ShapeDtypeStruct((batch * seq, RWKV_W), BF16),
        scratch_shapes=[pltpu.VMEM((PAIRS, LANES, LANES), F32),
                        pltpu.VMEM((2, OPERANDS, STEP_ROWS, RWKV_W), BF16),
                        pltpu.VMEM((2, STEP_ROWS, RWKV_W), F32),
                        pltpu.VMEM((2, STEP_CHUNKS * SUBLANES, RWKV_W), F32),
                        pltpu.VMEM((2, STEP_ROWS, RWKV_W), F32)],
        compiler_params=pltpu.CompilerParams(
            dimension_semantics=("arbitrary",), vmem_limit_bytes=VMEM_LIMIT),
        name="rwkv7_time_mix",
    )(p_shift, p_gates, row2(w0), row2(a0), row2(k_k), row2(k_a), row2(r_k), row2(ln_x_w), row2(ln_x_b),
      _stack_x3(w_up), _stack_x3(a_up))


def _attn_kernel(q_ref, k_ref, v_ref, gate_ref, y_ref, acc_scr, max_scr, sum_scr, *, seq):
    n_blocks = seq // ATT_BLK
    qi = lax.broadcasted_iota(jnp.int32, (2 * ATT_BLK, ATT_BLK), 0) % ATT_BLK
    kj = lax.broadcasted_iota(jnp.int32, (2 * ATT_BLK, ATT_BLK), 1)
    head0 = lax.broadcasted_iota(jnp.int32, (ATT_BLK, LANES), 1) < HEAD_DIM
    ones = jnp.ones((ATT_BLK, LANES), BF16)
    scale = HEAD_DIM ** -0.5 * LOG2_E
    neg = -1e30

    def attend(pi, d, kv_blocks, blocks):
        pieces = RESIDUES // d
        length = ATT_BLK // pieces
        pos = lambda i: pieces * (i % length) + i // length
        ok_prev, ok_cur = pos(kj) >= pos(qi), pos(kj) <= pos(qi)

        def at(res, blk, j):
            return j * d + res, pl.ds(pl.multiple_of(blk * length, length), length)

        def load(ref, res, blk):
            parts = []
            for j in range(pieces):
                r, rows = at(res, blk, j)
                parts.append(ref[r, rows, :])
            return jnp.concatenate(parts, axis=0) if pieces > 1 else parts[0]

        def store(scr, res, blk, val):
            for j in range(pieces):
                r, rows = at(res, blk, j)
                scr[pi, r, rows, :] = val[j * length:(j + 1) * length]

        kb = [load(k_ref, *b).astype(BF16) for b in kv_blocks]
        vb = [jnp.concatenate([load(v_ref, *b).astype(BF16), ones], axis=1) for b in kv_blocks]
        qs = []
        for cur, _, _ in blocks:
            q = load(q_ref, *kv_blocks[cur]) * scale
            qs.append(jnp.concatenate([jnp.where(head0, q, 0.0), jnp.where(head0, 0.0, q)],
                                      axis=0).astype(BF16))
        scores = []
        for (cur, prev, prev_ok), q in zip(blocks, qs):
            if prev is None:
                scores.append(jnp.where(ok_cur, _mm_nt(q, kb[cur]), neg))
            else:
                s = _mm_nt(q, jnp.concatenate([kb[prev], kb[cur]], axis=0))
                okp = ok_prev if prev_ok is None else ok_prev & prev_ok
                scores.append(jnp.concatenate([jnp.where(okp, s[:, :ATT_BLK], neg),
                                               jnp.where(ok_cur, s[:, ATT_BLK:], neg)], axis=1))
        mx = [jnp.max(s, axis=-1, keepdims=True) for s in scores]
        probs = [jnp.exp2(s - m).astype(BF16) for s, m in zip(scores, mx)]
        pv = [_mm(p, vb[cur] if prev is None else jnp.concatenate([vb[prev], vb[cur]], axis=0))
              for (cur, prev, _), p in zip(blocks, probs)]
        for (cur, _, _), m, x in zip(blocks, mx, pv):
            res, blk = kv_blocks[cur]
            store(acc_scr, res, blk, jnp.where(head0, x[:ATT_BLK, :LANES], x[ATT_BLK:, :LANES]))
            store(sum_scr, res, blk, jnp.where(head0, x[:ATT_BLK, LANES:], x[ATT_BLK:, LANES:]))
            store(max_scr, res, blk, jnp.where(head0, m[:ATT_BLK], m[ATT_BLK:]))

    for pi, d in enumerate(DILATIONS):
        nb = n_blocks // d
        if nb <= ATT_GROUP:
            n_res = ATT_GROUP // nb
            def body(it, carry, pi=pi, d=d, nb=nb, n_res=n_res):
                kv = [(it * n_res + r, u) for r in range(n_res) for u in range(nb)]
                blocks = [(r * nb + u, r * nb + u - 1 if u else None, None)
                          for r in range(n_res) for u in range(nb)]
                attend(pi, d, kv, blocks)
                return carry
            trips = d // n_res
        else:
            per_res = nb // ATT_GROUP
            def body(it, carry, pi=pi, d=d, per_res=per_res):
                res, blk0 = it // per_res, (it % per_res) * ATT_GROUP
                kv = [(res, jnp.maximum(blk0 - 1, 0))] + [(res, blk0 + u) for u in range(ATT_GROUP)]
                attend(pi, d, kv, [(1, 0, blk0 > 0)] + [(u + 1, u, None) for u in range(1, ATT_GROUP)])
                return carry
            trips = d * per_res
        lax.fori_loop(0, trips, body, 0)

    m = jnp.maximum(jnp.maximum(max_scr[0], max_scr[1]), max_scr[2])
    num = den = None
    for pi in range(len(DILATIONS)):
        w = jnp.exp2(max_scr[pi] - m)
        num = w * acc_scr[pi] if num is None else num + w * acc_scr[pi]
        den = w * sum_scr[pi] if den is None else den + w * sum_scr[pi]
    y = num / den * gate_ref[...].astype(F32)
    for r in range(RESIDUES):
        y_ref[pl.ds(r, seq // RESIDUES, stride=RESIDUES), :] = y[r]


def _dilated_attention(p_att, p_gate, batch, seq):
    per_res = seq // RESIDUES

    def tile(blk0):
        return pl.BlockSpec((RESIDUES, per_res, LANES), lambda b, g: (b, 0, blk0 + g))

    return pl.pallas_call(
        functools.partial(_attn_kernel, seq=seq),
        grid=(batch, PAIRS),
        in_specs=[tile(0), tile(PAIRS), tile(2 * PAIRS), tile(0)],
        out_specs=pl.BlockSpec((seq, LANES), lambda b, g: (b, g)),
        out_shape=jax.ShapeDtypeStruct((batch * seq, ATT_W), F32),
        scratch_shapes=[pltpu.VMEM((len(DILATIONS), RESIDUES, per_res, LANES), F32)] * 3,
        compiler_params=pltpu.CompilerParams(
            dimension_semantics=("arbitrary", "arbitrary"), vmem_limit_bytes=VMEM_LIMIT),
        name="dilated_attention",
    )(p_att, p_att, p_att, p_gate)


def _out_kernel(yr_ref, ya_ref, x_ref, wr_ref, wa_ref, fw_ref, o_ref):
    h = (x_ref[...]
         + jnp.dot(yr_ref[...], wr_ref[...], preferred_element_type=F32)
         + jnp.dot(ya_ref[...].astype(BF16), wa_ref[...], preferred_element_type=F32))
    ms = jnp.mean(h * h, axis=-1, keepdims=True)
    o_ref[...] = h * lax.rsqrt(ms + RMS_EPS) * fw_ref[...]


def _out_projection(y_rwkv, y_att, x2, w_out_bf16, final_norm_w, tm):
    tokens = x2.shape[0]
    tok = lambda width: pl.BlockSpec((tm, width), lambda i: (i, 0))
    return pl.pallas_call(
        _out_kernel,
        grid=(tokens // tm,),
        in_specs=[
            tok(RWKV_W), tok(ATT_W), tok(D_MODEL),
            pl.BlockSpec((RWKV_W, D_MODEL), lambda i: (0, 0)),
            pl.BlockSpec((ATT_W, D_MODEL), lambda i: (1, 0)),
            pl.BlockSpec((1, D_MODEL), lambda i: (0, 0)),
        ],
        out_specs=tok(D_MODEL),
        out_shape=jax.ShapeDtypeStruct((tokens, D_MODEL), F32),
        compiler_params=pltpu.CompilerParams(
            dimension_semantics=("arbitrary",), vmem_limit_bytes=VMEM_LIMIT),
        name="out_projection",
    )(y_rwkv, y_att, x2, w_out_bf16, w_out_bf16, final_norm_w.reshape(1, D_MODEL))


def kernel(x, norm_w, w_in, mu_shift, w0, w_up, a0, a_up, k_k, k_a, r_k, ln_x_w, ln_x_b, w_out,
           final_norm_w):
    batch, seq, _ = x.shape
    assert seq % (ATT_BLK * max(DILATIONS)) == 0 and seq % STEP_ROWS == 0
    x2 = x.astype(F32).reshape(batch * seq, D_MODEL)
    nw = norm_w.astype(F32)
    w_bf16 = w_in.astype(BF16)
    w_att = jnp.concatenate([w_bf16[:, ATT0:ATT_GATE0], w_bf16[:, ATT_GATE0:]], axis=1)
    p_shift = _in_projection(x2, nw, w_bf16[:, :SHIFT_COLS], seq, "shift", mu_shift)
    gate_rwkv = _in_projection(x2, nw, w_bf16[:, RWKV_GATE0:ATT0], seq, "silu", out_dtype=BF16)
    p_att, gate_att = _att_projection(x2, nw, w_att, batch, seq)
    y_rwkv = _rwkv_time_mix(p_shift, gate_rwkv, batch, seq, w0, w_up, a0, a_up, k_k, k_a, r_k,
                            ln_x_w, ln_x_b)
    per_res = seq // RESIDUES
    y_att = _dilated_attention(p_att.reshape(batch * RESIDUES, per_res, 3 * ATT_W),
                               gate_att.reshape(batch * RESIDUES, per_res, ATT_W), batch, seq)
    out = _out_projection(y_rwkv, y_att, x2, w_out.astype(BF16), final_norm_w.astype(F32),
                          tm=min(PROJ_TM, batch * seq))
    return out.reshape(batch, seq, D_MODEL).astype(x.dtype)
```

```python
import functools

import jax
import jax.numpy as jnp
from jax import lax
from jax.experimental import pallas as pl
from jax.experimental.pallas import tpu as pltpu

D_MODEL = 1024
HEAD_DIM = 64
RWKV_W = 1024
ATT_W = 1024
MIX_W = RWKV_W + ATT_W
LORA_RANK = 64
SHIFT_COLS = 3 * RWKV_W + 2 * LORA_RANK
RWKV_GATE0 = SHIFT_COLS
ATT0 = RWKV_GATE0 + RWKV_W
ATT_GATE0 = ATT0 + 3 * ATT_W
DILATIONS = (1, 4, 16)
ATT_BLK = 128
RESIDUES = max(DILATIONS)
ATT_GROUP = 8
LOG2_E = 1.4426950408889634
RMS_EPS = 1e-5
GN_EPS = 64e-5

LANES = 128
SUBLANES = 8
PAIRS = RWKV_W // LANES
CHUNK = 64
K_COL0, V_COL0, LORA_COL0 = RWKV_W, 2 * RWKV_W, 3 * RWKV_W
PROJ_TM = 512

F32 = jnp.float32
BF16 = jnp.bfloat16
VMEM_LIMIT = 56 * 1024 * 1024

NT_DIMS = (((1,), (1,)), ((), ()))
TN_DIMS = (((0,), (0,)), ((), ()))


def _mm(a, b):
    return jnp.dot(a.astype(BF16), b.astype(BF16), preferred_element_type=F32)


def _mm_nt(a, b):
    return lax.dot_general(a.astype(BF16), b.astype(BF16), NT_DIMS, preferred_element_type=F32)


def _mm_tn(a, b):
    return lax.dot_general(a.astype(BF16), b.astype(BF16), TN_DIMS, preferred_element_type=F32)


def _split(x):
    hi = x.astype(BF16)
    return hi, (x - hi.astype(F32)).astype(BF16)


def _mm_split_rhs(w_bf16, x):
    n = x.shape[1]
    both = jnp.dot(w_bf16, jnp.concatenate(_split(x), axis=1), preferred_element_type=F32)
    return both[:, :n] + both[:, n:]


def _mm_x3(x, w3):
    hi, lo = _split(x)
    return jnp.dot(jnp.concatenate([hi, lo, hi], axis=1), w3, preferred_element_type=F32)


def _stack_x3(w):
    hi, lo = _split(w.astype(F32))
    return jnp.concatenate([hi, hi, lo], axis=0)


def _sigmoid(x):
    return 1.0 / (1.0 + jnp.exp(-x))


def _regrouped_rows(x_refs):
    per_res = x_refs[0].shape[0] // RESIDUES
    return jnp.concatenate(
        [jnp.concatenate([ref[pl.ds(r, per_res, stride=RESIDUES), :] for r in range(RESIDUES)], axis=0)
         for ref in x_refs], axis=1)


def _proj_kernel(*refs, epilogue, tiles_per_seq):
    if epilogue == "qkv_gate":
        slabs = D_MODEL // LANES
        x, (nw_ref, w_ref, *rest) = _regrouped_rows(refs[:slabs]), refs[slabs:]
    else:
        x_ref, nw_ref, w_ref, *rest = refs
        x = x_ref[...]
    ms = jnp.mean(x * x, axis=-1, keepdims=True)
    xn = (x * lax.rsqrt(ms + RMS_EPS) * nw_ref[...]).astype(BF16)
    o = jnp.dot(xn, w_ref[...], preferred_element_type=F32)
    if epilogue == "shift_gate":
        mu_ref, o_ref, gate_ref, prev_ref = rest
        tm = o.shape[0]
        gate = o[:, SHIFT_COLS:]
        gate_ref[...] = (gate * _sigmoid(gate)).astype(gate_ref.dtype)
        o = o[:, :SHIFT_COLS]
        first = pl.program_id(0) % tiles_per_seq == 0
        last = jnp.where(first, 0.0, prev_ref[SUBLANES - 1:SUBLANES, :])
        row = lax.broadcasted_iota(jnp.int32, (tm, 1), 0)
        oprev = jnp.where(row == 0, last, pltpu.roll(o, 1, 0))
        prev_ref[...] = o[tm - SUBLANES:, :]
        o_ref[...] = o + (oprev - o) * mu_ref[...]
    elif epilogue == "qkv_gate":
        qkv_ref, gate_ref = rest
        qkv_ref[...] = o[:, :3 * ATT_W].reshape(qkv_ref.shape)
        gate = o[:, 3 * ATT_W:]
        gate_ref[...] = (gate * _sigmoid(gate)).astype(gate_ref.dtype).reshape(gate_ref.shape)


def _rwkv_projection(x2, norm_w, w_bf16, seq, mu):
    tokens = x2.shape[0]
    tm = min(PROJ_TM, seq)
    once = pl.Buffered(1)
    return pl.pallas_call(
        functools.partial(_proj_kernel, epilogue="shift_gate", tiles_per_seq=seq // tm),
        grid=(tokens // tm,),
        in_specs=[pl.BlockSpec((tm, D_MODEL), lambda i: (i, 0)),
                  pl.BlockSpec((1, D_MODEL), lambda i: (0, 0), pipeline_mode=once),
                  pl.BlockSpec(w_bf16.shape, lambda i: (0, 0), pipeline_mode=once),
                  pl.BlockSpec((1, SHIFT_COLS), lambda i: (0, 0), pipeline_mode=once)],
        out_specs=[pl.BlockSpec((tm, SHIFT_COLS), lambda i: (i, 0)),
                   pl.BlockSpec((tm, RWKV_W), lambda i: (i, 0))],
        out_shape=[jax.ShapeDtypeStruct((tokens, SHIFT_COLS), F32),
                   jax.ShapeDtypeStruct((tokens, RWKV_W), BF16)],
        scratch_shapes=[pltpu.VMEM((SUBLANES, SHIFT_COLS), F32)],
        compiler_params=pltpu.CompilerParams(
            dimension_semantics=("arbitrary",), vmem_limit_bytes=VMEM_LIMIT),
        name="in_projection_rwkv",
    )(x2, norm_w.reshape(1, D_MODEL), w_bf16, mu.reshape(1, SHIFT_COLS).astype(F32))


def _att_projection(x2, norm_w, w_bf16, batch, seq):
    tm = min(PROJ_TM, seq)
    tiles = seq // tm
    rows = tm // RESIDUES
    out_block = lambda width: pl.BlockSpec((None, RESIDUES, rows, width),
                                           lambda i: (i // tiles, 0, i % tiles, 0))
    return pl.pallas_call(
        functools.partial(_proj_kernel, epilogue="qkv_gate", tiles_per_seq=tiles),
        grid=(batch * tiles,),
        in_specs=[pl.BlockSpec((tm, LANES), functools.partial(lambda c, i: (i, c), c))
                  for c in range(D_MODEL // LANES)]
                 + [pl.BlockSpec((1, D_MODEL), lambda i: (0, 0)),
                    pl.BlockSpec(w_bf16.shape, lambda i: (0, 0))],
        out_specs=[out_block(3 * ATT_W), out_block(ATT_W)],
        out_shape=[jax.ShapeDtypeStruct((batch, RESIDUES, seq // RESIDUES, 3 * ATT_W), F32),
                   jax.ShapeDtypeStruct((batch, RESIDUES, seq // RESIDUES, ATT_W), BF16)],
        compiler_params=pltpu.CompilerParams(
            dimension_semantics=("arbitrary",), vmem_limit_bytes=VMEM_LIMIT),
        name="in_projection_attention",
    )(*[x2] * (D_MODEL // LANES), norm_w.reshape(1, D_MODEL), w_bf16)


OPERANDS = 6
STEP_CHUNKS = 2
STEP_ROWS = STEP_CHUNKS * CHUNK


def _rwkv_kernel(p_ref, gate_ref, w0_ref, a0_ref, kk_ref, ka_ref, rk_ref, lnw_ref, lnb_ref,
                 wu3_ref, au3_ref, y_ref, state_ref, ops_ref, rt_ref, gend_ref, bonus_ref,
                 *, steps_per_seq):
    t = pl.program_id(0)

    @pl.when(t == 0)
    def _():
        ops_ref[...] = jnp.zeros_like(ops_ref)
        rt_ref[...] = jnp.zeros_like(rt_ref)
        gend_ref[...] = jnp.zeros_like(gend_ref)
        bonus_ref[...] = jnp.zeros_like(bonus_ref)
        state_ref[...] = jnp.zeros_like(state_ref)

    fill, use = t % 2, (t + 1) % 2
    row = lax.broadcasted_iota(jnp.int32, (CHUNK, LANES), 0)
    lane = lax.broadcasted_iota(jnp.int32, (CHUNK, LANES), 1)
    head0 = lane < HEAD_DIM
    tcol = lane % CHUNK
    strict, incl = tcol < row, tcol <= row
    same16 = (tcol // 16) == (row // 16)
    same32 = (tcol // 32) == (row // 32)
    eye = jnp.where(tcol == row, 1.0, 0.0).astype(F32)
    bi = lax.broadcasted_iota(jnp.int32, (LANES, LANES), 0)
    bj = lax.broadcasted_iota(jnp.int32, (LANES, LANES), 1)
    block_diag = (bi // HEAD_DIM) == (bj // HEAD_DIM)
    head_ones = jnp.where(block_diag, 1.0, 0.0).astype(BF16)
    no_ones = jnp.zeros_like(head_ones)
    head_ones_diag = jnp.concatenate([jnp.concatenate([head_ones, no_ones], axis=1),
                                      jnp.concatenate([no_ones, head_ones], axis=1)], axis=0)
    ti = lax.broadcasted_iota(jnp.int32, (STEP_ROWS, STEP_ROWS), 0)
    tj = lax.broadcasted_iota(jnp.int32, (STEP_ROWS, STEP_ROWS), 1)
    tri = jnp.where((tj <= ti) & (tj // CHUNK == ti // CHUNK), 1.0, 0.0).astype(BF16)
    pairs = range(PAIRS)
    chunks = range(STEP_CHUNKS)
    units = [(c, g) for c in chunks for g in pairs]
    cols = lambda c0, g: slice(c0 + g * LANES, c0 + (g + 1) * LANES)
    rows = lambda c: slice(c * CHUNK, (c + 1) * CHUNK)

    def stack(x):
        x = x.astype(BF16)
        zero = jnp.zeros_like(x)
        return jnp.concatenate([jnp.where(head0, x, zero), jnp.where(head0, zero, x)], axis=0)

    def chain():
        ops = lambda i, c, g: ops_ref[use, i, rows(c), cols(0, g)]
        a_t, b_t, k_t, v, b_h, k_h = ([ops(i, c, g) for c, g in units] for i in range(OPERANDS))
        r_t = [rt_ref[use, rows(c), cols(0, g)] for c, g in units]
        n = range(len(units))
        v_s = [stack(x) for x in v]
        gram = [_mm_nt(jnp.concatenate([a_t[u], r_t[u]], axis=0),
                       jnp.concatenate([stack(b_t[u]), stack(k_t[u])], axis=0)) for u in n]
        yield
        a_ab = [jnp.where(strict, x[:CHUNK, :LANES], 0.0) for x in gram]
        a_ak = [jnp.where(strict, x[:CHUNK, LANES:], 0.0) for x in gram]
        a_rb = [jnp.where(incl, x[CHUNK:, :LANES], 0.0) for x in gram]
        a_rk = [jnp.where(incl, x[CHUNK:, LANES:], 0.0) for x in gram]
        nil = [jnp.where(same16, x, 0.0) for x in a_ab]
        inv = [eye + x for x in nil]
        nil = [_mm(x, stack(x)) for x in nil]
        yield
        av = [_mm(jnp.concatenate([a_ak[u], a_rk[u]], axis=0), v_s[u]) for u in n]
        yield
        for _ in range(2):
            both = [_mm(jnp.concatenate([nil[u], inv[u]], axis=0), stack(nil[u])) for u in n]
            nil = [x[:CHUNK] for x in both]
            inv = [inv[u] + both[u][CHUNK:] for u in n]
            yield
        vk = [_mm_tn(v[u], k_h[u]) for u in n]
        yield
        inv = [inv[u] + _mm(inv[u], stack(nil[u])) for u in n]
        yield
        for off_diag in (same32 & ~same16, ~same32):
            low = [_mm(inv[u], stack(jnp.where(off_diag, a_ab[u], 0.0))) for u in n]
            yield
            inv = [inv[u] + _mm(low[u], stack(inv[u])) for u in n]
            yield
        tx = [_mm(inv[u], jnp.concatenate([stack(a_t[u]), stack(av[u][:CHUNK])], axis=1))
              for u in n]
        yield
        zz = [_mm(a_rb[u], jnp.concatenate([stack(tx[u][:, :LANES]), stack(tx[u][:, LANES:])], axis=1))
              for u in n]
        yield
        r_p = [r_t[u] + zz[u][:, :LANES] for u in n]
        y0 = [zz[u][:, LANES:] + av[u][CHUNK:] for u in n]
        mn = [_mm_tn(tx[u], b_h[u]) for u in n]
        yield
        fresh = lax.rem(t - 1, steps_per_seq) == 0
        state = [jnp.where(fresh, 0.0, state_ref[g]) for g in pairs]
        y = []
        for u, (c, g) in enumerate(units):
            y.append(_mm_nt(r_p[u], state[g]) + y0[u])
            m_t = jnp.where(block_diag, mn[u][:LANES], 0.0)
            n_t = jnp.where(block_diag, mn[u][LANES:] + vk[u], 0.0)
            g_end = gend_ref[use, c * SUBLANES:c * SUBLANES + 1, cols(0, g)]
            state[g] = state[g] * g_end + _mm(state[g], m_t) + n_t
            if g == PAIRS - 1:
                yield
        for g in pairs:
            state_ref[g] = state[g]
        y_all = jnp.concatenate(y, axis=0)
        moments = jnp.dot(jnp.concatenate([y_all, y_all * y_all], axis=1).astype(BF16), head_ones_diag,
                          preferred_element_type=F32) * (1.0 / HEAD_DIM)
        for u, (c, g) in enumerate(units):
            mean = moments[u * CHUNK:(u + 1) * CHUNK, :LANES]
            var = moments[u * CHUNK:(u + 1) * CHUNK, LANES:] - mean * mean
            yn = (y[u] - mean) * lax.rsqrt(var + GN_EPS)
            yn = yn * lnw_ref[:, cols(0, g)] + lnb_ref[:, cols(0, g)]
            y_ref[rows(c), cols(0, g)] = ((yn + bonus_ref[use, rows(c), cols(0, g)])
                                          * gate_ref[rows(c), cols(0, g)].astype(F32)).astype(y_ref.dtype)

    def prepare():
        lo = p_ref[:, LORA_COL0:LORA_COL0 + LANES]
        z_all = -(w0_ref[...] + _mm_x3(jnp.tanh(lo[:, :LORA_RANK]), wu3_ref[...]))
        a_all = a0_ref[...] + _mm_x3(lo[:, LORA_RANK:], au3_ref[...])
        yield
        for g in pairs:
            c = cols(0, g)
            z = z_all[:, c]
            w = -(jnp.maximum(z, 0.0) + jnp.log(1.0 + jnp.exp(-jnp.abs(z)))) - 0.5
            lw = -jnp.exp(w)
            lg = _mm_split_rhs(tri, lw)
            ends = [lg[(i + 1) * CHUNK - 1:(i + 1) * CHUNK, :] for i in chunks]
            lg_end = jnp.concatenate([jnp.broadcast_to(e, (CHUNK, LANES)) for e in ends], axis=0)
            for i in chunks:
                gend_ref[fill, i * SUBLANES:(i + 1) * SUBLANES, c] = jnp.broadcast_to(
                    jnp.exp(ends[i]), (SUBLANES, LANES))
            e_inv = jnp.exp(-lg)
            e_end = jnp.exp(lg_end - lg)
            a = _sigmoid(a_all[:, c])
            r, k, v = p_ref[:, c], p_ref[:, cols(K_COL0, g)], p_ref[:, cols(V_COL0, g)]
            kk = k * kk_ref[:, c]
            k2 = k * (1.0 + (a - 1.0) * ka_ref[:, c])
            sums = jnp.dot(jnp.concatenate([kk * kk, r * k2 * rk_ref[:, c]], axis=1).astype(BF16),
                           head_ones_diag, preferred_element_type=F32)
            kk = kk / jnp.maximum(jnp.sqrt(sums[:, :LANES]), 1e-12)
            bonus_ref[fill, :, c] = sums[:, LANES:] * v
            rt_ref[fill, :, c] = r * jnp.exp(lg)
            ops_ref[fill, 0, :, c] = (-kk * jnp.exp(lg - lw)).astype(BF16)
            ops_ref[fill, 1, :, c] = (kk * a * e_inv).astype(BF16)
            ops_ref[fill, 2, :, c] = (k2 * e_inv).astype(BF16)
            ops_ref[fill, 3, :, c] = v.astype(BF16)
            ops_ref[fill, 4, :, c] = (kk * a * e_end).astype(BF16)
            ops_ref[fill, 5, :, c] = (k2 * e_end).astype(BF16)
            yield

    done = object()
    streams = [chain(), prepare()]
    while streams:
        streams = [s for s in streams if next(s, done) is not done]


def _rwkv_time_mix(p_shift, p_gates, batch, seq, w0, w_up, a0, a_up, k_k, k_a, r_k, ln_x_w, ln_x_b):
    nt = batch * seq // STEP_ROWS
    row2 = lambda v: v.reshape(1, -1).astype(F32)
    whole = lambda shape: pl.BlockSpec(shape, lambda t: (0, 0))
    chan = whole((1, RWKV_W))
    lora = whole((3 * LORA_RANK, RWKV_W))
    prepared = lambda t: (jnp.minimum(t, nt - 1), 0)
    chained = lambda t: (jnp.maximum(t - 1, 0), 0)
    return pl.pallas_call(
        functools.partial(_rwkv_kernel, steps_per_seq=seq // STEP_ROWS),
        grid=(nt + 1,),
        in_specs=[pl.BlockSpec((STEP_ROWS, SHIFT_COLS), prepared),
                  pl.BlockSpec((STEP_ROWS, RWKV_W), chained),
                  chan, chan, chan, chan, chan, chan, chan, lora, lora],
        out_specs=pl.BlockSpec((STEP_ROWS, RWKV_W), chained),
        out_shape=jax.ShapeDtypeStruct((batch * seq, RWKV_W), BF16),
        scratch_shapes=[pltpu.VMEM((PAIRS, LANES, LANES), F32),
                        pltpu.VMEM((2, OPERANDS, STEP_ROWS, RWKV_W), BF16),
                        pltpu.VMEM((2, STEP_ROWS, RWKV_W), F32),
                        pltpu.VMEM((2, STEP_CHUNKS * SUBLANES, RWKV_W), F32),
                        pltpu.VMEM((2, STEP_ROWS, RWKV_W), F32)],
        compiler_params=pltpu.CompilerParams(
            dimension_semantics=("arbitrary",), vmem_limit_bytes=VMEM_LIMIT),
        name="rwkv7_time_mix",
    )(p_shift, p_gates, row2(w0), row2(a0), row2(k_k), row2(k_a), row2(r_k), row2(ln_x_w), row2(ln_x_b),
      _stack_x3(w_up), _stack_x3(a_up))


def _attn_kernel(q_ref, k_ref, v_ref, gate_ref, y_ref, acc_scr, max_scr, sum_scr, *, seq):
    n_blocks = seq // ATT_BLK
    qi = lax.broadcasted_iota(jnp.int32, (2 * ATT_BLK, ATT_BLK), 0) % ATT_BLK
    kj = lax.broadcasted_iota(jnp.int32, (2 * ATT_BLK, ATT_BLK), 1)
    head0 = lax.broadcasted_iota(jnp.int32, (ATT_BLK, LANES), 1) < HEAD_DIM
    ones = jnp.ones((ATT_BLK, LANES), BF16)
    scale = HEAD_DIM ** -0.5 * LOG2_E
    neg = -1e30

    def attend(pi, d, kv_blocks, blocks):
        pieces = RESIDUES // d
        length = ATT_BLK // pieces
        pos = lambda i: pieces * (i % length) + i // length
        ok_prev, ok_cur = pos(kj) >= pos(qi), pos(kj) <= pos(qi)

        def at(res, blk, j):
            return j * d + res, pl.ds(pl.multiple_of(blk * length, length), length)

        def load(ref, res, blk):
            parts = []
            for j in range(pieces):
                r, rows = at(res, blk, j)
                parts.append(ref[r, rows, :])
            return jnp.concatenate(parts, axis=0) if pieces > 1 else parts[0]

        def store(scr, res, blk, val):
            for j in range(pieces):
                r, rows = at(res, blk, j)
                scr[pi, r, rows, :] = val[j * length:(j + 1) * length]

        kb = [load(k_ref, *b).astype(BF16) for b in kv_blocks]
        vb = [jnp.concatenate([load(v_ref, *b).astype(BF16), ones], axis=1) for b in kv_blocks]
        qs = []
        for cur, _, _ in blocks:
            q = load(q_ref, *kv_blocks[cur]) * scale
            qs.append(jnp.concatenate([jnp.where(head0, q, 0.0), jnp.where(head0, 0.0, q)],
                                      axis=0).astype(BF16))
        scores = []
        for (cur, prev, prev_ok), q in zip(blocks, qs):
            if prev is None:
                scores.append(jnp.where(ok_cur, _mm_nt(q, kb[cur]), neg))
            else:
                s = _mm_nt(q, jnp.concatenate([kb[prev], kb[cur]], axis=0))
                okp = ok_prev if prev_ok is None else ok_prev & prev_ok
                scores.append(jnp.concatenate([jnp.where(okp, s[:, :ATT_BLK], neg),
                                               jnp.where(ok_cur, s[:, ATT_BLK:], neg)], axis=1))
        mx = [jnp.max(s, axis=-1, keepdims=True) for s in scores]
        probs = [jnp.exp2(s - m).astype(BF16) for s, m in zip(scores, mx)]
        pv = [_mm(p, vb[cur] if prev is None else jnp.concatenate([vb[prev], vb[cur]], axis=0))
              for (cur, prev, _), p in zip(blocks, probs)]
        for (cur, _, _), m, x in zip(blocks, mx, pv):
            res, blk = kv_blocks[cur]
            store(acc_scr, res, blk, jnp.where(head0, x[:ATT_BLK, :LANES], x[ATT_BLK:, :LANES]))
            store(sum_scr, res, blk, jnp.where(head0, x[:ATT_BLK, LANES:], x[ATT_BLK:, LANES:]))
            store(max_scr, res, blk, jnp.where(head0, m[:ATT_BLK], m[ATT_BLK:]))

    for pi, d in enumerate(DILATIONS):
        nb = n_blocks // d
        if nb <= ATT_GROUP:
            n_res = ATT_GROUP // nb
            def body(it, carry, pi=pi, d=d, nb=nb, n_res=n_res):
                kv = [(it * n_res + r, u) for r in range(n_res) for u in range(nb)]
                blocks = [(r * nb + u, r * nb + u - 1 if u else None, None)
                          for r in range(n_res) for u in range(nb)]
                attend(pi, d, kv, blocks)
                return carry
            trips = d // n_res
        else:
            per_res = nb // ATT_GROUP
            def body(it, carry, pi=pi, d=d, per_res=per_res):
                res, blk0 = it // per_res, (it % per_res) * ATT_GROUP
                kv = [(res, jnp.maximum(blk0 - 1, 0))] + [(res, blk0 + u) for u in range(ATT_GROUP)]
                attend(pi, d, kv, [(1, 0, blk0 > 0)] + [(u + 1, u, None) for u in range(1, ATT_GROUP)])
                return carry
            trips = d * per_res
        lax.fori_loop(0, trips, body, 0)

    m = jnp.maximum(jnp.maximum(max_scr[0], max_scr[1]), max_scr[2])
    num = den = None
    for pi in range(len(DILATIONS)):
        w = jnp.exp2(max_scr[pi] - m)
        num = w * acc_scr[pi] if num is None else num + w * acc_scr[pi]
        den = w * sum_scr[pi] if den is None else den + w * sum_scr[pi]
    y = num / den * gate_ref[...].astype(F32)
    for r in range(RESIDUES):
        y_ref[pl.ds(r, seq // RESIDUES, stride=RESIDUES), :] = y[r]


def _dilated_attention(p_att, p_gate, batch, seq):
    per_res = seq // RESIDUES

    def tile(blk0):
        return pl.BlockSpec((RESIDUES, per_res, LANES), lambda b, g: (b, 0, blk0 + g))

    return pl.pallas_call(
        functools.partial(_attn_kernel, seq=seq),
        grid=(batch, PAIRS),
        in_specs=[tile(0), tile(PAIRS), tile(2 * PAIRS), tile(0)],
        out_specs=pl.BlockSpec((seq, LANES), lambda b, g: (b, g)),
        out_shape=jax.ShapeDtypeStruct((batch * seq, ATT_W), F32),
        scratch_shapes=[pltpu.VMEM((len(DILATIONS), RESIDUES, per_res, LANES), F32)] * 3,
        compiler_params=pltpu.CompilerParams(
            dimension_semantics=("arbitrary", "arbitrary"), vmem_limit_bytes=VMEM_LIMIT),
        name="dilated_attention",
    )(p_att, p_att, p_att, p_gate)


def _out_kernel(yr_ref, ya_ref, x_ref, wr_ref, wa_ref, fw_ref, o_ref):
    h = (x_ref[...]
         + jnp.dot(yr_ref[...], wr_ref[...], preferred_element_type=F32)
         + jnp.dot(ya_ref[...].astype(BF16), wa_ref[...], preferred_element_type=F32))
    ms = jnp.mean(h * h, axis=-1, keepdims=True)
    o_ref[...] = h * lax.rsqrt(ms + RMS_EPS) * fw_ref[...]


def _out_projection(y_rwkv, y_att, x2, w_out_bf16, final_norm_w, tm):
    tokens = x2.shape[0]
    tok = lambda width: pl.BlockSpec((tm, width), lambda i: (i, 0))
    return pl.pallas_call(
        _out_kernel,
        grid=(tokens // tm,),
        in_specs=[
            tok(RWKV_W), tok(ATT_W), tok(D_MODEL),
            pl.BlockSpec((RWKV_W, D_MODEL), lambda i: (0, 0)),
            pl.BlockSpec((ATT_W, D_MODEL), lambda i: (1, 0)),
            pl.BlockSpec((1, D_MODEL), lambda i: (0, 0)),
        ],
        out_specs=tok(D_MODEL),
        out_shape=jax.ShapeDtypeStruct((tokens, D_MODEL), F32),
        compiler_params=pltpu.CompilerParams(
            dimension_semantics=("arbitrary",), vmem_limit_bytes=VMEM_LIMIT),
        name="out_projection",
    )(y_rwkv, y_att, x2, w_out_bf16, w_out_bf16, final_norm_w.reshape(1, D_MODEL))


def kernel(x, norm_w, w_in, mu_shift, w0, w_up, a0, a_up, k_k, k_a, r_k, ln_x_w, ln_x_b, w_out,
           final_norm_w):
    batch, seq, _ = x.shape
    assert seq % (ATT_BLK * max(DILATIONS)) == 0 and seq % STEP_ROWS == 0
    x2 = x.astype(F32).reshape(batch * seq, D_MODEL)
    nw = norm_w.astype(F32)
    w_bf16 = w_in.astype(BF16)
    w_att = jnp.concatenate([w_bf16[:, ATT0:ATT_GATE0], w_bf16[:, ATT_GATE0:]], axis=1)
    p_shift, gate_rwkv = _rwkv_projection(x2, nw, w_bf16[:, :ATT0], seq, mu_shift)
    p_att, gate_att = _att_projection(x2, nw, w_att, batch, seq)
    y_rwkv = _rwkv_time_mix(p_shift, gate_rwkv, batch, seq, w0, w_up, a0, a_up, k_k, k_a, r_k,
                            ln_x_w, ln_x_b)
    per_res = seq // RESIDUES
    y_att = _dilated_attention(p_att.reshape(batch * RESIDUES, per_res, 3 * ATT_W),
                               gate_att.reshape(batch * RESIDUES, per_res, ATT_W), batch, seq)
    out = _out_projection(y_rwkv, y_att, x2, w_out.astype(BF16), final_norm_w.astype(F32),
                          tm=min(PROJ_TM, batch * seq))
    return out.reshape(batch, seq, D_MODEL).astype(x.dtype)
```

```python
import functools

import jax
import jax.numpy as jnp
from jax import lax
from jax.experimental import pallas as pl
from jax.experimental.pallas import tpu as pltpu

D_MODEL = 1024
HEAD_DIM = 64
RWKV_W = 1024
ATT_W = 1024
MIX_W = RWKV_W + ATT_W
LORA_RANK = 64
SHIFT_COLS = 3 * RWKV_W + 2 * LORA_RANK
RWKV_GATE0 = SHIFT_COLS
ATT0 = RWKV_GATE0 + RWKV_W
ATT_GATE0 = ATT0 + 3 * ATT_W
DILATIONS = (1, 4, 16)
ATT_BLK = 128
RESIDUES = max(DILATIONS)
ATT_GROUP = 8
LOG2_E = 1.4426950408889634
RMS_EPS = 1e-5
GN_EPS = 64e-5

LANES = 128
SUBLANES = 8
PAIRS = RWKV_W // LANES
CHUNK = 64
K_COL0, V_COL0, LORA_COL0 = RWKV_W, 2 * RWKV_W, 3 * RWKV_W
PROJ_TM = 512

F32 = jnp.float32
BF16 = jnp.bfloat16
VMEM_LIMIT = 56 * 1024 * 1024

NT_DIMS = (((1,), (1,)), ((), ()))
TN_DIMS = (((0,), (0,)), ((), ()))


def _mm(a, b):
    return jnp.dot(a.astype(BF16), b.astype(BF16), preferred_element_type=F32)


def _mm_nt(a, b):
    return lax.dot_general(a.astype(BF16), b.astype(BF16), NT_DIMS, preferred_element_type=F32)


def _mm_tn(a, b):
    return lax.dot_general(a.astype(BF16), b.astype(BF16), TN_DIMS, preferred_element_type=F32)


def _split(x):
    hi = x.astype(BF16)
    return hi, (x - hi.astype(F32)).astype(BF16)


def _mm_split_rhs(w_bf16, x):
    n = x.shape[1]
    both = jnp.dot(w_bf16, jnp.concatenate(_split(x), axis=1), preferred_element_type=F32)
    return both[:, :n] + both[:, n:]


def _mm_x3(x, w3):
    hi, lo = _split(x)
    return jnp.dot(jnp.concatenate([hi, lo, hi], axis=1), w3, preferred_element_type=F32)


def _stack_x3(w):
    hi, lo = _split(w.astype(F32))
    return jnp.concatenate([hi, hi, lo], axis=0)


def _sigmoid(x):
    return 1.0 / (1.0 + jnp.exp(-x))


REGROUP_STRIDE = 4


def _regrouped_rows(x_refs, tmp_ref):
    tm = x_refs[0].shape[0]
    quarter, per_res = tm // REGROUP_STRIDE, tm // RESIDUES
    slabs = []
    for c, ref in enumerate(x_refs):
        for g in range(REGROUP_STRIDE):
            tmp_ref[c, g * quarter:(g + 1) * quarter, :] = ref[pl.ds(g, quarter, stride=REGROUP_STRIDE), :]
        slabs.append(jnp.concatenate(
            [tmp_ref[c, pl.ds((r % REGROUP_STRIDE) * quarter + r // REGROUP_STRIDE, per_res,
                              stride=REGROUP_STRIDE), :] for r in range(RESIDUES)], axis=0))
    return jnp.concatenate(slabs, axis=1)


def _proj_kernel(*refs, epilogue, tiles_per_seq):
    if epilogue == "qkv_gate":
        slabs = D_MODEL // LANES
        nw_ref, w_ref, *rest, tmp_ref = refs[slabs:]
        x = _regrouped_rows(refs[:slabs], tmp_ref)
    else:
        x_ref, nw_ref, w_ref, *rest = refs
        x = x_ref[...]
    ms = jnp.mean(x * x, axis=-1, keepdims=True)
    xn = (x * lax.rsqrt(ms + RMS_EPS) * nw_ref[...]).astype(BF16)
    o = jnp.dot(xn, w_ref[...], preferred_element_type=F32)
    if epilogue == "shift_gate":
        mu_ref, o_ref, gate_ref, prev_ref = rest
        tm = o.shape[0]
        gate = o[:, SHIFT_COLS:]
        gate_ref[...] = (gate * _sigmoid(gate)).astype(gate_ref.dtype)
        o = o[:, :SHIFT_COLS]
        first = pl.program_id(0) % tiles_per_seq == 0
        last = jnp.where(first, 0.0, prev_ref[SUBLANES - 1:SUBLANES, :])
        row = lax.broadcasted_iota(jnp.int32, (tm, 1), 0)
        oprev = jnp.where(row == 0, last, pltpu.roll(o, 1, 0))
        prev_ref[...] = o[tm - SUBLANES:, :]
        o_ref[...] = o + (oprev - o) * mu_ref[...]
    elif epilogue == "qkv_gate":
        qkv_ref, gate_ref = rest
        qkv_ref[...] = o[:, :3 * ATT_W].reshape(qkv_ref.shape)
        gate = o[:, 3 * ATT_W:]
        gate_ref[...] = (gate * _sigmoid(gate)).astype(gate_ref.dtype).reshape(gate_ref.shape)


def _rwkv_projection(x2, norm_w, w_bf16, seq, mu):
    tokens = x2.shape[0]
    tm = min(PROJ_TM, seq)
    once = pl.Buffered(1)
    return pl.pallas_call(
        functools.partial(_proj_kernel, epilogue="shift_gate", tiles_per_seq=seq // tm),
        grid=(tokens // tm,),
        in_specs=[pl.BlockSpec((tm, D_MODEL), lambda i: (i, 0)),
                  pl.BlockSpec((1, D_MODEL), lambda i: (0, 0), pipeline_mode=once),
                  pl.BlockSpec(w_bf16.shape, lambda i: (0, 0), pipeline_mode=once),
                  pl.BlockSpec((1, SHIFT_COLS), lambda i: (0, 0), pipeline_mode=once)],
        out_specs=[pl.BlockSpec((tm, SHIFT_COLS), lambda i: (i, 0)),
                   pl.BlockSpec((tm, RWKV_W), lambda i: (i, 0))],
        out_shape=[jax.ShapeDtypeStruct((tokens, SHIFT_COLS), F32),
                   jax.ShapeDtypeStruct((tokens, RWKV_W), BF16)],
        scratch_shapes=[pltpu.VMEM((SUBLANES, SHIFT_COLS), F32)],
        compiler_params=pltpu.CompilerParams(
            dimension_semantics=("arbitrary",), vmem_limit_bytes=VMEM_LIMIT),
        name="in_projection_rwkv",
    )(x2, norm_w.reshape(1, D_MODEL), w_bf16, mu.reshape(1, SHIFT_COLS).astype(F32))


def _att_projection(x2, norm_w, w_bf16, batch, seq):
    tm = min(PROJ_TM, seq)
    tiles = seq // tm
    rows = tm // RESIDUES
    out_block = lambda width: pl.BlockSpec((None, RESIDUES, rows, width),
                                           lambda i: (i // tiles, 0, i % tiles, 0))
    return pl.pallas_call(
        functools.partial(_proj_kernel, epilogue="qkv_gate", tiles_per_seq=tiles),
        grid=(batch * tiles,),
        in_specs=[pl.BlockSpec((tm, LANES), functools.partial(lambda c, i: (i, c), c))
                  for c in range(D_MODEL // LANES)]
                 + [pl.BlockSpec((1, D_MODEL), lambda i: (0, 0)),
                    pl.BlockSpec(w_bf16.shape, lambda i: (0, 0))],
        out_specs=[out_block(3 * ATT_W), out_block(ATT_W)],
        out_shape=[jax.ShapeDtypeStruct((batch, RESIDUES, seq // RESIDUES, 3 * ATT_W), F32),
                   jax.ShapeDtypeStruct((batch, RESIDUES, seq // RESIDUES, ATT_W), BF16)],
        scratch_shapes=[pltpu.VMEM((D_MODEL // LANES, tm, LANES), F32)],
        compiler_params=pltpu.CompilerParams(
            dimension_semantics=("arbitrary",), vmem_limit_bytes=VMEM_LIMIT),
        name="in_projection_attention",
    )(*[x2] * (D_MODEL // LANES), norm_w.reshape(1, D_MODEL), w_bf16)


OPERANDS = 6
STEP_CHUNKS = 2
STEP_ROWS = STEP_CHUNKS * CHUNK


def _rwkv_kernel(p_ref, gate_ref, w0_ref, a0_ref, kk_ref, ka_ref, rk_ref, lnw_ref, lnb_ref,
                 wu3_ref, au3_ref, y_ref, state_ref, ops_ref, rt_ref, gend_ref, bonus_ref,
                 *, steps_per_seq):
    t = pl.program_id(0)

    @pl.when(t == 0)
    def _():
        ops_ref[...] = jnp.zeros_like(ops_ref)
        rt_ref[...] = jnp.zeros_like(rt_ref)
        gend_ref[...] = jnp.zeros_like(gend_ref)
        bonus_ref[...] = jnp.zeros_like(bonus_ref)
        state_ref[...] = jnp.zeros_like(state_ref)

    fill, use = t % 2, (t + 1) % 2
    row = lax.broadcasted_iota(jnp.int32, (CHUNK, LANES), 0)
    lane = lax.broadcasted_iota(jnp.int32, (CHUNK, LANES), 1)
    head0 = lane < HEAD_DIM
    tcol = lane % CHUNK
    strict, incl = tcol < row, tcol <= row
    same16 = (tcol // 16) == (row // 16)
    same32 = (tcol // 32) == (row // 32)
    eye = jnp.where(tcol == row, 1.0, 0.0).astype(F32)
    bi = lax.broadcasted_iota(jnp.int32, (LANES, LANES), 0)
    bj = lax.broadcasted_iota(jnp.int32, (LANES, LANES), 1)
    block_diag = (bi // HEAD_DIM) == (bj // HEAD_DIM)
    head_ones = jnp.where(block_diag, 1.0, 0.0).astype(BF16)
    no_ones = jnp.zeros_like(head_ones)
    head_ones_diag = jnp.concatenate([jnp.concatenate([head_ones, no_ones], axis=1),
                                      jnp.concatenate([no_ones, head_ones], axis=1)], axis=0)
    ti = lax.broadcasted_iota(jnp.int32, (STEP_ROWS, STEP_ROWS), 0)
    tj = lax.broadcasted_iota(jnp.int32, (STEP_ROWS, STEP_ROWS), 1)
    tri = jnp.where((tj <= ti) & (tj // CHUNK == ti // CHUNK), 1.0, 0.0).astype(BF16)
    pairs = range(PAIRS)
    chunks = range(STEP_CHUNKS)
    units = [(c, g) for c in chunks for g in pairs]
    cols = lambda c0, g: slice(c0 + g * LANES, c0 + (g + 1) * LANES)
    rows = lambda c: slice(c * CHUNK, (c + 1) * CHUNK)

    def stack(x):
        x = x.astype(BF16)
        zero = jnp.zeros_like(x)
        return jnp.concatenate([jnp.where(head0, x, zero), jnp.where(head0, zero, x)], axis=0)

    def chain():
        ops = lambda i, c, g: ops_ref[use, i, rows(c), cols(0, g)]
        a_t, b_t, k_t, v, b_h, k_h = ([ops(i, c, g) for c, g in units] for i in range(OPERANDS))
        r_t = [rt_ref[use, rows(c), cols(0, g)] for c, g in units]
        n = range(len(units))
        v_s = [stack(x) for x in v]
        gram = [_mm_nt(jnp.concatenate([a_t[u], r_t[u]], axis=0),
                       jnp.concatenate([stack(b_t[u]), stack(k_t[u])], axis=0)) for u in n]
        yield
        a_ab = [jnp.where(strict, x[:CHUNK, :LANES], 0.0) for x in gram]
        a_ak = [jnp.where(strict, x[:CHUNK, LANES:], 0.0) for x in gram]
        a_rb = [jnp.where(incl, x[CHUNK:, :LANES], 0.0) for x in gram]
        a_rk = [jnp.where(incl, x[CHUNK:, LANES:], 0.0) for x in gram]
        nil = [jnp.where(same16, x, 0.0) for x in a_ab]
        inv = [eye + x for x in nil]
        nil = [_mm(x, stack(x)) for x in nil]
        yield
        av = [_mm(jnp.concatenate([a_ak[u], a_rk[u]], axis=0), v_s[u]) for u in n]
        yield
        for _ in range(2):
            both = [_mm(jnp.concatenate([nil[u], inv[u]], axis=0), stack(nil[u])) for u in n]
            nil = [x[:CHUNK] for x in both]
            inv = [inv[u] + both[u][CHUNK:] for u in n]
            yield
        vk = [_mm_tn(v[u], k_h[u]) for u in n]
        yield
        inv = [inv[u] + _mm(inv[u], stack(nil[u])) for u in n]
        yield
        for off_diag in (same32 & ~same16, ~same32):
            low = [_mm(inv[u], stack(jnp.where(off_diag, a_ab[u], 0.0))) for u in n]
            yield
            inv = [inv[u] + _mm(low[u], stack(inv[u])) for u in n]
            yield
        tx = [_mm(inv[u], jnp.concatenate([stack(a_t[u]), stack(av[u][:CHUNK])], axis=1))
              for u in n]
        yield
        zz = [_mm(a_rb[u], jnp.concatenate([stack(tx[u][:, :LANES]), stack(tx[u][:, LANES:])], axis=1))
              for u in n]
        yield
        r_p = [r_t[u] + zz[u][:, :LANES] for u in n]
        y0 = [zz[u][:, LANES:] + av[u][CHUNK:] for u in n]
        mn = [_mm_tn(tx[u], b_h[u]) for u in n]
        yield
        fresh = lax.rem(t - 1, steps_per_seq) == 0
        state = [jnp.where(fresh, 0.0, state_ref[g]) for g in pairs]
        y = []
        for u, (c, g) in enumerate(units):
            y.append(_mm_nt(r_p[u], state[g]) + y0[u])
            m_t = jnp.where(block_diag, mn[u][:LANES], 0.0)
            n_t = jnp.where(block_diag, mn[u][LANES:] + vk[u], 0.0)
            g_end = gend_ref[use, c * SUBLANES:c * SUBLANES + 1, cols(0, g)]
            state[g] = state[g] * g_end + _mm(state[g], m_t) + n_t
            if g == PAIRS - 1:
                yield
        for g in pairs:
            state_ref[g] = state[g]
        y_all = jnp.concatenate(y, axis=0)
        moments = jnp.dot(jnp.concatenate([y_all, y_all * y_all], axis=1).astype(BF16), head_ones_diag,
                          preferred_element_type=F32) * (1.0 / HEAD_DIM)
        for u, (c, g) in enumerate(units):
            mean = moments[u * CHUNK:(u + 1) * CHUNK, :LANES]
            var = moments[u * CHUNK:(u + 1) * CHUNK, LANES:] - mean * mean
            yn = (y[u] - mean) * lax.rsqrt(var + GN_EPS)
            yn = yn * lnw_ref[:, cols(0, g)] + lnb_ref[:, cols(0, g)]
            y_ref[rows(c), cols(0, g)] = ((yn + bonus_ref[use, rows(c), cols(0, g)])
                                          * gate_ref[rows(c), cols(0, g)].astype(F32)).astype(y_ref.dtype)

    def prepare():
        lo = p_ref[:, LORA_COL0:LORA_COL0 + LANES]
        z_all = -(w0_ref[...] + _mm_x3(jnp.tanh(lo[:, :LORA_RANK]), wu3_ref[...]))
        a_all = a0_ref[...] + _mm_x3(lo[:, LORA_RANK:], au3_ref[...])
        yield
        for g in pairs:
            c = cols(0, g)
            z = z_all[:, c]
            w = -(jnp.maximum(z, 0.0) + jnp.log(1.0 + jnp.exp(-jnp.abs(z)))) - 0.5
            lw = -jnp.exp(w)
            lg = _mm_split_rhs(tri, lw)
            ends = [lg[(i + 1) * CHUNK - 1:(i + 1) * CHUNK, :] for i in chunks]
            lg_end = jnp.concatenate([jnp.broadcast_to(e, (CHUNK, LANES)) for e in ends], axis=0)
            for i in chunks:
                gend_ref[fill, i * SUBLANES:(i + 1) * SUBLANES, c] = jnp.broadcast_to(
                    jnp.exp(ends[i]), (SUBLANES, LANES))
            e_inv = jnp.exp(-lg)
            e_end = jnp.exp(lg_end - lg)
            a = _sigmoid(a_all[:, c])
            r, k, v = p_ref[:, c], p_ref[:, cols(K_COL0, g)], p_ref[:, cols(V_COL0, g)]
            kk = k * kk_ref[:, c]
            k2 = k * (1.0 + (a - 1.0) * ka_ref[:, c])
            sums = jnp.dot(jnp.concatenate([kk * kk, r * k2 * rk_ref[:, c]], axis=1).astype(BF16),
                           head_ones_diag, preferred_element_type=F32)
            kk = kk / jnp.maximum(jnp.sqrt(sums[:, :LANES]), 1e-12)
            bonus_ref[fill, :, c] = sums[:, LANES:] * v
            rt_ref[fill, :, c] = r * jnp.exp(lg)
            ops_ref[fill, 0, :, c] = (-kk * jnp.exp(lg - lw)).astype(BF16)
            ops_ref[fill, 1, :, c] = (kk * a * e_inv).astype(BF16)
            ops_ref[fill, 2, :, c] = (k2 * e_inv).astype(BF16)
            ops_ref[fill, 3, :, c] = v.astype(BF16)
            ops_ref[fill, 4, :, c] = (kk * a * e_end).astype(BF16)
            ops_ref[fill, 5, :, c] = (k2 * e_end).astype(BF16)
            yield

    done = object()
    streams = [chain(), prepare()]
    while streams:
        streams = [s for s in streams if next(s, done) is not done]


def _rwkv_time_mix(p_shift, p_gates, batch, seq, w0, w_up, a0, a_up, k_k, k_a, r_k, ln_x_w, ln_x_b):
    nt = batch * seq // STEP_ROWS
    row2 = lambda v: v.reshape(1, -1).astype(F32)
    whole = lambda shape: pl.BlockSpec(shape, lambda t: (0, 0))
    chan = whole((1, RWKV_W))
    lora = whole((3 * LORA_RANK, RWKV_W))
    prepared = lambda t: (jnp.minimum(t, nt - 1), 0)
    chained = lambda t: (jnp.maximum(t - 1, 0), 0)
    return pl.pallas_call(
        functools.partial(_rwkv_kernel, steps_per_seq=seq // STEP_ROWS),
        grid=(nt + 1,),
        in_specs=[pl.BlockSpec((STEP_ROWS, SHIFT_COLS), prepared),
                  pl.BlockSpec((STEP_ROWS, RWKV_W), chained),
                  chan, chan, chan, chan, chan, chan, chan, lora, lora],
        out_specs=pl.BlockSpec((STEP_ROWS, RWKV_W), chained),
        out_shape=jax.ShapeDtypeStruct((batch * seq, RWKV_W), BF16),
        scratch_shapes=[pltpu.VMEM((PAIRS, LANES, LANES), F32),
                        pltpu.VMEM((2, OPERANDS, STEP_ROWS, RWKV_W), BF16),
                        pltpu.VMEM((2, STEP_ROWS, RWKV_W), F32),
                        pltpu.VMEM((2, STEP_CHUNKS * SUBLANES, RWKV_W), F32),
                        pltpu.VMEM((2, STEP_ROWS, RWKV_W), F32)],
        compiler_params=pltpu.CompilerParams(
            dimension_semantics=("arbitrary",), vmem_limit_bytes=VMEM_LIMIT),
        name="rwkv7_time_mix",
    )(p_shift, p_gates, row2(w0), row2(a0), row2(k_k), row2(k_a), row2(r_k), row2(ln_x_w), row2(ln_x_b),
      _stack_x3(w_up), _stack_x3(a_up))


def _attn_kernel(q_ref, k_ref, v_ref, gate_ref, y_ref, acc_scr, max_scr, sum_scr, *, seq):
    n_blocks = seq // ATT_BLK
    qi = lax.broadcasted_iota(jnp.int32, (2 * ATT_BLK, ATT_BLK), 0) % ATT_BLK
    kj = lax.broadcasted_iota(jnp.int32, (2 * ATT_BLK, ATT_BLK), 1)
    head0 = lax.broadcasted_iota(jnp.int32, (ATT_BLK, LANES), 1) < HEAD_DIM
    ones = jnp.ones((ATT_BLK, LANES), BF16)
    scale = HEAD_DIM ** -0.5 * LOG2_E
    neg = -1e30

    def attend(pi, d, kv_blocks, blocks):
        pieces = RESIDUES // d
        length = ATT_BLK // pieces
        pos = lambda i: pieces * (i % length) + i // length
        ok_prev, ok_cur = pos(kj) >= pos(qi), pos(kj) <= pos(qi)

        def at(res, blk, j):
            return j * d + res, pl.ds(pl.multiple_of(blk * length, length), length)

        def load(ref, res, blk):
            parts = []
            for j in range(pieces):
                r, rows = at(res, blk, j)
                parts.append(ref[r, rows, :])
            return jnp.concatenate(parts, axis=0) if pieces > 1 else parts[0]

        def store(scr, res, blk, val):
            for j in range(pieces):
                r, rows = at(res, blk, j)
                scr[pi, r, rows, :] = val[j * length:(j + 1) * length]

        kb = [load(k_ref, *b).astype(BF16) for b in kv_blocks]
        vb = [jnp.concatenate([load(v_ref, *b).astype(BF16), ones], axis=1) for b in kv_blocks]
        qs = []
        for cur, _, _ in blocks:
            q = load(q_ref, *kv_blocks[cur]) * scale
            qs.append(jnp.concatenate([jnp.where(head0, q, 0.0), jnp.where(head0, 0.0, q)],
                                      axis=0).astype(BF16))
        scores = []
        for (cur, prev, prev_ok), q in zip(blocks, qs):
            if prev is None:
                scores.append(jnp.where(ok_cur, _mm_nt(q, kb[cur]), neg))
            else:
                s = _mm_nt(q, jnp.concatenate([kb[prev], kb[cur]], axis=0))
                okp = ok_prev if prev_ok is None else ok_prev & prev_ok
                scores.append(jnp.concatenate([jnp.where(okp, s[:, :ATT_BLK], neg),
                                               jnp.where(ok_cur, s[:, ATT_BLK:], neg)], axis=1))
        mx = [jnp.max(s, axis=-1, keepdims=True) for s in scores]
        probs = [jnp.exp2(s - m).astype(BF16) for s, m in zip(scores, mx)]
        pv = [_mm(p, vb[cur] if prev is None else jnp.concatenate([vb[prev], vb[cur]], axis=0))
              for (cur, prev, _), p in zip(blocks, probs)]
        for (cur, _, _), m, x in zip(blocks, mx, pv):
            res, blk = kv_blocks[cur]
            store(acc_scr, res, blk, jnp.where(head0, x[:ATT_BLK, :LANES], x[ATT_BLK:, :LANES]))
            store(sum_scr, res, blk, jnp.where(head0, x[:ATT_BLK, LANES:], x[ATT_BLK:, LANES:]))
            store(max_scr, res, blk, jnp.where(head0, m[:ATT_BLK], m[ATT_BLK:]))

    for pi, d in enumerate(DILATIONS):
        nb = n_blocks // d
        if nb <= ATT_GROUP:
            n_res = ATT_GROUP // nb
            def body(it, carry, pi=pi, d=d, nb=nb, n_res=n_res):
                kv = [(it * n_res + r, u) for r in range(n_res) for u in range(nb)]
                blocks = [(r * nb + u, r * nb + u - 1 if u else None, None)
                          for r in range(n_res) for u in range(nb)]
                attend(pi, d, kv, blocks)
                return carry
            trips = d // n_res
        else:
            per_res = nb // ATT_GROUP
            def body(it, carry, pi=pi, d=d, per_res=per_res):
                res, blk0 = it // per_res, (it % per_res) * ATT_GROUP
                kv = [(res, jnp.maximum(blk0 - 1, 0))] + [(res, blk0 + u) for u in range(ATT_GROUP)]
                attend(pi, d, kv, [(1, 0, blk0 > 0)] + [(u + 1, u, None) for u in range(1, ATT_GROUP)])
                return carry
            trips = d * per_res
        lax.fori_loop(0, trips, body, 0)

    m = jnp.maximum(jnp.maximum(max_scr[0], max_scr[1]), max_scr[2])
    num = den = None
    for pi in range(len(DILATIONS)):
        w = jnp.exp2(max_scr[pi] - m)
        num = w * acc_scr[pi] if num is None else num + w * acc_scr[pi]
        den = w * sum_scr[pi] if den is None else den + w * sum_scr[pi]
    y = num / den * gate_ref[...].astype(F32)
    for r in range(RESIDUES):
        y_ref[pl.ds(r, seq // RESIDUES, stride=RESIDUES), :] = y[r]


def _dilated_attention(p_att, p_gate, batch, seq):
    per_res = seq // RESIDUES

    def tile(blk0):
        return pl.BlockSpec((RESIDUES, per_res, LANES), lambda b, g: (b, 0, blk0 + g))

    return pl.pallas_call(
        functools.partial(_attn_kernel, seq=seq),
        grid=(batch, PAIRS),
        in_specs=[tile(0), tile(PAIRS), tile(2 * PAIRS), tile(0)],
        out_specs=pl.BlockSpec((seq, LANES), lambda b, g: (b, g)),
        out_shape=jax.ShapeDtypeStruct((batch * seq, ATT_W), F32),
        scratch_shapes=[pltpu.VMEM((len(DILATIONS), RESIDUES, per_res, LANES), F32)] * 3,
        compiler_params=pltpu.CompilerParams(
            dimension_semantics=("arbitrary", "arbitrary"), vmem_limit_bytes=VMEM_LIMIT),
        name="dilated_attention",
    )(p_att, p_att, p_att, p_gate)


def _out_kernel(yr_ref, ya_ref, x_ref, wr_ref, wa_ref, fw_ref, o_ref):
    h = (x_ref[...]
         + jnp.dot(yr_ref[...], wr_ref[...], preferred_element_type=F32)
         + jnp.dot(ya_ref[...].astype(BF16), wa_ref[...], preferred_element_type=F32))
    ms = jnp.mean(h * h, axis=-1, keepdims=True)
    o_ref[...] = h * lax.rsqrt(ms + RMS_EPS) * fw_ref[...]


def _out_projection(y_rwkv, y_att, x2, w_out_bf16, final_norm_w, tm):
    tokens = x2.shape[0]
    tok = lambda width: pl.BlockSpec((tm, width), lambda i: (i, 0))
    return pl.pallas_call(
        _out_kernel,
        grid=(tokens // tm,),
        in_specs=[
            tok(RWKV_W), tok(ATT_W), tok(D_MODEL),
            pl.BlockSpec((RWKV_W, D_MODEL), lambda i: (0, 0)),
            pl.BlockSpec((ATT_W, D_MODEL), lambda i: (1, 0)),
            pl.BlockSpec((1, D_MODEL), lambda i: (0, 0)),
        ],
        out_specs=tok(D_MODEL),
        out_shape=jax.ShapeDtypeStruct((tokens, D_MODEL), F32),
        compiler_params=pltpu.CompilerParams(
            dimension_semantics=("arbitrary",), vmem_limit_bytes=VMEM_LIMIT),
        name="out_projection",
    )(y_rwkv, y_att, x2, w_out_bf16, w_out_bf16, final_norm_w.reshape(1, D_MODEL))


def kernel(x, norm_w, w_in, mu_shift, w0, w_up, a0, a_up, k_k, k_a, r_k, ln_x_w, ln_x_b, w_out,
           final_norm_w):
    batch, seq, _ = x.shape
    assert seq % (ATT_BLK * max(DILATIONS)) == 0 and seq % STEP_ROWS == 0
    x2 = x.astype(F32).reshape(batch * seq, D_MODEL)
    nw = norm_w.astype(F32)
    p_shift, gate_rwkv = _rwkv_projection(x2, nw, w_in[:, :ATT0].astype(BF16), seq, mu_shift)
    p_att, gate_att = _att_projection(x2, nw, w_in[:, ATT0:].astype(BF16), batch, seq)
    y_rwkv = _rwkv_time_mix(p_shift, gate_rwkv, batch, seq, w0, w_up, a0, a_up, k_k, k_a, r_k,
                            ln_x_w, ln_x_b)
    per_res = seq // RESIDUES
    y_att = _dilated_attention(p_att.reshape(batch * RESIDUES, per_res, 3 * ATT_W),
                               gate_att.reshape(batch * RESIDUES, per_res, ATT_W), batch, seq)
    out = _out_projection(y_rwkv, y_att, x2, w_out.astype(BF16), final_norm_w.astype(F32),
                          tm=min(PROJ_TM, batch * seq))
    return out.reshape(batch, seq, D_MODEL).astype(x.dtype)
```

```python
import functools

import jax
import jax.numpy as jnp
from jax import lax
from jax.experimental import pallas as pl
from jax.experimental.pallas import tpu as pltpu

D_MODEL = 1024
HEAD_DIM = 64
RWKV_W = 1024
ATT_W = 1024
MIX_W = RWKV_W + ATT_W
LORA_RANK = 64
SHIFT_COLS = 3 * RWKV_W + 2 * LORA_RANK
RWKV_GATE0 = SHIFT_COLS
ATT0 = RWKV_GATE0 + RWKV_W
ATT_GATE0 = ATT0 + 3 * ATT_W
DILATIONS = (1, 4, 16)
ATT_BLK = 128
RESIDUES = max(DILATIONS)
ATT_GROUP = 8
LOG2_E = 1.4426950408889634
RMS_EPS = 1e-5
GN_EPS = 64e-5

LANES = 128
SUBLANES = 8
PAIRS = RWKV_W // LANES
CHUNK = 64
K_COL0, V_COL0, LORA_COL0 = RWKV_W, 2 * RWKV_W, 3 * RWKV_W
PROJ_TM = 512

F32 = jnp.float32
BF16 = jnp.bfloat16
VMEM_LIMIT = 56 * 1024 * 1024

NT_DIMS = (((1,), (1,)), ((), ()))
TN_DIMS = (((0,), (0,)), ((), ()))


def _mm(a, b):
    return jnp.dot(a.astype(BF16), b.astype(BF16), preferred_element_type=F32)


def _mm_nt(a, b):
    return lax.dot_general(a.astype(BF16), b.astype(BF16), NT_DIMS, preferred_element_type=F32)


def _mm_tn(a, b):
    return lax.dot_general(a.astype(BF16), b.astype(BF16), TN_DIMS, preferred_element_type=F32)


def _split(x):
    hi = x.astype(BF16)
    return hi, (x - hi.astype(F32)).astype(BF16)


def _mm_split_rhs(w_bf16, x):
    n = x.shape[1]
    both = jnp.dot(w_bf16, jnp.concatenate(_split(x), axis=1), preferred_element_type=F32)
    return both[:, :n] + both[:, n:]


def _mm_x3(x, w3):
    hi, lo = _split(x)
    return jnp.dot(jnp.concatenate([hi, lo, hi], axis=1), w3, preferred_element_type=F32)


def _stack_x3(w):
    hi, lo = _split(w.astype(F32))
    return jnp.concatenate([hi, hi, lo], axis=0)


def _sigmoid(x):
    return 1.0 / (1.0 + jnp.exp(-x))


REGROUP_STRIDE = 4


def _regrouped_rows(x_refs, tmp_ref):
    tm = x_refs[0].shape[0]
    quarter, per_res = tm // REGROUP_STRIDE, tm // RESIDUES
    slabs = []
    for c, ref in enumerate(x_refs):
        for g in range(REGROUP_STRIDE):
            tmp_ref[c, g * quarter:(g + 1) * quarter, :] = ref[pl.ds(g, quarter, stride=REGROUP_STRIDE), :]
        slabs.append(jnp.concatenate(
            [tmp_ref[c, pl.ds((r % REGROUP_STRIDE) * quarter + r // REGROUP_STRIDE, per_res,
                              stride=REGROUP_STRIDE), :] for r in range(RESIDUES)], axis=0))
    return jnp.concatenate(slabs, axis=1)


def _proj_kernel(*refs, epilogue, tiles_per_seq):
    if epilogue == "qkv_gate":
        slabs = D_MODEL // LANES
        nw_ref, w_ref, *rest, tmp_ref = refs[slabs:]
        x = _regrouped_rows(refs[:slabs], tmp_ref)
    else:
        x_ref, nw_ref, w_ref, *rest = refs
        x = x_ref[...]
    ms = jnp.mean(x * x, axis=-1, keepdims=True)
    xn = (x * lax.rsqrt(ms + RMS_EPS) * nw_ref[...]).astype(BF16)
    o = jnp.dot(xn, w_ref[...], preferred_element_type=F32)
    if epilogue == "shift_gate":
        mu_ref, o_ref, gate_ref, prev_ref = rest
        tm = o.shape[0]
        gate = o[:, SHIFT_COLS:]
        gate_ref[...] = (gate * _sigmoid(gate)).astype(gate_ref.dtype)
        o = o[:, :SHIFT_COLS]
        first = pl.program_id(0) % tiles_per_seq == 0
        last = jnp.where(first, 0.0, prev_ref[SUBLANES - 1:SUBLANES, :])
        row = lax.broadcasted_iota(jnp.int32, (tm, 1), 0)
        oprev = jnp.where(row == 0, last, pltpu.roll(o, 1, 0))
        prev_ref[...] = o[tm - SUBLANES:, :]
        o_ref[...] = o + (oprev - o) * mu_ref[...]
    elif epilogue == "qkv_gate":
        qkv_ref, gate_ref = rest
        qkv_ref[...] = o[:, :3 * ATT_W].reshape(qkv_ref.shape)
        gate = o[:, 3 * ATT_W:]
        gate_ref[...] = (gate * _sigmoid(gate)).astype(gate_ref.dtype).reshape(gate_ref.shape)


def _rwkv_projection(x2, norm_w, w_bf16, seq, mu):
    tokens = x2.shape[0]
    tm = min(PROJ_TM, seq)
    once = pl.Buffered(1)
    return pl.pallas_call(
        functools.partial(_proj_kernel, epilogue="shift_gate", tiles_per_seq=seq // tm),
        grid=(tokens // tm,),
        in_specs=[pl.BlockSpec((tm, D_MODEL), lambda i: (i, 0)),
                  pl.BlockSpec((1, D_MODEL), lambda i: (0, 0), pipeline_mode=once),
                  pl.BlockSpec(w_bf16.shape, lambda i: (0, 0), pipeline_mode=once),
                  pl.BlockSpec((1, SHIFT_COLS), lambda i: (0, 0), pipeline_mode=once)],
        out_specs=[pl.BlockSpec((tm, SHIFT_COLS), lambda i: (i, 0)),
                   pl.BlockSpec((tm, RWKV_W), lambda i: (i, 0))],
        out_shape=[jax.ShapeDtypeStruct((tokens, SHIFT_COLS), F32),
                   jax.ShapeDtypeStruct((tokens, RWKV_W), BF16)],
        scratch_shapes=[pltpu.VMEM((SUBLANES, SHIFT_COLS), F32)],
        compiler_params=pltpu.CompilerParams(
            dimension_semantics=("arbitrary",), vmem_limit_bytes=VMEM_LIMIT),
        name="in_projection_rwkv",
    )(x2, norm_w.reshape(1, D_MODEL), w_bf16, mu.reshape(1, SHIFT_COLS).astype(F32))


def _att_projection(x2, norm_w, w_bf16, batch, seq):
    tm = min(PROJ_TM, seq)
    tiles = seq // tm
    rows = tm // RESIDUES
    out_block = lambda width: pl.BlockSpec((None, RESIDUES, rows, width),
                                           lambda i: (i // tiles, 0, i % tiles, 0))
    return pl.pallas_call(
        functools.partial(_proj_kernel, epilogue="qkv_gate", tiles_per_seq=tiles),
        grid=(batch * tiles,),
        in_specs=[pl.BlockSpec((tm, LANES), functools.partial(lambda c, i: (i, c), c))
                  for c in range(D_MODEL // LANES)]
                 + [pl.BlockSpec((1, D_MODEL), lambda i: (0, 0)),
                    pl.BlockSpec(w_bf16.shape, lambda i: (0, 0))],
        out_specs=[out_block(3 * ATT_W), out_block(ATT_W)],
        out_shape=[jax.ShapeDtypeStruct((batch, RESIDUES, seq // RESIDUES, 3 * ATT_W), F32),
                   jax.ShapeDtypeStruct((batch, RESIDUES, seq // RESIDUES, ATT_W), BF16)],
        scratch_shapes=[pltpu.VMEM((D_MODEL // LANES, tm, LANES), F32)],
        compiler_params=pltpu.CompilerParams(
            dimension_semantics=("arbitrary",), vmem_limit_bytes=VMEM_LIMIT),
        name="in_projection_attention",
    )(*[x2] * (D_MODEL // LANES), norm_w.reshape(1, D_MODEL), w_bf16)


OPERANDS = 6
STEP_CHUNKS = 2
STEP_ROWS = STEP_CHUNKS * CHUNK


def _rwkv_kernel(p_ref, gate_ref, w0_ref, a0_ref, kk_ref, ka_ref, rk_ref, lnw_ref, lnb_ref,
                 wu3_ref, au3_ref, y_ref, state_ref, ops_ref, rt_ref, gend_ref, bonus_ref,
                 *, steps_per_seq):
    t = pl.program_id(0)

    @pl.when(t == 0)
    def _():
        ops_ref[...] = jnp.zeros_like(ops_ref)
        rt_ref[...] = jnp.zeros_like(rt_ref)
        gend_ref[...] = jnp.zeros_like(gend_ref)
        bonus_ref[...] = jnp.zeros_like(bonus_ref)
        state_ref[...] = jnp.zeros_like(state_ref)

    fill, use = t % 2, (t + 1) % 2
    row = lax.broadcasted_iota(jnp.int32, (CHUNK, LANES), 0)
    lane = lax.broadcasted_iota(jnp.int32, (CHUNK, LANES), 1)
    head0 = lane < HEAD_DIM
    tcol = lane % CHUNK
    strict, incl = tcol < row, tcol <= row
    same16 = (tcol // 16) == (row // 16)
    same32 = (tcol // 32) == (row // 32)
    eye = jnp.where(tcol == row, 1.0, 0.0).astype(F32)
    bi = lax.broadcasted_iota(jnp.int32, (LANES, LANES), 0)
    bj = lax.broadcasted_iota(jnp.int32, (LANES, LANES), 1)
    block_diag = (bi // HEAD_DIM) == (bj // HEAD_DIM)
    head_ones = jnp.where(block_diag, 1.0, 0.0).astype(BF16)
    no_ones = jnp.zeros_like(head_ones)
    head_ones_diag = jnp.concatenate([jnp.concatenate([head_ones, no_ones], axis=1),
                                      jnp.concatenate([no_ones, head_ones], axis=1)], axis=0)
    ti = lax.broadcasted_iota(jnp.int32, (STEP_ROWS, STEP_ROWS), 0)
    tj = lax.broadcasted_iota(jnp.int32, (STEP_ROWS, STEP_ROWS), 1)
    tri = jnp.where((tj <= ti) & (tj // CHUNK == ti // CHUNK), 1.0, 0.0).astype(BF16)
    pairs = range(PAIRS)
    chunks = range(STEP_CHUNKS)
    units = [(c, g) for c in chunks for g in pairs]
    cols = lambda c0, g: slice(c0 + g * LANES, c0 + (g + 1) * LANES)
    rows = lambda c: slice(c * CHUNK, (c + 1) * CHUNK)

    def stack(x):
        x = x.astype(BF16)
        zero = jnp.zeros_like(x)
        return jnp.concatenate([jnp.where(head0, x, zero), jnp.where(head0, zero, x)], axis=0)

    def chain():
        ops = lambda i, c, g: ops_ref[use, i, rows(c), cols(0, g)]
        a_t, b_t, k_t, v, b_h, k_h = ([ops(i, c, g) for c, g in units] for i in range(OPERANDS))
        r_t = [rt_ref[use, rows(c), cols(0, g)] for c, g in units]
        n = range(len(units))
        v_s = [stack(x) for x in v]
        gram = [_mm_nt(jnp.concatenate([a_t[u], r_t[u]], axis=0),
                       jnp.concatenate([stack(b_t[u]), stack(k_t[u])], axis=0)) for u in n]
        yield
        a_ab = [jnp.where(strict, x[:CHUNK, :LANES], 0.0) for x in gram]
        a_ak = [jnp.where(strict, x[:CHUNK, LANES:], 0.0) for x in gram]
        a_rb = [jnp.where(incl, x[CHUNK:, :LANES], 0.0) for x in gram]
        a_rk = [jnp.where(incl, x[CHUNK:, LANES:], 0.0) for x in gram]
        nil = [jnp.where(same16, x, 0.0) for x in a_ab]
        inv = [eye + x for x in nil]
        nil = [_mm(x, stack(x)) for x in nil]
        yield
        av = [_mm(jnp.concatenate([a_ak[u], a_rk[u]], axis=0), v_s[u]) for u in n]
        yield
        for _ in range(2):
            both = [_mm(jnp.concatenate([nil[u], inv[u]], axis=0), stack(nil[u])) for u in n]
            nil = [x[:CHUNK] for x in both]
            inv = [inv[u] + both[u][CHUNK:] for u in n]
            yield
        vk = [_mm_tn(v[u], k_h[u]) for u in n]
        yield
        inv = [inv[u] + _mm(inv[u], stack(nil[u])) for u in n]
        yield
        for off_diag in (same32 & ~same16, ~same32):
            low = [_mm(inv[u], stack(jnp.where(off_diag, a_ab[u], 0.0))) for u in n]
            yield
            inv = [inv[u] + _mm(low[u], stack(inv[u])) for u in n]
            yield
        tx = [_mm(inv[u], jnp.concatenate([stack(a_t[u]), stack(av[u][:CHUNK])], axis=1))
              for u in n]
        yield
        zz = [_mm(a_rb[u], jnp.concatenate([stack(tx[u][:, :LANES]), stack(tx[u][:, LANES:])], axis=1))
              for u in n]
        yield
        r_p = [r_t[u] + zz[u][:, :LANES] for u in n]
        y0 = [zz[u][:, LANES:] + av[u][CHUNK:] for u in n]
        mn = [_mm_tn(tx[u], b_h[u]) for u in n]
        yield
        fresh = lax.rem(t - 1, steps_per_seq) == 0
        state = [jnp.where(fresh, 0.0, state_ref[g]) for g in pairs]
        y = []
        for u, (c, g) in enumerate(units):
            y.append(_mm_nt(r_p[u], state[g]) + y0[u])
            m_t = jnp.where(block_diag, mn[u][:LANES], 0.0)
            n_t = jnp.where(block_diag, mn[u][LANES:] + vk[u], 0.0)
            g_end = gend_ref[use, c * SUBLANES:c * SUBLANES + 1, cols(0, g)]
            state[g] = state[g] * g_end + _mm(state[g], m_t) + n_t
            if g == PAIRS - 1:
                yield
        for g in pairs:
            state_ref[g] = state[g]
        y_all = jnp.concatenate(y, axis=0)
        moments = jnp.dot(jnp.concatenate([y_all, y_all * y_all], axis=1).astype(BF16), head_ones_diag,
                          preferred_element_type=F32) * (1.0 / HEAD_DIM)
        for u, (c, g) in enumerate(units):
            mean = moments[u * CHUNK:(u + 1) * CHUNK, :LANES]
            var = moments[u * CHUNK:(u + 1) * CHUNK, LANES:] - mean * mean
            yn = (y[u] - mean) * lax.rsqrt(var + GN_EPS)
            yn = yn * lnw_ref[:, cols(0, g)] + lnb_ref[:, cols(0, g)]
            y_ref[rows(c), cols(0, g)] = ((yn + bonus_ref[use, rows(c), cols(0, g)])
                                          * gate_ref[rows(c), cols(0, g)].astype(F32)).astype(y_ref.dtype)

    def prepare():
        lo = p_ref[:, LORA_COL0:LORA_COL0 + LANES]
        z_all = -(w0_ref[...] + _mm_x3(jnp.tanh(lo[:, :LORA_RANK]), wu3_ref[...]))
        a_all = a0_ref[...] + _mm_x3(lo[:, LORA_RANK:], au3_ref[...])
        yield
        for g in pairs:
            c = cols(0, g)
            z = z_all[:, c]
            w = -(jnp.maximum(z, 0.0) + jnp.log(1.0 + jnp.exp(-jnp.abs(z)))) - 0.5
            lw = -jnp.exp(w)
            lg = _mm_split_rhs(tri, lw)
            ends = [lg[(i + 1) * CHUNK - 1:(i + 1) * CHUNK, :] for i in chunks]
            lg_end = jnp.concatenate([jnp.broadcast_to(e, (CHUNK, LANES)) for e in ends], axis=0)
            for i in chunks:
                gend_ref[fill, i * SUBLANES:(i + 1) * SUBLANES, c] = jnp.broadcast_to(
                    jnp.exp(ends[i]), (SUBLANES, LANES))
            e_inv = jnp.exp(-lg)
            e_end = jnp.exp(lg_end - lg)
            a = _sigmoid(a_all[:, c])
            r, k, v = p_ref[:, c], p_ref[:, cols(K_COL0, g)], p_ref[:, cols(V_COL0, g)]
            kk = k * kk_ref[:, c]
            k2 = k * (1.0 + (a - 1.0) * ka_ref[:, c])
            sums = jnp.dot(jnp.concatenate([kk * kk, r * k2 * rk_ref[:, c]], axis=1).astype(BF16),
                           head_ones_diag, preferred_element_type=F32)
            kk = kk / jnp.maximum(jnp.sqrt(sums[:, :LANES]), 1e-12)
            bonus_ref[fill, :, c] = sums[:, LANES:] * v
            rt_ref[fill, :, c] = r * jnp.exp(lg)
            ops_ref[fill, 0, :, c] = (-kk * jnp.exp(lg - lw)).astype(BF16)
            ops_ref[fill, 1, :, c] = (kk * a * e_inv).astype(BF16)
            ops_ref[fill, 2, :, c] = (k2 * e_inv).astype(BF16)
            ops_ref[fill, 3, :, c] = v.astype(BF16)
            ops_ref[fill, 4, :, c] = (kk * a * e_end).astype(BF16)
            ops_ref[fill, 5, :, c] = (k2 * e_end).astype(BF16)
            yield

    done = object()
    streams = [chain(), prepare()]
    while streams:
        streams = [s for s in streams if next(s, done) is not done]


def _rwkv_time_mix(p_shift, p_gates, batch, seq, w0, w_up, a0, a_up, k_k, k_a, r_k, ln_x_w, ln_x_b):
    nt = batch * seq // STEP_ROWS
    row2 = lambda v: v.reshape(1, -1).astype(F32)
    whole = lambda shape: pl.BlockSpec(shape, lambda t: (0, 0))
    chan = whole((1, RWKV_W))
    lora = whole((3 * LORA_RANK, RWKV_W))
    prepared = lambda t: (jnp.minimum(t, nt - 1), 0)
    chained = lambda t: (jnp.maximum(t - 1, 0), 0)
    return pl.pallas_call(
        functools.partial(_rwkv_kernel, steps_per_seq=seq // STEP_ROWS),
        grid=(nt + 1,),
        in_specs=[pl.BlockSpec((STEP_ROWS, SHIFT_COLS), prepared),
                  pl.BlockSpec((STEP_ROWS, RWKV_W), chained),
                  chan, chan, chan, chan, chan, chan, chan, lora, lora],
        out_specs=pl.BlockSpec((STEP_ROWS, RWKV_W), chained),
        out_shape=jax.ShapeDtypeStruct((batch * seq, RWKV_W), BF16),
        scratch_shapes=[pltpu.VMEM((PAIRS, LANES, LANES), F32),
                        pltpu.VMEM((2, OPERANDS, STEP_ROWS, RWKV_W), BF16),
                        pltpu.VMEM((2, STEP_ROWS, RWKV_W), F32),
                        pltpu.VMEM((2, STEP_CHUNKS * SUBLANES, RWKV_W), F32),
                        pltpu.VMEM((2, STEP_ROWS, RWKV_W), F32)],
        compiler_params=pltpu.CompilerParams(
            dimension_semantics=("arbitrary",), vmem_limit_bytes=VMEM_LIMIT),
        name="rwkv7_time_mix",
    )(p_shift, p_gates, row2(w0), row2(a0), row2(k_k), row2(k_a), row2(r_k), row2(ln_x_w), row2(ln_x_b),
      _stack_x3(w_up), _stack_x3(a_up))


def _attn_kernel(q_ref, k_ref, v_ref, gate_ref, y_ref, acc_scr, max_scr, sum_scr, *, seq):
    n_blocks = seq // ATT_BLK
    qi = lax.broadcasted_iota(jnp.int32, (2 * ATT_BLK, ATT_BLK), 0) % ATT_BLK
    kj = lax.broadcasted_iota(jnp.int32, (2 * ATT_BLK, ATT_BLK), 1)
    head0 = lax.broadcasted_iota(jnp.int32, (ATT_BLK, LANES), 1) < HEAD_DIM
    ones = jnp.ones((ATT_BLK, LANES), BF16)
    scale = HEAD_DIM ** -0.5 * LOG2_E
    neg = -1e30

    def attend(pi, d, kv_blocks, blocks):
        pieces = RESIDUES // d
        length = ATT_BLK // pieces
        pos = lambda i: pieces * (i % length) + i // length
        ok_prev, ok_cur = pos(kj) >= pos(qi), pos(kj) <= pos(qi)

        def at(res, blk, j):
            return j * d + res, pl.ds(pl.multiple_of(blk * length, length), length)

        def load(ref, res, blk):
            parts = []
            for j in range(pieces):
                r, rows = at(res, blk, j)
                parts.append(ref[r, rows, :])
            return jnp.concatenate(parts, axis=0) if pieces > 1 else parts[0]

        def store(scr, res, blk, val):
            for j in range(pieces):
                r, rows = at(res, blk, j)
                scr[pi, r, rows, :] = val[j * length:(j + 1) * length]

        kb = [load(k_ref, *b).astype(BF16) for b in kv_blocks]
        vb = [jnp.concatenate([load(v_ref, *b).astype(BF16), ones], axis=1) for b in kv_blocks]
        qs = []
        for cur, _, _ in blocks:
            q = load(q_ref, *kv_blocks[cur]) * scale
            qs.append(jnp.concatenate([jnp.where(head0, q, 0.0), jnp.where(head0, 0.0, q)],
                                      axis=0).astype(BF16))
        scores = []
        for (cur, prev, prev_ok), q in zip(blocks, qs):
            if prev is None:
                scores.append(jnp.where(ok_cur, _mm_nt(q, kb[cur]), neg))
            else:
                s = _mm_nt(q, jnp.concatenate([kb[prev], kb[cur]], axis=0))
                okp = ok_prev if prev_ok is None else ok_prev & prev_ok
                scores.append(jnp.concatenate([jnp.where(okp, s[:, :ATT_BLK], neg),
                                               jnp.where(ok_cur, s[:, ATT_BLK:], neg)], axis=1))
        mx = [jnp.max(s, axis=-1, keepdims=True) for s in scores]
        probs = [jnp.exp2(s - m).astype(BF16) for s, m in zip(scores, mx)]
        pv = [_mm(p, vb[cur] if prev is None else jnp.concatenate([vb[prev], vb[cur]], axis=0))
              for (cur, prev, _), p in zip(blocks, probs)]
        for (cur, _, _), m, x in zip(blocks, mx, pv):
            res, blk = kv_blocks[cur]
            store(acc_scr, res, blk, jnp.where(head0, x[:ATT_BLK, :LANES], x[ATT_BLK:, :LANES]))
            store(sum_scr, res, blk, jnp.where(head0, x[:ATT_BLK, LANES:], x[ATT_BLK:, LANES:]))
            store(max_scr, res, blk, jnp.where(head0, m[:ATT_BLK], m[ATT_BLK:]))

    for pi, d in enumerate(DILATIONS):
        nb = n_blocks // d
        if nb <= ATT_GROUP:
            n_res = ATT_GROUP // nb
            def body(it, carry, pi=pi, d=d, nb=nb, n_res=n_res):
                kv = [(it * n_res + r, u) for r in range(n_res) for u in range(nb)]
                blocks = [(r * nb + u, r * nb + u - 1 if u else None, None)
                          for r in range(n_res) for u in range(nb)]
                attend(pi, d, kv, blocks)
                return carry
            trips = d // n_res
        else:
            per_res = nb // ATT_GROUP
            def body(it, carry, pi=pi, d=d, per_res=per_res):
                res, blk0 = it // per_res, (it % per_res) * ATT_GROUP
                kv = [(res, jnp.maximum(blk0 - 1, 0))] + [(res, blk0 + u) for u in range(ATT_GROUP)]
                attend(pi, d, kv, [(1, 0, blk0 > 0)] + [(u + 1, u, None) for u in range(1, ATT_GROUP)])
                return carry
            trips = d * per_res
        lax.fori_loop(0, trips, body, 0)

    m = jnp.maximum(jnp.maximum(max_scr[0], max_scr[1]), max_scr[2])
    num = den = None
    for pi in range(len(DILATIONS)):
        w = jnp.exp2(max_scr[pi] - m)
        num = w * acc_scr[pi] if num is None else num + w * acc_scr[pi]
        den = w * sum_scr[pi] if den is None else den + w * sum_scr[pi]
    y_ref[...] = (num / den * gate_ref[...].astype(F32)).astype(y_ref.dtype)


def _dilated_attention(p_att, p_gate, batch, seq):
    per_res = seq // RESIDUES

    def tile(blk0):
        return pl.BlockSpec((RESIDUES, per_res, LANES), lambda b, g: (b, 0, blk0 + g))

    return pl.pallas_call(
        functools.partial(_attn_kernel, seq=seq),
        grid=(batch, PAIRS),
        in_specs=[tile(0), tile(PAIRS), tile(2 * PAIRS), tile(0)],
        out_specs=pl.BlockSpec((RESIDUES, per_res, LANES), lambda b, g: (b, 0, g)),
        out_shape=jax.ShapeDtypeStruct((batch * RESIDUES, per_res, ATT_W), BF16),
        scratch_shapes=[pltpu.VMEM((len(DILATIONS), RESIDUES, per_res, LANES), F32)] * 3,
        compiler_params=pltpu.CompilerParams(
            dimension_semantics=("arbitrary", "arbitrary"), vmem_limit_bytes=VMEM_LIMIT),
        name="dilated_attention",
    )(p_att, p_att, p_att, p_gate)


def _out_kernel(yr_ref, ya_ref, x_ref, wr_ref, wa_ref, fw_ref, o_ref, tmp_ref):
    per_res = ya_ref.shape[1]
    for c in range(ATT_W // LANES):
        for r in range(RESIDUES):
            tmp_ref[c, pl.ds(r, per_res, stride=RESIDUES), :] = (
                ya_ref[r, :, c * LANES:(c + 1) * LANES].astype(F32))
    ya = jnp.concatenate([tmp_ref[c] for c in range(ATT_W // LANES)], axis=1).astype(BF16)
    h = (x_ref[...]
         + jnp.dot(yr_ref[...], wr_ref[...], preferred_element_type=F32)
         + jnp.dot(ya, wa_ref[...], preferred_element_type=F32))
    ms = jnp.mean(h * h, axis=-1, keepdims=True)
    o_ref[...] = h * lax.rsqrt(ms + RMS_EPS) * fw_ref[...]


def _out_projection(y_rwkv, y_att, x2, w_out_bf16, final_norm_w, batch, seq):
    tm = min(PROJ_TM, seq)
    tiles = seq // tm
    tok = lambda width: pl.BlockSpec((tm, width), lambda i: (i, 0))
    return pl.pallas_call(
        _out_kernel,
        grid=(batch * tiles,),
        in_specs=[
            tok(RWKV_W),
            pl.BlockSpec((None, RESIDUES, tm // RESIDUES, ATT_W), lambda i: (i // tiles, 0, i % tiles, 0)),
            tok(D_MODEL),
            pl.BlockSpec((RWKV_W, D_MODEL), lambda i: (0, 0)),
            pl.BlockSpec((ATT_W, D_MODEL), lambda i: (1, 0)),
            pl.BlockSpec((1, D_MODEL), lambda i: (0, 0)),
        ],
        out_specs=tok(D_MODEL),
        out_shape=jax.ShapeDtypeStruct((batch * seq, D_MODEL), F32),
        scratch_shapes=[pltpu.VMEM((ATT_W // LANES, tm, LANES), F32)],
        compiler_params=pltpu.CompilerParams(
            dimension_semantics=("arbitrary",), vmem_limit_bytes=VMEM_LIMIT),
        name="out_projection",
    )(y_rwkv, y_att, x2, w_out_bf16, w_out_bf16, final_norm_w.reshape(1, D_MODEL))


def kernel(x, norm_w, w_in, mu_shift, w0, w_up, a0, a_up, k_k, k_a, r_k, ln_x_w, ln_x_b, w_out,
           final_norm_w):
    batch, seq, _ = x.shape
    assert seq % (ATT_BLK * max(DILATIONS)) == 0 and seq % STEP_ROWS == 0
    x2 = x.astype(F32).reshape(batch * seq, D_MODEL)
    nw = norm_w.astype(F32)
    p_shift, gate_rwkv = _rwkv_projection(x2, nw, w_in[:, :ATT0].astype(BF16), seq, mu_shift)
    p_att, gate_att = _att_projection(x2, nw, w_in[:, ATT0:].astype(BF16), batch, seq)
    y_rwkv = _rwkv_time_mix(p_shift, gate_rwkv, batch, seq, w0, w_up, a0, a_up, k_k, k_a, r_k,
                            ln_x_w, ln_x_b)
    per_res = seq // RESIDUES
    y_att = _dilated_attention(p_att.reshape(batch * RESIDUES, per_res, 3 * ATT_W),
                               gate_att.reshape(batch * RESIDUES, per_res, ATT_W), batch, seq)
    out = _out_projection(y_rwkv, y_att.reshape(batch, RESIDUES, per_res, ATT_W), x2, w_out.astype(BF16),
                          final_norm_w.astype(F32), batch, seq)
    return out.reshape(batch, seq, D_MODEL).astype(x.dtype)
```

```python
import functools

import jax
import jax.numpy as jnp
from jax import lax
from jax.experimental import pallas as pl
from jax.experimental.pallas import tpu as pltpu

D_MODEL = 1024
HEAD_DIM = 64
RWKV_W = 1024
ATT_W = 1024
MIX_W = RWKV_W + ATT_W
LORA_RANK = 64
SHIFT_COLS = 3 * RWKV_W + 2 * LORA_RANK
RWKV_GATE0 = SHIFT_COLS
ATT0 = RWKV_GATE0 + RWKV_W
ATT_GATE0 = ATT0 + 3 * ATT_W
DILATIONS = (1, 4, 16)
ATT_BLK = 128
RESIDUES = max(DILATIONS)
ATT_GROUP = 8
LOG2_E = 1.4426950408889634
RMS_EPS = 1e-5
GN_EPS = 64e-5

LANES = 128
SUBLANES = 8
PAIRS = RWKV_W // LANES
CHUNK = 64
K_COL0, V_COL0, LORA_COL0 = RWKV_W, 2 * RWKV_W, 3 * RWKV_W
PROJ_TM = 512

F32 = jnp.float32
BF16 = jnp.bfloat16
VMEM_LIMIT = 56 * 1024 * 1024

NT_DIMS = (((1,), (1,)), ((), ()))
TN_DIMS = (((0,), (0,)), ((), ()))


def _mm(a, b):
    return jnp.dot(a.astype(BF16), b.astype(BF16), preferred_element_type=F32)


def _mm_nt(a, b):
    return lax.dot_general(a.astype(BF16), b.astype(BF16), NT_DIMS, preferred_element_type=F32)


def _mm_tn(a, b):
    return lax.dot_general(a.astype(BF16), b.astype(BF16), TN_DIMS, preferred_element_type=F32)


def _split(x):
    hi = x.astype(BF16)
    return hi, (x - hi.astype(F32)).astype(BF16)


def _mm_split_rhs(w_bf16, x):
    n = x.shape[1]
    both = jnp.dot(w_bf16, jnp.concatenate(_split(x), axis=1), preferred_element_type=F32)
    return both[:, :n] + both[:, n:]


def _mm_x3(x, w3):
    hi, lo = _split(x)
    return jnp.dot(jnp.concatenate([hi, lo, hi], axis=1), w3, preferred_element_type=F32)


def _stack_x3(w):
    hi, lo = _split(w.astype(F32))
    return jnp.concatenate([hi, hi, lo], axis=0)


def _sigmoid(x):
    return 1.0 / (1.0 + jnp.exp(-x))


REGROUP_STRIDE = 4


def _regrouped_rows(x_refs, tmp_ref):
    tm = x_refs[0].shape[0]
    quarter, per_res = tm // REGROUP_STRIDE, tm // RESIDUES
    slabs = []
    for c, ref in enumerate(x_refs):
        for g in range(REGROUP_STRIDE):
            tmp_ref[c, g * quarter:(g + 1) * quarter, :] = ref[pl.ds(g, quarter, stride=REGROUP_STRIDE), :]
        slabs.append(jnp.concatenate(
            [tmp_ref[c, pl.ds((r % REGROUP_STRIDE) * quarter + r // REGROUP_STRIDE, per_res,
                              stride=REGROUP_STRIDE), :] for r in range(RESIDUES)], axis=0))
    return jnp.concatenate(slabs, axis=1)


def _proj_kernel(*refs, epilogue, tiles_per_seq):
    if epilogue == "qkv_gate":
        slabs = D_MODEL // LANES
        nw_ref, w_ref, *rest, tmp_ref = refs[slabs:]
        x = _regrouped_rows(refs[:slabs], tmp_ref)
    else:
        x_ref, nw_ref, w_ref, *rest = refs
        x = x_ref[...]
    ms = jnp.mean(x * x, axis=-1, keepdims=True)
    xn = (x * lax.rsqrt(ms + RMS_EPS) * nw_ref[...]).astype(BF16)
    o = jnp.dot(xn, w_ref[...], preferred_element_type=F32)
    if epilogue == "shift_gate":
        mu_ref, o_ref, gate_ref, prev_ref = rest
        tm = o.shape[0]
        gate = o[:, SHIFT_COLS:]
        gate_ref[...] = (gate * _sigmoid(gate)).astype(gate_ref.dtype)
        o = o[:, :SHIFT_COLS]
        first = pl.program_id(0) % tiles_per_seq == 0
        last = jnp.where(first, 0.0, prev_ref[SUBLANES - 1:SUBLANES, :])
        row = lax.broadcasted_iota(jnp.int32, (tm, 1), 0)
        oprev = jnp.where(row == 0, last, pltpu.roll(o, 1, 0))
        prev_ref[...] = o[tm - SUBLANES:, :]
        o_ref[...] = o + (oprev - o) * mu_ref[...]
    elif epilogue == "qkv_gate":
        qkv_ref, gate_ref = rest
        qkv_ref[...] = o[:, :3 * ATT_W].reshape(qkv_ref.shape)
        gate = o[:, 3 * ATT_W:]
        gate_ref[...] = (gate * _sigmoid(gate)).astype(gate_ref.dtype).reshape(gate_ref.shape)


def _rwkv_projection(x2, norm_w, w_bf16, seq, mu):
    tokens = x2.shape[0]
    tm = min(PROJ_TM, seq)
    once = pl.Buffered(1)
    return pl.pallas_call(
        functools.partial(_proj_kernel, epilogue="shift_gate", tiles_per_seq=seq // tm),
        grid=(tokens // tm,),
        in_specs=[pl.BlockSpec((tm, D_MODEL), lambda i: (i, 0)),
                  pl.BlockSpec((1, D_MODEL), lambda i: (0, 0), pipeline_mode=once),
                  pl.BlockSpec(w_bf16.shape, lambda i: (0, 0), pipeline_mode=once),
                  pl.BlockSpec((1, SHIFT_COLS), lambda i: (0, 0), pipeline_mode=once)],
        out_specs=[pl.BlockSpec((tm, SHIFT_COLS), lambda i: (i, 0)),
                   pl.BlockSpec((tm, RWKV_W), lambda i: (i, 0))],
        out_shape=[jax.ShapeDtypeStruct((tokens, SHIFT_COLS), F32),
                   jax.ShapeDtypeStruct((tokens, RWKV_W), BF16)],
        scratch_shapes=[pltpu.VMEM((SUBLANES, SHIFT_COLS), F32)],
        compiler_params=pltpu.CompilerParams(
            dimension_semantics=("arbitrary",), vmem_limit_bytes=VMEM_LIMIT),
        name="in_projection_rwkv",
    )(x2, norm_w.reshape(1, D_MODEL), w_bf16, mu.reshape(1, SHIFT_COLS).astype(F32))


def _att_projection(x2, norm_w, w_bf16, batch, seq):
    tm = min(PROJ_TM, seq)
    tiles = seq // tm
    rows = tm // RESIDUES
    out_block = lambda width: pl.BlockSpec((None, RESIDUES, rows, width),
                                           lambda i: (i // tiles, 0, i % tiles, 0))
    return pl.pallas_call(
        functools.partial(_proj_kernel, epilogue="qkv_gate", tiles_per_seq=tiles),
        grid=(batch * tiles,),
        in_specs=[pl.BlockSpec((tm, LANES), functools.partial(lambda c, i: (i, c), c))
                  for c in range(D_MODEL // LANES)]
                 + [pl.BlockSpec((1, D_MODEL), lambda i: (0, 0)),
                    pl.BlockSpec(w_bf16.shape, lambda i: (0, 0))],
        out_specs=[out_block(3 * ATT_W), out_block(ATT_W)],
        out_shape=[jax.ShapeDtypeStruct((batch, RESIDUES, seq // RESIDUES, 3 * ATT_W), F32),
                   jax.ShapeDtypeStruct((batch, RESIDUES, seq // RESIDUES, ATT_W), BF16)],
        scratch_shapes=[pltpu.VMEM((D_MODEL // LANES, tm, LANES), F32)],
        compiler_params=pltpu.CompilerParams(
            dimension_semantics=("arbitrary",), vmem_limit_bytes=VMEM_LIMIT),
        name="in_projection_attention",
    )(*[x2] * (D_MODEL // LANES), norm_w.reshape(1, D_MODEL), w_bf16)


OPERANDS = 6
STEP_CHUNKS = 2
STEP_ROWS = STEP_CHUNKS * CHUNK


def _rwkv_kernel(p_ref, gate_ref, w0_ref, a0_ref, kk_ref, ka_ref, rk_ref, lnw_ref, lnb_ref,
                 wu3_ref, au3_ref, y_ref, state_ref, ops_ref, rt_ref, gend_ref, bonus_ref,
                 *, steps_per_seq):
    t = pl.program_id(0)

    @pl.when(t == 0)
    def _():
        ops_ref[...] = jnp.zeros_like(ops_ref)
        rt_ref[...] = jnp.zeros_like(rt_ref)
        gend_ref[...] = jnp.zeros_like(gend_ref)
        bonus_ref[...] = jnp.zeros_like(bonus_ref)
        state_ref[...] = jnp.zeros_like(state_ref)

    fill, use = t % 2, (t + 1) % 2
    row = lax.broadcasted_iota(jnp.int32, (CHUNK, LANES), 0)
    lane = lax.broadcasted_iota(jnp.int32, (CHUNK, LANES), 1)
    head0 = lane < HEAD_DIM
    tcol = lane % CHUNK
    strict, incl = tcol < row, tcol <= row
    same16 = (tcol // 16) == (row // 16)
    same32 = (tcol // 32) == (row // 32)
    eye = jnp.where(tcol == row, 1.0, 0.0).astype(F32)
    bi = lax.broadcasted_iota(jnp.int32, (LANES, LANES), 0)
    bj = lax.broadcasted_iota(jnp.int32, (LANES, LANES), 1)
    block_diag = (bi // HEAD_DIM) == (bj // HEAD_DIM)
    head_ones = jnp.where(block_diag, 1.0, 0.0).astype(BF16)
    no_ones = jnp.zeros_like(head_ones)
    head_ones_diag = jnp.concatenate([jnp.concatenate([head_ones, no_ones], axis=1),
                                      jnp.concatenate([no_ones, head_ones], axis=1)], axis=0)
    ti = lax.broadcasted_iota(jnp.int32, (STEP_ROWS, STEP_ROWS), 0)
    tj = lax.broadcasted_iota(jnp.int32, (STEP_ROWS, STEP_ROWS), 1)
    tri = jnp.where((tj <= ti) & (tj // CHUNK == ti // CHUNK), 1.0, 0.0).astype(BF16)
    pairs = range(PAIRS)
    chunks = range(STEP_CHUNKS)
    units = [(c, g) for c in chunks for g in pairs]
    cols = lambda c0, g: slice(c0 + g * LANES, c0 + (g + 1) * LANES)
    rows = lambda c: slice(c * CHUNK, (c + 1) * CHUNK)

    def stack(x):
        x = x.astype(BF16)
        zero = jnp.zeros_like(x)
        return jnp.concatenate([jnp.where(head0, x, zero), jnp.where(head0, zero, x)], axis=0)

    def chain():
        ops = lambda i, c, g: ops_ref[use, i, rows(c), cols(0, g)]
        a_t, b_t, k_t, v, b_h, k_h = ([ops(i, c, g) for c, g in units] for i in range(OPERANDS))
        r_t = [rt_ref[use, rows(c), cols(0, g)] for c, g in units]
        n = range(len(units))
        v_s = [stack(x) for x in v]
        gram = [_mm_nt(jnp.concatenate([a_t[u], r_t[u]], axis=0),
                       jnp.concatenate([stack(b_t[u]), stack(k_t[u])], axis=0)) for u in n]
        yield
        a_ab = [jnp.where(strict, x[:CHUNK, :LANES], 0.0) for x in gram]
        a_ak = [jnp.where(strict, x[:CHUNK, LANES:], 0.0) for x in gram]
        a_rb = [jnp.where(incl, x[CHUNK:, :LANES], 0.0) for x in gram]
        a_rk = [jnp.where(incl, x[CHUNK:, LANES:], 0.0) for x in gram]
        nil = [jnp.where(same16, x, 0.0) for x in a_ab]
        inv = [eye + x for x in nil]
        nil = [_mm(x, stack(x)) for x in nil]
        yield
        av = [_mm(jnp.concatenate([a_ak[u], a_rk[u]], axis=0), v_s[u]) for u in n]
        yield
        for _ in range(2):
            both = [_mm(jnp.concatenate([nil[u], inv[u]], axis=0), stack(nil[u])) for u in n]
            nil = [x[:CHUNK] for x in both]
            inv = [inv[u] + both[u][CHUNK:] for u in n]
            yield
        vk = [_mm_tn(v[u], k_h[u]) for u in n]
        yield
        inv = [inv[u] + _mm(inv[u], stack(nil[u])) for u in n]
        yield
        for off_diag in (same32 & ~same16, ~same32):
            low = [_mm(inv[u], stack(jnp.where(off_diag, a_ab[u], 0.0))) for u in n]
            yield
            inv = [inv[u] + _mm(low[u], stack(inv[u])) for u in n]
            yield
        tx = [_mm(inv[u], jnp.concatenate([stack(a_t[u]), stack(av[u][:CHUNK])], axis=1))
              for u in n]
        yield
        zz = [_mm(a_rb[u], jnp.concatenate([stack(tx[u][:, :LANES]), stack(tx[u][:, LANES:])], axis=1))
              for u in n]
        yield
        r_p = [r_t[u] + zz[u][:, :LANES] for u in n]
        y0 = [zz[u][:, LANES:] + av[u][CHUNK:] for u in n]
        mn = [_mm_tn(tx[u], b_h[u]) for u in n]
        yield
        fresh = lax.rem(t - 1, steps_per_seq) == 0
        state = [jnp.where(fresh, 0.0, state_ref[g]) for g in pairs]
        y = []
        for u, (c, g) in enumerate(units):
            y.append(_mm_nt(r_p[u], state[g]) + y0[u])
            m_t = jnp.where(block_diag, mn[u][:LANES], 0.0)
            n_t = jnp.where(block_diag, mn[u][LANES:] + vk[u], 0.0)
            g_end = gend_ref[use, c * SUBLANES:c * SUBLANES + 1, cols(0, g)]
            state[g] = state[g] * g_end + _mm(state[g], m_t) + n_t
            if g == PAIRS - 1:
                yield
        for g in pairs:
            state_ref[g] = state[g]
        y_all = jnp.concatenate(y, axis=0)
        moments = jnp.dot(jnp.concatenate([y_all, y_all * y_all], axis=1).astype(BF16), head_ones_diag,
                          preferred_element_type=F32) * (1.0 / HEAD_DIM)
        for u, (c, g) in enumerate(units):
            mean = moments[u * CHUNK:(u + 1) * CHUNK, :LANES]
            var = moments[u * CHUNK:(u + 1) * CHUNK, LANES:] - mean * mean
            yn = (y[u] - mean) * lax.rsqrt(var + GN_EPS)
            yn = yn * lnw_ref[:, cols(0, g)] + lnb_ref[:, cols(0, g)]
            y_ref[rows(c), cols(0, g)] = ((yn + bonus_ref[use, rows(c), cols(0, g)])
                                          * gate_ref[rows(c), cols(0, g)].astype(F32)).astype(y_ref.dtype)

    def prepare():
        lo = p_ref[:, LORA_COL0:LORA_COL0 + LANES]
        z_all = -(w0_ref[...] + _mm_x3(jnp.tanh(lo[:, :LORA_RANK]), wu3_ref[...]))
        a_all = a0_ref[...] + _mm_x3(lo[:, LORA_RANK:], au3_ref[...])
        yield
        for g in pairs:
            c = cols(0, g)
            z = z_all[:, c]
            w = -(jnp.maximum(z, 0.0) + jnp.log(1.0 + jnp.exp(-jnp.abs(z)))) - 0.5
            lw = -jnp.exp(w)
            lg = _mm_split_rhs(tri, lw)
            ends = [lg[(i + 1) * CHUNK - 1:(i + 1) * CHUNK, :] for i in chunks]
            lg_end = jnp.concatenate([jnp.broadcast_to(e, (CHUNK, LANES)) for e in ends], axis=0)
            for i in chunks:
                gend_ref[fill, i * SUBLANES:(i + 1) * SUBLANES, c] = jnp.broadcast_to(
                    jnp.exp(ends[i]), (SUBLANES, LANES))
            e_inv = jnp.exp(-lg)
            e_end = jnp.exp(lg_end - lg)
            a = _sigmoid(a_all[:, c])
            r, k, v = p_ref[:, c], p_ref[:, cols(K_COL0, g)], p_ref[:, cols(V_COL0, g)]
            kk = k * kk_ref[:, c]
            k2 = k * (1.0 + (a - 1.0) * ka_ref[:, c])
            sums = jnp.dot(jnp.concatenate([kk * kk, r * k2 * rk_ref[:, c]], axis=1).astype(BF16),
                           head_ones_diag, preferred_element_type=F32)
            kk = kk / jnp.maximum(jnp.sqrt(sums[:, :LANES]), 1e-12)
            bonus_ref[fill, :, c] = sums[:, LANES:] * v
            rt_ref[fill, :, c] = r * jnp.exp(lg)
            ops_ref[fill, 0, :, c] = (-kk * jnp.exp(lg - lw)).astype(BF16)
            ops_ref[fill, 1, :, c] = (kk * a * e_inv).astype(BF16)
            ops_ref[fill, 2, :, c] = (k2 * e_inv).astype(BF16)
            ops_ref[fill, 3, :, c] = v.astype(BF16)
            ops_ref[fill, 4, :, c] = (kk * a * e_end).astype(BF16)
            ops_ref[fill, 5, :, c] = (k2 * e_end).astype(BF16)
            yield

    done = object()
    streams = [chain(), prepare()]
    while streams:
        streams = [s for s in streams if next(s, done) is not done]


def _rwkv_time_mix(p_shift, p_gates, batch, seq, w0, w_up, a0, a_up, k_k, k_a, r_k, ln_x_w, ln_x_b):
    nt = batch * seq // STEP_ROWS
    row2 = lambda v: v.reshape(1, -1).astype(F32)
    whole = lambda shape: pl.BlockSpec(shape, lambda t: (0, 0))
    chan = whole((1, RWKV_W))
    lora = whole((3 * LORA_RANK, RWKV_W))
    prepared = lambda t: (jnp.minimum(t, nt - 1), 0)
    chained = lambda t: (jnp.maximum(t - 1, 0), 0)
    return pl.pallas_call(
        functools.partial(_rwkv_kernel, steps_per_seq=seq // STEP_ROWS),
        grid=(nt + 1,),
        in_specs=[pl.BlockSpec((STEP_ROWS, SHIFT_COLS), prepared),
                  pl.BlockSpec((STEP_ROWS, RWKV_W), chained),
                  chan, chan, chan, chan, chan, chan, chan, lora, lora],
        out_specs=pl.BlockSpec((STEP_ROWS, RWKV_W), chained),
        out_shape=jax.ShapeDtypeStruct((batch * seq, RWKV_W), BF16),
        scratch_shapes=[pltpu.VMEM((PAIRS, LANES, LANES), F32),
                        pltpu.VMEM((2, OPERANDS, STEP_ROWS, RWKV_W), BF16),
                        pltpu.VMEM((2, STEP_ROWS, RWKV_W), F32),
                        pltpu.VMEM((2, STEP_CHUNKS * SUBLANES, RWKV_W), F32),
                        pltpu.VMEM((2, STEP_ROWS, RWKV_W), F32)],
        compiler_params=pltpu.CompilerParams(
            dimension_semantics=("arbitrary",), vmem_limit_bytes=VMEM_LIMIT),
        name="rwkv7_time_mix",
    )(p_shift, p_gates, row2(w0), row2(a0), row2(k_k), row2(k_a), row2(r_k), row2(ln_x_w), row2(ln_x_b),
      _stack_x3(w_up), _stack_x3(a_up))


def _attn_kernel(q_ref, k_ref, v_ref, gate_ref, y_ref, acc_scr, max_scr, sum_scr, *, seq):
    n_blocks = seq // ATT_BLK
    qi = lax.broadcasted_iota(jnp.int32, (2 * ATT_BLK, ATT_BLK), 0) % ATT_BLK
    kj = lax.broadcasted_iota(jnp.int32, (2 * ATT_BLK, ATT_BLK), 1)
    head0 = lax.broadcasted_iota(jnp.int32, (ATT_BLK, LANES), 1) < HEAD_DIM
    ones = jnp.ones((ATT_BLK, LANES), BF16)
    scale = HEAD_DIM ** -0.5 * LOG2_E
    neg = -1e30

    def attend(pi, d, kv_blocks, blocks):
        pieces = RESIDUES // d
        length = ATT_BLK // pieces
        pos = lambda i: pieces * (i % length) + i // length
        ok_prev, ok_cur = pos(kj) >= pos(qi), pos(kj) <= pos(qi)

        def at(res, blk, j):
            return j * d + res, pl.ds(pl.multiple_of(blk * length, length), length)

        def load(ref, res, blk):
            parts = []
            for j in range(pieces):
                r, rows = at(res, blk, j)
                parts.append(ref[r, rows, :])
            return jnp.concatenate(parts, axis=0) if pieces > 1 else parts[0]

        def store(scr, res, blk, val):
            for j in range(pieces):
                r, rows = at(res, blk, j)
                scr[pi, r, rows, :] = val[j * length:(j + 1) * length]

        kb = [load(k_ref, *b).astype(BF16) for b in kv_blocks]
        vb = [jnp.concatenate([load(v_ref, *b).astype(BF16), ones], axis=1) for b in kv_blocks]
        qs = []
        for cur, _, _ in blocks:
            q = load(q_ref, *kv_blocks[cur]) * scale
            qs.append(jnp.concatenate([jnp.where(head0, q, 0.0), jnp.where(head0, 0.0, q)],
                                      axis=0).astype(BF16))
        scores = []
        for (cur, prev, prev_ok), q in zip(blocks, qs):
            if prev is None:
                scores.append(jnp.where(ok_cur, _mm_nt(q, kb[cur]), neg))
            else:
                s = _mm_nt(q, jnp.concatenate([kb[prev], kb[cur]], axis=0))
                okp = ok_prev if prev_ok is None else ok_prev & prev_ok
                scores.append(jnp.concatenate([jnp.where(okp, s[:, :ATT_BLK], neg),
                                               jnp.where(ok_cur, s[:, ATT_BLK:], neg)], axis=1))
        mx = [jnp.max(s, axis=-1, keepdims=True) for s in scores]
        probs = [jnp.exp2(s - m).astype(BF16) for s, m in zip(scores, mx)]
        pv = [_mm(p, vb[cur] if prev is None else jnp.concatenate([vb[prev], vb[cur]], axis=0))
              for (cur, prev, _), p in zip(blocks, probs)]
        for (cur, _, _), m, x in zip(blocks, mx, pv):
            res, blk = kv_blocks[cur]
            store(acc_scr, res, blk, jnp.where(head0, x[:ATT_BLK, :LANES], x[ATT_BLK:, :LANES]))
            store(sum_scr, res, blk, jnp.where(head0, x[:ATT_BLK, LANES:], x[ATT_BLK:, LANES:]))
            store(max_scr, res, blk, jnp.where(head0, m[:ATT_BLK], m[ATT_BLK:]))

    for pi, d in enumerate(DILATIONS):
        nb = n_blocks // d
        if nb <= ATT_GROUP:
            n_res = ATT_GROUP // nb
            def body(it, carry, pi=pi, d=d, nb=nb, n_res=n_res):
                kv = [(it * n_res + r, u) for r in range(n_res) for u in range(nb)]
                blocks = [(r * nb + u, r * nb + u - 1 if u else None, None)
                          for r in range(n_res) for u in range(nb)]
                attend(pi, d, kv, blocks)
                return carry
            trips = d // n_res
        else:
            per_res = nb // ATT_GROUP
            def body(it, carry, pi=pi, d=d, per_res=per_res):
                res, blk0 = it // per_res, (it % per_res) * ATT_GROUP
                kv = [(res, jnp.maximum(blk0 - 1, 0))] + [(res, blk0 + u) for u in range(ATT_GROUP)]
                attend(pi, d, kv, [(1, 0, blk0 > 0)] + [(u + 1, u, None) for u in range(1, ATT_GROUP)])
                return carry
            trips = d * per_res
        lax.fori_loop(0, trips, body, 0)

    m = jnp.maximum(jnp.maximum(max_scr[0], max_scr[1]), max_scr[2])
    num = den = None
    for pi in range(len(DILATIONS)):
        w = jnp.exp2(max_scr[pi] - m)
        num = w * acc_scr[pi] if num is None else num + w * acc_scr[pi]
        den = w * sum_scr[pi] if den is None else den + w * sum_scr[pi]
    y_ref[...] = (num / den * gate_ref[...].astype(F32)).astype(y_ref.dtype)


def _dilated_attention(p_att, p_gate, batch, seq):
    per_res = seq // RESIDUES

    def tile(blk0):
        return pl.BlockSpec((RESIDUES, per_res, LANES), lambda b, g: (b, 0, blk0 + g))

    return pl.pallas_call(
        functools.partial(_attn_kernel, seq=seq),
        grid=(batch, PAIRS),
        in_specs=[tile(0), tile(PAIRS), tile(2 * PAIRS), tile(0)],
        out_specs=pl.BlockSpec((RESIDUES, per_res, LANES), lambda b, g: (b, 0, g)),
        out_shape=jax.ShapeDtypeStruct((batch * RESIDUES, per_res, ATT_W), BF16),
        scratch_shapes=[pltpu.VMEM((len(DILATIONS), RESIDUES, per_res, LANES), F32)] * 3,
        compiler_params=pltpu.CompilerParams(
            dimension_semantics=("arbitrary", "arbitrary"), vmem_limit_bytes=VMEM_LIMIT),
        name="dilated_attention",
    )(p_att, p_att, p_att, p_gate)


def _out_kernel(yr_ref, ya_ref, x_ref, wr_ref, wa_ref, fw_ref, o_ref, tmp_ref, seq_ref):
    per_res = ya_ref.shape[1]
    quarter = per_res * REGROUP_STRIDE
    for c in range(ATT_W // LANES):
        for r in range(RESIDUES):
            tmp_ref[c, pl.ds((r % REGROUP_STRIDE) * quarter + r // REGROUP_STRIDE, per_res,
                             stride=REGROUP_STRIDE), :] = ya_ref[r, :, c * LANES:(c + 1) * LANES].astype(F32)
        for g in range(REGROUP_STRIDE):
            seq_ref[c, pl.ds(g, quarter, stride=REGROUP_STRIDE), :] = tmp_ref[c, g * quarter:(g + 1) * quarter, :]
    ya = jnp.concatenate([seq_ref[c] for c in range(ATT_W // LANES)], axis=1).astype(BF16)
    h = (x_ref[...]
         + jnp.dot(yr_ref[...], wr_ref[...], preferred_element_type=F32)
         + jnp.dot(ya, wa_ref[...], preferred_element_type=F32))
    ms = jnp.mean(h * h, axis=-1, keepdims=True)
    o_ref[...] = h * lax.rsqrt(ms + RMS_EPS) * fw_ref[...]


def _out_projection(y_rwkv, y_att, x2, w_out_bf16, final_norm_w, batch, seq):
    tm = min(PROJ_TM, seq)
    tiles = seq // tm
    tok = lambda width: pl.BlockSpec((tm, width), lambda i: (i, 0))
    return pl.pallas_call(
        _out_kernel,
        grid=(batch * tiles,),
        in_specs=[
            tok(RWKV_W),
            pl.BlockSpec((None, RESIDUES, tm // RESIDUES, ATT_W), lambda i: (i // tiles, 0, i % tiles, 0)),
            tok(D_MODEL),
            pl.BlockSpec((RWKV_W, D_MODEL), lambda i: (0, 0)),
            pl.BlockSpec((ATT_W, D_MODEL), lambda i: (1, 0)),
            pl.BlockSpec((1, D_MODEL), lambda i: (0, 0)),
        ],
        out_specs=tok(D_MODEL),
        out_shape=jax.ShapeDtypeStruct((batch * seq, D_MODEL), F32),
        scratch_shapes=[pltpu.VMEM((ATT_W // LANES, tm, LANES), F32)] * 2,
        compiler_params=pltpu.CompilerParams(
            dimension_semantics=("arbitrary",), vmem_limit_bytes=VMEM_LIMIT),
        name="out_projection",
    )(y_rwkv, y_att, x2, w_out_bf16, w_out_bf16, final_norm_w.reshape(1, D_MODEL))


def kernel(x, norm_w, w_in, mu_shift, w0, w_up, a0, a_up, k_k, k_a, r_k, ln_x_w, ln_x_b, w_out,
           final_norm_w):
    batch, seq, _ = x.shape
    assert seq % (ATT_BLK * max(DILATIONS)) == 0 and seq % STEP_ROWS == 0
    x2 = x.astype(F32).reshape(batch * seq, D_MODEL)
    nw = norm_w.astype(F32)
    p_shift, gate_rwkv = _rwkv_projection(x2, nw, w_in[:, :ATT0].astype(BF16), seq, mu_shift)
    p_att, gate_att = _att_projection(x2, nw, w_in[:, ATT0:].astype(BF16), batch, seq)
    y_rwkv = _rwkv_time_mix(p_shift, gate_rwkv, batch, seq, w0, w_up, a0, a_up, k_k, k_a, r_k,
                            ln_x_w, ln_x_b)
    per_res = seq // RESIDUES
    y_att = _dilated_attention(p_att.reshape(batch * RESIDUES, per_res, 3 * ATT_W),
                               gate_att.reshape(batch * RESIDUES, per_res, ATT_W), batch, seq)
    out = _out_projection(y_rwkv, y_att.reshape(batch, RESIDUES, per_res, ATT_W), x2, w_out.astype(BF16),
                          final_norm_w.astype(F32), batch, seq)
    return out.reshape(batch, seq, D_MODEL).astype(x.dtype)
```

```python
import functools

import jax
import jax.numpy as jnp
from jax import lax
from jax.experimental import pallas as pl
from jax.experimental.pallas import tpu as pltpu

D_MODEL = 1024
HEAD_DIM = 64
RWKV_W = 1024
ATT_W = 1024
MIX_W = RWKV_W + ATT_W
LORA_RANK = 64
SHIFT_COLS = 3 * RWKV_W + 2 * LORA_RANK
RWKV_GATE0 = SHIFT_COLS
ATT0 = RWKV_GATE0 + RWKV_W
ATT_GATE0 = ATT0 + 3 * ATT_W
DILATIONS = (1, 4, 16)
ATT_BLK = 128
RESIDUES = max(DILATIONS)
ATT_GROUP = 8
LOG2_E = 1.4426950408889634
RMS_EPS = 1e-5
GN_EPS = 64e-5

LANES = 128
SUBLANES = 8
PAIRS = RWKV_W // LANES
CHUNK = 64
K_COL0, V_COL0, LORA_COL0 = RWKV_W, 2 * RWKV_W, 3 * RWKV_W
PROJ_TM = 512

F32 = jnp.float32
BF16 = jnp.bfloat16
VMEM_LIMIT = 56 * 1024 * 1024

NT_DIMS = (((1,), (1,)), ((), ()))
TN_DIMS = (((0,), (0,)), ((), ()))


def _mm(a, b):
    return jnp.dot(a.astype(BF16), b.astype(BF16), preferred_element_type=F32)


def _mm_nt(a, b):
    return lax.dot_general(a.astype(BF16), b.astype(BF16), NT_DIMS, preferred_element_type=F32)


def _mm_tn(a, b):
    return lax.dot_general(a.astype(BF16), b.astype(BF16), TN_DIMS, preferred_element_type=F32)


def _split(x):
    hi = x.astype(BF16)
    return hi, (x - hi.astype(F32)).astype(BF16)


def _mm_split_rhs(w_bf16, x):
    n = x.shape[1]
    both = jnp.dot(w_bf16, jnp.concatenate(_split(x), axis=1), preferred_element_type=F32)
    return both[:, :n] + both[:, n:]


def _mm_x3(x, w3):
    hi, lo = _split(x)
    return jnp.dot(jnp.concatenate([hi, lo, hi], axis=1), w3, preferred_element_type=F32)


def _stack_x3(w):
    hi, lo = _split(w.astype(F32))
    return jnp.concatenate([hi, hi, lo], axis=0)


def _sigmoid(x):
    return 1.0 / (1.0 + jnp.exp(-x))


REGROUP_STRIDE = 4


def _regrouped_rows(x_refs, tmp_ref):
    tm = x_refs[0].shape[0]
    quarter, per_res = tm // REGROUP_STRIDE, tm // RESIDUES
    slabs = []
    for c, ref in enumerate(x_refs):
        for g in range(REGROUP_STRIDE):
            tmp_ref[c, g * quarter:(g + 1) * quarter, :] = ref[pl.ds(g, quarter, stride=REGROUP_STRIDE), :]
        slabs.append(jnp.concatenate(
            [tmp_ref[c, pl.ds((r % REGROUP_STRIDE) * quarter + r // REGROUP_STRIDE, per_res,
                              stride=REGROUP_STRIDE), :] for r in range(RESIDUES)], axis=0))
    return jnp.concatenate(slabs, axis=1)


def _proj_kernel(*refs, epilogue, tiles_per_seq):
    if epilogue == "qkv_gate":
        slabs = D_MODEL // LANES
        nw_ref, w_ref, *rest, tmp_ref = refs[slabs:]
        x = _regrouped_rows(refs[:slabs], tmp_ref)
    else:
        x_ref, nw_ref, w_ref, *rest = refs
        x = x_ref[...]
    ms = jnp.mean(x * x, axis=-1, keepdims=True)
    xn = (x * lax.rsqrt(ms + RMS_EPS) * nw_ref[...]).astype(BF16)
    w = w_ref[:, :4 * ATT_W] if epilogue == "qkv_gate" else w_ref[...]
    o = jnp.dot(xn, w, preferred_element_type=F32)
    if epilogue == "shift_gate":
        mu_ref, o_ref, gate_ref, prev_ref = rest
        tm = o.shape[0]
        gate = o[:, SHIFT_COLS:]
        gate_ref[...] = (gate * _sigmoid(gate)).astype(gate_ref.dtype)
        o = o[:, :SHIFT_COLS]
        first = pl.program_id(0) % tiles_per_seq == 0
        last = jnp.where(first, 0.0, prev_ref[SUBLANES - 1:SUBLANES, :])
        row = lax.broadcasted_iota(jnp.int32, (tm, 1), 0)
        oprev = jnp.where(row == 0, last, pltpu.roll(o, 1, 0))
        prev_ref[...] = o[tm - SUBLANES:, :]
        o_ref[...] = o + (oprev - o) * mu_ref[...]
    elif epilogue == "qkv_gate":
        qkv_ref, gate_ref = rest
        qkv_ref[...] = o[:, :3 * ATT_W].reshape(qkv_ref.shape)
        gate = o[:, 3 * ATT_W:]
        gate_ref[...] = (gate * _sigmoid(gate)).astype(gate_ref.dtype).reshape(gate_ref.shape)


def _rwkv_projection(x2, norm_w, w_bf16, seq, mu):
    tokens = x2.shape[0]
    tm = min(PROJ_TM, seq)
    once = pl.Buffered(1)
    return pl.pallas_call(
        functools.partial(_proj_kernel, epilogue="shift_gate", tiles_per_seq=seq // tm),
        grid=(tokens // tm,),
        in_specs=[pl.BlockSpec((tm, D_MODEL), lambda i: (i, 0)),
                  pl.BlockSpec((1, D_MODEL), lambda i: (0, 0), pipeline_mode=once),
                  pl.BlockSpec((D_MODEL, ATT0), lambda i: (0, 0), pipeline_mode=once),
                  pl.BlockSpec((1, SHIFT_COLS), lambda i: (0, 0), pipeline_mode=once)],
        out_specs=[pl.BlockSpec((tm, SHIFT_COLS), lambda i: (i, 0)),
                   pl.BlockSpec((tm, RWKV_W), lambda i: (i, 0))],
        out_shape=[jax.ShapeDtypeStruct((tokens, SHIFT_COLS), F32),
                   jax.ShapeDtypeStruct((tokens, RWKV_W), BF16)],
        scratch_shapes=[pltpu.VMEM((SUBLANES, SHIFT_COLS), F32)],
        compiler_params=pltpu.CompilerParams(
            dimension_semantics=("arbitrary",), vmem_limit_bytes=VMEM_LIMIT),
        name="in_projection_rwkv",
    )(x2, norm_w.reshape(1, D_MODEL), w_bf16, mu.reshape(1, SHIFT_COLS).astype(F32))


def _att_projection(x2, norm_w, w_bf16, batch, seq):
    tm = min(PROJ_TM, seq)
    tiles = seq // tm
    rows = tm // RESIDUES
    out_block = lambda width: pl.BlockSpec((None, RESIDUES, rows, width),
                                           lambda i: (i // tiles, 0, i % tiles, 0))
    return pl.pallas_call(
        functools.partial(_proj_kernel, epilogue="qkv_gate", tiles_per_seq=tiles),
        grid=(batch * tiles,),
        in_specs=[pl.BlockSpec((tm, LANES), functools.partial(lambda c, i: (i, c), c))
                  for c in range(D_MODEL // LANES)]
                 + [pl.BlockSpec((1, D_MODEL), lambda i: (0, 0)),
                    pl.BlockSpec((D_MODEL, ATT0), lambda i: (0, 1))],
        out_specs=[out_block(3 * ATT_W), out_block(ATT_W)],
        out_shape=[jax.ShapeDtypeStruct((batch, RESIDUES, seq // RESIDUES, 3 * ATT_W), F32),
                   jax.ShapeDtypeStruct((batch, RESIDUES, seq // RESIDUES, ATT_W), BF16)],
        scratch_shapes=[pltpu.VMEM((D_MODEL // LANES, tm, LANES), F32)],
        compiler_params=pltpu.CompilerParams(
            dimension_semantics=("arbitrary",), vmem_limit_bytes=VMEM_LIMIT),
        name="in_projection_attention",
    )(*[x2] * (D_MODEL // LANES), norm_w.reshape(1, D_MODEL), w_bf16)


OPERANDS = 6
STEP_CHUNKS = 2
STEP_ROWS = STEP_CHUNKS * CHUNK


def _rwkv_kernel(p_ref, gate_ref, w0_ref, a0_ref, kk_ref, ka_ref, rk_ref, lnw_ref, lnb_ref,
                 wu3_ref, au3_ref, y_ref, state_ref, ops_ref, rt_ref, gend_ref, bonus_ref,
                 *, steps_per_seq):
    t = pl.program_id(0)

    @pl.when(t == 0)
    def _():
        ops_ref[...] = jnp.zeros_like(ops_ref)
        rt_ref[...] = jnp.zeros_like(rt_ref)
        gend_ref[...] = jnp.zeros_like(gend_ref)
        bonus_ref[...] = jnp.zeros_like(bonus_ref)
        state_ref[...] = jnp.zeros_like(state_ref)

    fill, use = t % 2, (t + 1) % 2
    row = lax.broadcasted_iota(jnp.int32, (CHUNK, LANES), 0)
    lane = lax.broadcasted_iota(jnp.int32, (CHUNK, LANES), 1)
    head0 = lane < HEAD_DIM
    tcol = lane % CHUNK
    strict, incl = tcol < row, tcol <= row
    same16 = (tcol // 16) == (row // 16)
    same32 = (tcol // 32) == (row // 32)
    eye = jnp.where(tcol == row, 1.0, 0.0).astype(F32)
    bi = lax.broadcasted_iota(jnp.int32, (LANES, LANES), 0)
    bj = lax.broadcasted_iota(jnp.int32, (LANES, LANES), 1)
    block_diag = (bi // HEAD_DIM) == (bj // HEAD_DIM)
    head_ones = jnp.where(block_diag, 1.0, 0.0).astype(BF16)
    no_ones = jnp.zeros_like(head_ones)
    head_ones_diag = jnp.concatenate([jnp.concatenate([head_ones, no_ones], axis=1),
                                      jnp.concatenate([no_ones, head_ones], axis=1)], axis=0)
    ti = lax.broadcasted_iota(jnp.int32, (STEP_ROWS, STEP_ROWS), 0)
    tj = lax.broadcasted_iota(jnp.int32, (STEP_ROWS, STEP_ROWS), 1)
    tri = jnp.where((tj <= ti) & (tj // CHUNK == ti // CHUNK), 1.0, 0.0).astype(BF16)
    pairs = range(PAIRS)
    chunks = range(STEP_CHUNKS)
    units = [(c, g) for c in chunks for g in pairs]
    cols = lambda c0, g: slice(c0 + g * LANES, c0 + (g + 1) * LANES)
    rows = lambda c: slice(c * CHUNK, (c + 1) * CHUNK)

    def stack(x):
        x = x.astype(BF16)
        zero = jnp.zeros_like(x)
        return jnp.concatenate([jnp.where(head0, x, zero), jnp.where(head0, zero, x)], axis=0)

    def chain():
        ops = lambda i, c, g: ops_ref[use, i, rows(c), cols(0, g)]
        a_t, b_t, k_t, v, b_h, k_h = ([ops(i, c, g) for c, g in units] for i in range(OPERANDS))
        r_t = [rt_ref[use, rows(c), cols(0, g)] for c, g in units]
        n = range(len(units))
        v_s = [stack(x) for x in v]
        gram = [_mm_nt(jnp.concatenate([a_t[u], r_t[u]], axis=0),
                       jnp.concatenate([stack(b_t[u]), stack(k_t[u])], axis=0)) for u in n]
        yield
        a_ab = [jnp.where(strict, x[:CHUNK, :LANES], 0.0) for x in gram]
        a_ak = [jnp.where(strict, x[:CHUNK, LANES:], 0.0) for x in gram]
        a_rb = [jnp.where(incl, x[CHUNK:, :LANES], 0.0) for x in gram]
        a_rk = [jnp.where(incl, x[CHUNK:, LANES:], 0.0) for x in gram]
        nil = [jnp.where(same16, x, 0.0) for x in a_ab]
        inv = [eye + x for x in nil]
        nil = [_mm(x, stack(x)) for x in nil]
        yield
        av = [_mm(jnp.concatenate([a_ak[u], a_rk[u]], axis=0), v_s[u]) for u in n]
        yield
        for _ in range(2):
            both = [_mm(jnp.concatenate([nil[u], inv[u]], axis=0), stack(nil[u])) for u in n]
            nil = [x[:CHUNK] for x in both]
            inv = [inv[u] + both[u][CHUNK:] for u in n]
            yield
        vk = [_mm_tn(v[u], k_h[u]) for u in n]
        yield
        inv = [inv[u] + _mm(inv[u], stack(nil[u])) for u in n]
        yield
        for off_diag in (same32 & ~same16, ~same32):
            low = [_mm(inv[u], stack(jnp.where(off_diag, a_ab[u], 0.0))) for u in n]
            yield
            inv = [inv[u] + _mm(low[u], stack(inv[u])) for u in n]
            yield
        tx = [_mm(inv[u], jnp.concatenate([stack(a_t[u]), stack(av[u][:CHUNK])], axis=1))
              for u in n]
        yield
        zz = [_mm(a_rb[u], jnp.concatenate([stack(tx[u][:, :LANES]), stack(tx[u][:, LANES:])], axis=1))
              for u in n]
        yield
        r_p = [r_t[u] + zz[u][:, :LANES] for u in n]
        y0 = [zz[u][:, LANES:] + av[u][CHUNK:] for u in n]
        mn = [_mm_tn(tx[u], b_h[u]) for u in n]
        yield
        fresh = lax.rem(t - 1, steps_per_seq) == 0
        state = [jnp.where(fresh, 0.0, state_ref[g]) for g in pairs]
        y = []
        for u, (c, g) in enumerate(units):
            y.append(_mm_nt(r_p[u], state[g]) + y0[u])
            m_t = jnp.where(block_diag, mn[u][:LANES], 0.0)
            n_t = jnp.where(block_diag, mn[u][LANES:] + vk[u], 0.0)
            g_end = gend_ref[use, c * SUBLANES:c * SUBLANES + 1, cols(0, g)]
            state[g] = state[g] * g_end + _mm(state[g], m_t) + n_t
            if g == PAIRS - 1:
                yield
        for g in pairs:
            state_ref[g] = state[g]
        y_all = jnp.concatenate(y, axis=0)
        moments = jnp.dot(jnp.concatenate([y_all, y_all * y_all], axis=1).astype(BF16), head_ones_diag,
                          preferred_element_type=F32) * (1.0 / HEAD_DIM)
        for u, (c, g) in enumerate(units):
            mean = moments[u * CHUNK:(u + 1) * CHUNK, :LANES]
            var = moments[u * CHUNK:(u + 1) * CHUNK, LANES:] - mean * mean
            yn = (y[u] - mean) * lax.rsqrt(var + GN_EPS)
            yn = yn * lnw_ref[:, cols(0, g)] + lnb_ref[:, cols(0, g)]
            y_ref[rows(c), cols(0, g)] = ((yn + bonus_ref[use, rows(c), cols(0, g)])
                                          * gate_ref[rows(c), cols(0, g)].astype(F32)).astype(y_ref.dtype)

    def prepare():
        lo = p_ref[:, LORA_COL0:LORA_COL0 + LANES]
        z_all = -(w0_ref[...] + _mm_x3(jnp.tanh(lo[:, :LORA_RANK]), wu3_ref[...]))
        a_all = a0_ref[...] + _mm_x3(lo[:, LORA_RANK:], au3_ref[...])
        yield
        for g in pairs:
            c = cols(0, g)
            z = z_all[:, c]
            w = -(jnp.maximum(z, 0.0) + jnp.log(1.0 + jnp.exp(-jnp.abs(z)))) - 0.5
            lw = -jnp.exp(w)
            lg = _mm_split_rhs(tri, lw)
            ends = [lg[(i + 1) * CHUNK - 1:(i + 1) * CHUNK, :] for i in chunks]
            lg_end = jnp.concatenate([jnp.broadcast_to(e, (CHUNK, LANES)) for e in ends], axis=0)
            for i in chunks:
                gend_ref[fill, i * SUBLANES:(i + 1) * SUBLANES, c] = jnp.broadcast_to(
                    jnp.exp(ends[i]), (SUBLANES, LANES))
            e_inv = jnp.exp(-lg)
            e_end = jnp.exp(lg_end - lg)
            a = _sigmoid(a_all[:, c])
            r, k, v = p_ref[:, c], p_ref[:, cols(K_COL0, g)], p_ref[:, cols(V_COL0, g)]
            kk = k * kk_ref[:, c]
            k2 = k * (1.0 + (a - 1.0) * ka_ref[:, c])
            sums = jnp.dot(jnp.concatenate([kk * kk, r * k2 * rk_ref[:, c]], axis=1).astype(BF16),
                           head_ones_diag, preferred_element_type=F32)
            kk = kk / jnp.maximum(jnp.sqrt(sums[:, :LANES]), 1e-12)
            bonus_ref[fill, :, c] = sums[:, LANES:] * v
            rt_ref[fill, :, c] = r * jnp.exp(lg)
            ops_ref[fill, 0, :, c] = (-kk * jnp.exp(lg - lw)).astype(BF16)
            ops_ref[fill, 1, :, c] = (kk * a * e_inv).astype(BF16)
            ops_ref[fill, 2, :, c] = (k2 * e_inv).astype(BF16)
            ops_ref[fill, 3, :, c] = v.astype(BF16)
            ops_ref[fill, 4, :, c] = (kk * a * e_end).astype(BF16)
            ops_ref[fill, 5, :, c] = (k2 * e_end).astype(BF16)
            yield

    done = object()
    streams = [chain(), prepare()]
    while streams:
        streams = [s for s in streams if next(s, done) is not done]


def _rwkv_time_mix(p_shift, p_gates, batch, seq, w0, w_up, a0, a_up, k_k, k_a, r_k, ln_x_w, ln_x_b):
    nt = batch * seq // STEP_ROWS
    row2 = lambda v: v.reshape(1, -1).astype(F32)
    whole = lambda shape: pl.BlockSpec(shape, lambda t: (0, 0))
    chan = whole((1, RWKV_W))
    lora = whole((3 * LORA_RANK, RWKV_W))
    prepared = lambda t: (jnp.minimum(t, nt - 1), 0)
    chained = lambda t: (jnp.maximum(t - 1, 0), 0)
    return pl.pallas_call(
        functools.partial(_rwkv_kernel, steps_per_seq=seq // STEP_ROWS),
        grid=(nt + 1,),
        in_specs=[pl.BlockSpec((STEP_ROWS, SHIFT_COLS), prepared),
                  pl.BlockSpec((STEP_ROWS, RWKV_W), chained),
                  chan, chan, chan, chan, chan, chan, chan, lora, lora],
        out_specs=pl.BlockSpec((STEP_ROWS, RWKV_W), chained),
        out_shape=jax.ShapeDtypeStruct((batch * seq, RWKV_W), BF16),
        scratch_shapes=[pltpu.VMEM((PAIRS, LANES, LANES), F32),
                        pltpu.VMEM((2, OPERANDS, STEP_ROWS, RWKV_W), BF16),
                        pltpu.VMEM((2, STEP_ROWS, RWKV_W), F32),
                        pltpu.VMEM((2, STEP_CHUNKS * SUBLANES, RWKV_W), F32),
                        pltpu.VMEM((2, STEP_ROWS, RWKV_W), F32)],
        compiler_params=pltpu.CompilerParams(
            dimension_semantics=("arbitrary",), vmem_limit_bytes=VMEM_LIMIT),
        name="rwkv7_time_mix",
    )(p_shift, p_gates, row2(w0), row2(a0), row2(k_k), row2(k_a), row2(r_k), row2(ln_x_w), row2(ln_x_b),
      _stack_x3(w_up), _stack_x3(a_up))


def _attn_kernel(q_ref, k_ref, v_ref, gate_ref, y_ref, acc_scr, max_scr, sum_scr, *, seq):
    n_blocks = seq // ATT_BLK
    qi = lax.broadcasted_iota(jnp.int32, (2 * ATT_BLK, ATT_BLK), 0) % ATT_BLK
    kj = lax.broadcasted_iota(jnp.int32, (2 * ATT_BLK, ATT_BLK), 1)
    head0 = lax.broadcasted_iota(jnp.int32, (ATT_BLK, LANES), 1) < HEAD_DIM
    ones = jnp.ones((ATT_BLK, LANES), BF16)
    scale = HEAD_DIM ** -0.5 * LOG2_E
    neg = -1e30

    def attend(pi, d, kv_blocks, blocks):
        pieces = RESIDUES // d
        length = ATT_BLK // pieces
        pos = lambda i: pieces * (i % length) + i // length
        ok_prev, ok_cur = pos(kj) >= pos(qi), pos(kj) <= pos(qi)

        def at(res, blk, j):
            return j * d + res, pl.ds(pl.multiple_of(blk * length, length), length)

        def load(ref, res, blk):
            parts = []
            for j in range(pieces):
                r, rows = at(res, blk, j)
                parts.append(ref[r, rows, :])
            return jnp.concatenate(parts, axis=0) if pieces > 1 else parts[0]

        def store(scr, res, blk, val):
            for j in range(pieces):
                r, rows = at(res, blk, j)
                scr[pi, r, rows, :] = val[j * length:(j + 1) * length]

        kb = [load(k_ref, *b).astype(BF16) for b in kv_blocks]
        vb = [jnp.concatenate([load(v_ref, *b).astype(BF16), ones], axis=1) for b in kv_blocks]
        qs = []
        for cur, _, _ in blocks:
            q = load(q_ref, *kv_blocks[cur]) * scale
            qs.append(jnp.concatenate([jnp.where(head0, q, 0.0), jnp.where(head0, 0.0, q)],
                                      axis=0).astype(BF16))
        scores = []
        for (cur, prev, prev_ok), q in zip(blocks, qs):
            if prev is None:
                scores.append(jnp.where(ok_cur, _mm_nt(q, kb[cur]), neg))
            else:
                s = _mm_nt(q, jnp.concatenate([kb[prev], kb[cur]], axis=0))
                okp = ok_prev if prev_ok is None else ok_prev & prev_ok
                scores.append(jnp.concatenate([jnp.where(okp, s[:, :ATT_BLK], neg),
                                               jnp.where(ok_cur, s[:, ATT_BLK:], neg)], axis=1))
        mx = [jnp.max(s, axis=-1, keepdims=True) for s in scores]
        probs = [jnp.exp2(s - m).astype(BF16) for s, m in zip(scores, mx)]
        pv = [_mm(p, vb[cur] if prev is None else jnp.concatenate([vb[prev], vb[cur]], axis=0))
              for (cur, prev, _), p in zip(blocks, probs)]
        for (cur, _, _), m, x in zip(blocks, mx, pv):
            res, blk = kv_blocks[cur]
            store(acc_scr, res, blk, jnp.where(head0, x[:ATT_BLK, :LANES], x[ATT_BLK:, :LANES]))
            store(sum_scr, res, blk, jnp.where(head0, x[:ATT_BLK, LANES:], x[ATT_BLK:, LANES:]))
            store(max_scr, res, blk, jnp.where(head0, m[:ATT_BLK], m[ATT_BLK:]))

    for pi, d in enumerate(DILATIONS):
        nb = n_blocks // d
        if nb <= ATT_GROUP:
            n_res = ATT_GROUP // nb
            def body(it, carry, pi=pi, d=d, nb=nb, n_res=n_res):
                kv = [(it * n_res + r, u) for r in range(n_res) for u in range(nb)]
                blocks = [(r * nb + u, r * nb + u - 1 if u else None, None)
                          for r in range(n_res) for u in range(nb)]
                attend(pi, d, kv, blocks)
                return carry
            trips = d // n_res
        else:
            per_res = nb // ATT_GROUP
            def body(it, carry, pi=pi, d=d, per_res=per_res):
                res, blk0 = it // per_res, (it % per_res) * ATT_GROUP
                kv = [(res, jnp.maximum(blk0 - 1, 0))] + [(res, blk0 + u) for u in range(ATT_GROUP)]
                attend(pi, d, kv, [(1, 0, blk0 > 0)] + [(u + 1, u, None) for u in range(1, ATT_GROUP)])
                return carry
            trips = d * per_res
        lax.fori_loop(0, trips, body, 0)

    m = jnp.maximum(jnp.maximum(max_scr[0], max_scr[1]), max_scr[2])
    num = den = None
    for pi in range(len(DILATIONS)):
        w = jnp.exp2(max_scr[pi] - m)
        num = w * acc_scr[pi] if num is None else num + w * acc_scr[pi]
        den = w * sum_scr[pi] if den is None else den + w * sum_scr[pi]
    y_ref[...] = (num / den * gate_ref[...].astype(F32)).astype(y_ref.dtype)


def _dilated_attention(p_att, p_gate, batch, seq):
    per_res = seq // RESIDUES

    def tile(blk0):
        return pl.BlockSpec((RESIDUES, per_res, LANES), lambda b, g: (b, 0, blk0 + g))

    return pl.pallas_call(
        functools.partial(_attn_kernel, seq=seq),
        grid=(batch, PAIRS),
        in_specs=[tile(0), tile(PAIRS), tile(2 * PAIRS), tile(0)],
        out_specs=pl.BlockSpec((RESIDUES, per_res, LANES), lambda b, g: (b, 0, g)),
        out_shape=jax.ShapeDtypeStruct((batch * RESIDUES, per_res, ATT_W), BF16),
        scratch_shapes=[pltpu.VMEM((len(DILATIONS), RESIDUES, per_res, LANES), F32)] * 3,
        compiler_params=pltpu.CompilerParams(
            dimension_semantics=("arbitrary", "arbitrary"), vmem_limit_bytes=VMEM_LIMIT),
        name="dilated_attention",
    )(p_att, p_att, p_att, p_gate)


def _out_kernel(yr_ref, ya_ref, x_ref, wr_ref, wa_ref, fw_ref, o_ref, tmp_ref, seq_ref):
    per_res = ya_ref.shape[1]
    quarter = per_res * REGROUP_STRIDE
    for c in range(ATT_W // LANES):
        for r in range(RESIDUES):
            tmp_ref[c, pl.ds((r % REGROUP_STRIDE) * quarter + r // REGROUP_STRIDE, per_res,
                             stride=REGROUP_STRIDE), :] = ya_ref[r, :, c * LANES:(c + 1) * LANES].astype(F32)
        for g in range(REGROUP_STRIDE):
            seq_ref[c, pl.ds(g, quarter, stride=REGROUP_STRIDE), :] = tmp_ref[c, g * quarter:(g + 1) * quarter, :]
    ya = jnp.concatenate([seq_ref[c] for c in range(ATT_W // LANES)], axis=1).astype(BF16)
    h = (x_ref[...]
         + jnp.dot(yr_ref[...], wr_ref[...], preferred_element_type=F32)
         + jnp.dot(ya, wa_ref[...], preferred_element_type=F32))
    ms = jnp.mean(h * h, axis=-1, keepdims=True)
    o_ref[...] = h * lax.rsqrt(ms + RMS_EPS) * fw_ref[...]


def _out_projection(y_rwkv, y_att, x2, w_out_bf16, final_norm_w, batch, seq):
    tm = min(PROJ_TM, seq)
    tiles = seq // tm
    tok = lambda width: pl.BlockSpec((tm, width), lambda i: (i, 0))
    return pl.pallas_call(
        _out_kernel,
        grid=(batch * tiles,),
        in_specs=[
            tok(RWKV_W),
            pl.BlockSpec((None, RESIDUES, tm // RESIDUES, ATT_W), lambda i: (i // tiles, 0, i % tiles, 0)),
            tok(D_MODEL),
            pl.BlockSpec((RWKV_W, D_MODEL), lambda i: (0, 0)),
            pl.BlockSpec((ATT_W, D_MODEL), lambda i: (1, 0)),
            pl.BlockSpec((1, D_MODEL), lambda i: (0, 0)),
        ],
        out_specs=tok(D_MODEL),
        out_shape=jax.ShapeDtypeStruct((batch * seq, D_MODEL), F32),
        scratch_shapes=[pltpu.VMEM((ATT_W // LANES, tm, LANES), F32)] * 2,
        compiler_params=pltpu.CompilerParams(
            dimension_semantics=("arbitrary",), vmem_limit_bytes=VMEM_LIMIT),
        name="out_projection",
    )(y_rwkv, y_att, x2, w_out_bf16, w_out_bf16, final_norm_w.reshape(1, D_MODEL))


def kernel(x, norm_w, w_in, mu_shift, w0, w_up, a0, a_up, k_k, k_a, r_k, ln_x_w, ln_x_b, w_out,
           final_norm_w):
    batch, seq, _ = x.shape
    assert seq % (ATT_BLK * max(DILATIONS)) == 0 and seq % STEP_ROWS == 0
    x2 = x.astype(F32).reshape(batch * seq, D_MODEL)
    nw = norm_w.astype(F32)
    assert w_in.shape[1] == ATT0 + 4 * ATT_W and 4 * ATT_W <= ATT0
    w_bf16 = w_in.astype(BF16)
    p_shift, gate_rwkv = _rwkv_projection(x2, nw, w_bf16, seq, mu_shift)
    p_att, gate_att = _att_projection(x2, nw, w_bf16, batch, seq)
    y_rwkv = _rwkv_time_mix(p_shift, gate_rwkv, batch, seq, w0, w_up, a0, a_up, k_k, k_a, r_k,
                            ln_x_w, ln_x_b)
    per_res = seq // RESIDUES
    y_att = _dilated_attention(p_att.reshape(batch * RESIDUES, per_res, 3 * ATT_W),
                               gate_att.reshape(batch * RESIDUES, per_res, ATT_W), batch, seq)
    out = _out_projection(y_rwkv, y_att.reshape(batch, RESIDUES, per_res, ATT_W), x2, w_out.astype(BF16),
                          final_norm_w.astype(F32), batch, seq)
    return out.reshape(batch, seq, D_MODEL).astype(x.dtype)
```

```python
import functools

import jax
import jax.numpy as jnp
from jax import lax
from jax.experimental import pallas as pl
from jax.experimental.pallas import tpu as pltpu

D_MODEL = 1024
HEAD_DIM = 64
RWKV_W = 1024
ATT_W = 1024
MIX_W = RWKV_W + ATT_W
LORA_RANK = 64
SHIFT_COLS = 3 * RWKV_W + 2 * LORA_RANK
RWKV_GATE0 = SHIFT_COLS
ATT0 = RWKV_GATE0 + RWKV_W
ATT_GATE0 = ATT0 + 3 * ATT_W
DILATIONS = (1, 4, 16)
ATT_BLK = 128
RESIDUES = max(DILATIONS)
ATT_GROUP = 8
LOG2_E = 1.4426950408889634
RMS_EPS = 1e-5
GN_EPS = 64e-5

LANES = 128
SUBLANES = 8
PAIRS = RWKV_W // LANES
CHUNK = 64
K_COL0, V_COL0, LORA_COL0 = RWKV_W, 2 * RWKV_W, 3 * RWKV_W
PROJ_TM = 512
OUT_TM = 1024

F32 = jnp.float32
BF16 = jnp.bfloat16
VMEM_LIMIT = 56 * 1024 * 1024

NT_DIMS = (((1,), (1,)), ((), ()))
TN_DIMS = (((0,), (0,)), ((), ()))


def _mm(a, b):
    return jnp.dot(a.astype(BF16), b.astype(BF16), preferred_element_type=F32)


def _mm_nt(a, b):
    return lax.dot_general(a.astype(BF16), b.astype(BF16), NT_DIMS, preferred_element_type=F32)


def _mm_tn(a, b):
    return lax.dot_general(a.astype(BF16), b.astype(BF16), TN_DIMS, preferred_element_type=F32)


def _split(x):
    hi = x.astype(BF16)
    return hi, (x - hi.astype(F32)).astype(BF16)


def _mm_split_rhs(w_bf16, x):
    n = x.shape[1]
    both = jnp.dot(w_bf16, jnp.concatenate(_split(x), axis=1), preferred_element_type=F32)
    return both[:, :n] + both[:, n:]


def _mm_x3(x, w3):
    hi, lo = _split(x)
    return jnp.dot(jnp.concatenate([hi, lo, hi], axis=1), w3, preferred_element_type=F32)


def _stack_x3(w):
    hi, lo = _split(w.astype(F32))
    return jnp.concatenate([hi, hi, lo], axis=0)


def _sigmoid(x):
    return 1.0 / (1.0 + jnp.exp(-x))


REGROUP_STRIDE = 4


def _regrouped_rows(x_refs, tmp_ref):
    tm = x_refs[0].shape[0]
    quarter, per_res = tm // REGROUP_STRIDE, tm // RESIDUES
    slabs = []
    for c, ref in enumerate(x_refs):
        for g in range(REGROUP_STRIDE):
            tmp_ref[c, g * quarter:(g + 1) * quarter, :] = ref[pl.ds(g, quarter, stride=REGROUP_STRIDE), :]
        slabs.append(jnp.concatenate(
            [tmp_ref[c, pl.ds((r % REGROUP_STRIDE) * quarter + r // REGROUP_STRIDE, per_res,
                              stride=REGROUP_STRIDE), :] for r in range(RESIDUES)], axis=0))
    return jnp.concatenate(slabs, axis=1)


def _proj_kernel(*refs, epilogue, tiles_per_seq):
    if epilogue == "qkv_gate":
        slabs = D_MODEL // LANES
        nw_ref, w_ref, *rest, tmp_ref = refs[slabs:]
        x = _regrouped_rows(refs[:slabs], tmp_ref)
    else:
        x_ref, nw_ref, w_ref, *rest = refs
        x = x_ref[...]
    ms = jnp.mean(x * x, axis=-1, keepdims=True)
    xn = (x * lax.rsqrt(ms + RMS_EPS) * nw_ref[...]).astype(BF16)
    w = w_ref[:, :4 * ATT_W] if epilogue == "qkv_gate" else w_ref[...]
    o = jnp.dot(xn, w, preferred_element_type=F32)
    if epilogue == "shift_gate":
        mu_ref, o_ref, gate_ref, prev_ref = rest
        tm = o.shape[0]
        gate = o[:, SHIFT_COLS:]
        gate_ref[...] = (gate * _sigmoid(gate)).astype(gate_ref.dtype)
        o = o[:, :SHIFT_COLS]
        first = pl.program_id(0) % tiles_per_seq == 0
        last = jnp.where(first, 0.0, prev_ref[SUBLANES - 1:SUBLANES, :])
        row = lax.broadcasted_iota(jnp.int32, (tm, 1), 0)
        oprev = jnp.where(row == 0, last, pltpu.roll(o, 1, 0))
        prev_ref[...] = o[tm - SUBLANES:, :]
        o_ref[...] = o + (oprev - o) * mu_ref[...]
    elif epilogue == "qkv_gate":
        qkv_ref, gate_ref = rest
        qkv_ref[...] = o[:, :3 * ATT_W].reshape(qkv_ref.shape)
        gate = o[:, 3 * ATT_W:]
        gate_ref[...] = (gate * _sigmoid(gate)).astype(gate_ref.dtype).reshape(gate_ref.shape)


def _rwkv_projection(x2, norm_w, w_bf16, seq, mu):
    tokens = x2.shape[0]
    tm = min(PROJ_TM, seq)
    once = pl.Buffered(1)
    return pl.pallas_call(
        functools.partial(_proj_kernel, epilogue="shift_gate", tiles_per_seq=seq // tm),
        grid=(tokens // tm,),
        in_specs=[pl.BlockSpec((tm, D_MODEL), lambda i: (i, 0)),
                  pl.BlockSpec((1, D_MODEL), lambda i: (0, 0), pipeline_mode=once),
                  pl.BlockSpec((D_MODEL, ATT0), lambda i: (0, 0), pipeline_mode=once),
                  pl.BlockSpec((1, SHIFT_COLS), lambda i: (0, 0), pipeline_mode=once)],
        out_specs=[pl.BlockSpec((tm, SHIFT_COLS), lambda i: (i, 0)),
                   pl.BlockSpec((tm, RWKV_W), lambda i: (i, 0))],
        out_shape=[jax.ShapeDtypeStruct((tokens, SHIFT_COLS), F32),
                   jax.ShapeDtypeStruct((tokens, RWKV_W), BF16)],
        scratch_shapes=[pltpu.VMEM((SUBLANES, SHIFT_COLS), F32)],
        compiler_params=pltpu.CompilerParams(
            dimension_semantics=("arbitrary",), vmem_limit_bytes=VMEM_LIMIT),
        name="in_projection_rwkv",
    )(x2, norm_w.reshape(1, D_MODEL), w_bf16, mu.reshape(1, SHIFT_COLS).astype(F32))


def _att_projection(x2, norm_w, w_bf16, batch, seq):
    tm = min(PROJ_TM, seq)
    tiles = seq // tm
    rows = tm // RESIDUES
    out_block = lambda width: pl.BlockSpec((None, RESIDUES, rows, width),
                                           lambda i: (i // tiles, 0, i % tiles, 0))
    return pl.pallas_call(
        functools.partial(_proj_kernel, epilogue="qkv_gate", tiles_per_seq=tiles),
        grid=(batch * tiles,),
        in_specs=[pl.BlockSpec((tm, LANES), functools.partial(lambda c, i: (i, c), c))
                  for c in range(D_MODEL // LANES)]
                 + [pl.BlockSpec((1, D_MODEL), lambda i: (0, 0)),
                    pl.BlockSpec((D_MODEL, ATT0), lambda i: (0, 1))],
        out_specs=[out_block(3 * ATT_W), out_block(ATT_W)],
        out_shape=[jax.ShapeDtypeStruct((batch, RESIDUES, seq // RESIDUES, 3 * ATT_W), F32),
                   jax.ShapeDtypeStruct((batch, RESIDUES, seq // RESIDUES, ATT_W), BF16)],
        scratch_shapes=[pltpu.VMEM((D_MODEL // LANES, tm, LANES), F32)],
        compiler_params=pltpu.CompilerParams(
            dimension_semantics=("arbitrary",), vmem_limit_bytes=VMEM_LIMIT),
        name="in_projection_attention",
    )(*[x2] * (D_MODEL // LANES), norm_w.reshape(1, D_MODEL), w_bf16)


OPERANDS = 6
STEP_CHUNKS = 2
STEP_ROWS = STEP_CHUNKS * CHUNK


def _rwkv_kernel(p_ref, gate_ref, w0_ref, a0_ref, kk_ref, ka_ref, rk_ref, lnw_ref, lnb_ref,
                 wu3_ref, au3_ref, y_ref, state_ref, ops_ref, rt_ref, gend_ref, bonus_ref,
                 *, steps_per_seq):
    t = pl.program_id(0)

    @pl.when(t == 0)
    def _():
        ops_ref[...] = jnp.zeros_like(ops_ref)
        rt_ref[...] = jnp.zeros_like(rt_ref)
        gend_ref[...] = jnp.zeros_like(gend_ref)
        bonus_ref[...] = jnp.zeros_like(bonus_ref)
        state_ref[...] = jnp.zeros_like(state_ref)

    fill, use = t % 2, (t + 1) % 2
    row = lax.broadcasted_iota(jnp.int32, (CHUNK, LANES), 0)
    lane = lax.broadcasted_iota(jnp.int32, (CHUNK, LANES), 1)
    head0 = lane < HEAD_DIM
    tcol = lane % CHUNK
    strict, incl = tcol < row, tcol <= row
    same16 = (tcol // 16) == (row // 16)
    same32 = (tcol // 32) == (row // 32)
    eye = jnp.where(tcol == row, 1.0, 0.0).astype(F32)
    bi = lax.broadcasted_iota(jnp.int32, (LANES, LANES), 0)
    bj = lax.broadcasted_iota(jnp.int32, (LANES, LANES), 1)
    block_diag = (bi // HEAD_DIM) == (bj // HEAD_DIM)
    head_ones = jnp.where(block_diag, 1.0, 0.0).astype(BF16)
    no_ones = jnp.zeros_like(head_ones)
    head_ones_diag = jnp.concatenate([jnp.concatenate([head_ones, no_ones], axis=1),
                                      jnp.concatenate([no_ones, head_ones], axis=1)], axis=0)
    ti = lax.broadcasted_iota(jnp.int32, (STEP_ROWS, STEP_ROWS), 0)
    tj = lax.broadcasted_iota(jnp.int32, (STEP_ROWS, STEP_ROWS), 1)
    tri = jnp.where((tj <= ti) & (tj // CHUNK == ti // CHUNK), 1.0, 0.0).astype(BF16)
    pairs = range(PAIRS)
    chunks = range(STEP_CHUNKS)
    units = [(c, g) for c in chunks for g in pairs]
    cols = lambda c0, g: slice(c0 + g * LANES, c0 + (g + 1) * LANES)
    rows = lambda c: slice(c * CHUNK, (c + 1) * CHUNK)

    def stack(x):
        x = x.astype(BF16)
        zero = jnp.zeros_like(x)
        return jnp.concatenate([jnp.where(head0, x, zero), jnp.where(head0, zero, x)], axis=0)

    def chain():
        ops = lambda i, c, g: ops_ref[use, i, rows(c), cols(0, g)]
        a_t, b_t, k_t, v, b_h, k_h = ([ops(i, c, g) for c, g in units] for i in range(OPERANDS))
        r_t = [rt_ref[use, rows(c), cols(0, g)] for c, g in units]
        n = range(len(units))
        v_s = [stack(x) for x in v]
        gram = [_mm_nt(jnp.concatenate([a_t[u], r_t[u]], axis=0),
                       jnp.concatenate([stack(b_t[u]), stack(k_t[u])], axis=0)) for u in n]
        yield
        a_ab = [jnp.where(strict, x[:CHUNK, :LANES], 0.0) for x in gram]
        a_ak = [jnp.where(strict, x[:CHUNK, LANES:], 0.0) for x in gram]
        a_rb = [jnp.where(incl, x[CHUNK:, :LANES], 0.0) for x in gram]
        a_rk = [jnp.where(incl, x[CHUNK:, LANES:], 0.0) for x in gram]
        nil = [jnp.where(same16, x, 0.0) for x in a_ab]
        inv = [eye + x for x in nil]
        nil = [_mm(x, stack(x)) for x in nil]
        yield
        av = [_mm(jnp.concatenate([a_ak[u], a_rk[u]], axis=0), v_s[u]) for u in n]
        yield
        for _ in range(2):
            both = [_mm(jnp.concatenate([nil[u], inv[u]], axis=0), stack(nil[u])) for u in n]
            nil = [x[:CHUNK] for x in both]
            inv = [inv[u] + both[u][CHUNK:] for u in n]
            yield
        vk = [_mm_tn(v[u], k_h[u]) for u in n]
        yield
        inv = [inv[u] + _mm(inv[u], stack(nil[u])) for u in n]
        yield
        for off_diag in (same32 & ~same16, ~same32):
            low = [_mm(inv[u], stack(jnp.where(off_diag, a_ab[u], 0.0))) for u in n]
            yield
            inv = [inv[u] + _mm(low[u], stack(inv[u])) for u in n]
            yield
        tx = [_mm(inv[u], jnp.concatenate([stack(a_t[u]), stack(av[u][:CHUNK])], axis=1))
              for u in n]
        yield
        zz = [_mm(a_rb[u], jnp.concatenate([stack(tx[u][:, :LANES]), stack(tx[u][:, LANES:])], axis=1))
              for u in n]
        yield
        r_p = [r_t[u] + zz[u][:, :LANES] for u in n]
        y0 = [zz[u][:, LANES:] + av[u][CHUNK:] for u in n]
        mn = [_mm_tn(tx[u], b_h[u]) for u in n]
        yield
        fresh = lax.rem(t - 1, steps_per_seq) == 0
        state = [jnp.where(fresh, 0.0, state_ref[g]) for g in pairs]
        y = []
        for u, (c, g) in enumerate(units):
            y.append(_mm_nt(r_p[u], state[g]) + y0[u])
            m_t = jnp.where(block_diag, mn[u][:LANES], 0.0)
            n_t = jnp.where(block_diag, mn[u][LANES:] + vk[u], 0.0)
            g_end = gend_ref[use, c * SUBLANES:c * SUBLANES + 1, cols(0, g)]
            state[g] = state[g] * g_end + _mm(state[g], m_t) + n_t
            if g == PAIRS - 1:
                yield
        for g in pairs:
            state_ref[g] = state[g]
        y_all = jnp.concatenate(y, axis=0)
        moments = jnp.dot(jnp.concatenate([y_all, y_all * y_all], axis=1).astype(BF16), head_ones_diag,
                          preferred_element_type=F32) * (1.0 / HEAD_DIM)
        for u, (c, g) in enumerate(units):
            mean = moments[u * CHUNK:(u + 1) * CHUNK, :LANES]
            var = moments[u * CHUNK:(u + 1) * CHUNK, LANES:] - mean * mean
            yn = (y[u] - mean) * lax.rsqrt(var + GN_EPS)
            yn = yn * lnw_ref[:, cols(0, g)] + lnb_ref[:, cols(0, g)]
            y_ref[rows(c), cols(0, g)] = ((yn + bonus_ref[use, rows(c), cols(0, g)])
                                          * gate_ref[rows(c), cols(0, g)].astype(F32)).astype(y_ref.dtype)

    def prepare():
        lo = p_ref[:, LORA_COL0:LORA_COL0 + LANES]
        z_all = -(w0_ref[...] + _mm_x3(jnp.tanh(lo[:, :LORA_RANK]), wu3_ref[...]))
        a_all = a0_ref[...] + _mm_x3(lo[:, LORA_RANK:], au3_ref[...])
        yield
        for g in pairs:
            c = cols(0, g)
            z = z_all[:, c]
            w = -(jnp.maximum(z, 0.0) + jnp.log(1.0 + jnp.exp(-jnp.abs(z)))) - 0.5
            lw = -jnp.exp(w)
            lg = _mm_split_rhs(tri, lw)
            ends = [lg[(i + 1) * CHUNK - 1:(i + 1) * CHUNK, :] for i in chunks]
            lg_end = jnp.concatenate([jnp.broadcast_to(e, (CHUNK, LANES)) for e in ends], axis=0)
            for i in chunks:
                gend_ref[fill, i * SUBLANES:(i + 1) * SUBLANES, c] = jnp.broadcast_to(
                    jnp.exp(ends[i]), (SUBLANES, LANES))
            e_inv = jnp.exp(-lg)
            e_end = jnp.exp(lg_end - lg)
            a = _sigmoid(a_all[:, c])
            r, k, v = p_ref[:, c], p_ref[:, cols(K_COL0, g)], p_ref[:, cols(V_COL0, g)]
            kk = k * kk_ref[:, c]
            k2 = k * (1.0 + (a - 1.0) * ka_ref[:, c])
            sums = jnp.dot(jnp.concatenate([kk * kk, r * k2 * rk_ref[:, c]], axis=1).astype(BF16),
                           head_ones_diag, preferred_element_type=F32)
            kk = kk / jnp.maximum(jnp.sqrt(sums[:, :LANES]), 1e-12)
            bonus_ref[fill, :, c] = sums[:, LANES:] * v
            rt_ref[fill, :, c] = r * jnp.exp(lg)
            ops_ref[fill, 0, :, c] = (-kk * jnp.exp(lg - lw)).astype(BF16)
            ops_ref[fill, 1, :, c] = (kk * a * e_inv).astype(BF16)
            ops_ref[fill, 2, :, c] = (k2 * e_inv).astype(BF16)
            ops_ref[fill, 3, :, c] = v.astype(BF16)
            ops_ref[fill, 4, :, c] = (kk * a * e_end).astype(BF16)
            ops_ref[fill, 5, :, c] = (k2 * e_end).astype(BF16)
            yield

    done = object()
    streams = [chain(), prepare()]
    while streams:
        streams = [s for s in streams if next(s, done) is not done]


def _rwkv_time_mix(p_shift, p_gates, batch, seq, w0, w_up, a0, a_up, k_k, k_a, r_k, ln_x_w, ln_x_b):
    nt = batch * seq // STEP_ROWS
    row2 = lambda v: v.reshape(1, -1).astype(F32)
    whole = lambda shape: pl.BlockSpec(shape, lambda t: (0, 0))
    chan = whole((1, RWKV_W))
    lora = whole((3 * LORA_RANK, RWKV_W))
    prepared = lambda t: (jnp.minimum(t, nt - 1), 0)
    chained = lambda t: (jnp.maximum(t - 1, 0), 0)
    return pl.pallas_call(
        functools.partial(_rwkv_kernel, steps_per_seq=seq // STEP_ROWS),
        grid=(nt + 1,),
        in_specs=[pl.BlockSpec((STEP_ROWS, SHIFT_COLS), prepared),
                  pl.BlockSpec((STEP_ROWS, RWKV_W), chained),
                  chan, chan, chan, chan, chan, chan, chan, lora, lora],
        out_specs=pl.BlockSpec((STEP_ROWS, RWKV_W), chained),
        out_shape=jax.ShapeDtypeStruct((batch * seq, RWKV_W), BF16),
        scratch_shapes=[pltpu.VMEM((PAIRS, LANES, LANES), F32),
                        pltpu.VMEM((2, OPERANDS, STEP_ROWS, RWKV_W), BF16),
                        pltpu.VMEM((2, STEP_ROWS, RWKV_W), F32),
                        pltpu.VMEM((2, STEP_CHUNKS * SUBLANES, RWKV_W), F32),
                        pltpu.VMEM((2, STEP_ROWS, RWKV_W), F32)],
        compiler_params=pltpu.CompilerParams(
            dimension_semantics=("arbitrary",), vmem_limit_bytes=VMEM_LIMIT),
        name="rwkv7_time_mix",
    )(p_shift, p_gates, row2(w0), row2(a0), row2(k_k), row2(k_a), row2(r_k), row2(ln_x_w), row2(ln_x_b),
      _stack_x3(w_up), _stack_x3(a_up))


def _attn_kernel(q_ref, k_ref, v_ref, gate_ref, y_ref, acc_scr, max_scr, sum_scr, *, seq):
    n_blocks = seq // ATT_BLK
    qi = lax.broadcasted_iota(jnp.int32, (2 * ATT_BLK, ATT_BLK), 0) % ATT_BLK
    kj = lax.broadcasted_iota(jnp.int32, (2 * ATT_BLK, ATT_BLK), 1)
    head0 = lax.broadcasted_iota(jnp.int32, (ATT_BLK, LANES), 1) < HEAD_DIM
    ones = jnp.ones((ATT_BLK, LANES), BF16)
    scale = HEAD_DIM ** -0.5 * LOG2_E
    neg = -1e30

    def attend(pi, d, kv_blocks, blocks):
        pieces = RESIDUES // d
        length = ATT_BLK // pieces
        pos = lambda i: pieces * (i % length) + i // length
        ok_prev, ok_cur = pos(kj) >= pos(qi), pos(kj) <= pos(qi)

        def at(res, blk, j):
            return j * d + res, pl.ds(pl.multiple_of(blk * length, length), length)

        def load(ref, res, blk):
            parts = []
            for j in range(pieces):
                r, rows = at(res, blk, j)
                parts.append(ref[r, rows, :])
            return jnp.concatenate(parts, axis=0) if pieces > 1 else parts[0]

        def store(scr, res, blk, val):
            for j in range(pieces):
                r, rows = at(res, blk, j)
                scr[pi, r, rows, :] = val[j * length:(j + 1) * length]

        kb = [load(k_ref, *b).astype(BF16) for b in kv_blocks]
        vb = [jnp.concatenate([load(v_ref, *b).astype(BF16), ones], axis=1) for b in kv_blocks]
        qs = []
        for cur, _, _ in blocks:
            q = load(q_ref, *kv_blocks[cur]) * scale
            qs.append(jnp.concatenate([jnp.where(head0, q, 0.0), jnp.where(head0, 0.0, q)],
                                      axis=0).astype(BF16))
        scores = []
        for (cur, prev, prev_ok), q in zip(blocks, qs):
            if prev is None:
                scores.append(jnp.where(ok_cur, _mm_nt(q, kb[cur]), neg))
            else:
                s = _mm_nt(q, jnp.concatenate([kb[prev], kb[cur]], axis=0))
                okp = ok_prev if prev_ok is None else ok_prev & prev_ok
                scores.append(jnp.concatenate([jnp.where(okp, s[:, :ATT_BLK], neg),
                                               jnp.where(ok_cur, s[:, ATT_BLK:], neg)], axis=1))
        mx = [jnp.max(s, axis=-1, keepdims=True) for s in scores]
        probs = [jnp.exp2(s - m).astype(BF16) for s, m in zip(scores, mx)]
        pv = [_mm(p, vb[cur] if prev is None else jnp.concatenate([vb[prev], vb[cur]], axis=0))
              for (cur, prev, _), p in zip(blocks, probs)]
        for (cur, _, _), m, x in zip(blocks, mx, pv):
            res, blk = kv_blocks[cur]
            store(acc_scr, res, blk, jnp.where(head0, x[:ATT_BLK, :LANES], x[ATT_BLK:, :LANES]))
            store(sum_scr, res, blk, jnp.where(head0, x[:ATT_BLK, LANES:], x[ATT_BLK:, LANES:]))
            store(max_scr, res, blk, jnp.where(head0, m[:ATT_BLK], m[ATT_BLK:]))

    for pi, d in enumerate(DILATIONS):
        nb = n_blocks // d
        if nb <= ATT_GROUP:
            n_res = ATT_GROUP // nb
            def body(it, carry, pi=pi, d=d, nb=nb, n_res=n_res):
                kv = [(it * n_res + r, u) for r in range(n_res) for u in range(nb)]
                blocks = [(r * nb + u, r * nb + u - 1 if u else None, None)
                          for r in range(n_res) for u in range(nb)]
                attend(pi, d, kv, blocks)
                return carry
            trips = d // n_res
        else:
            per_res = nb // ATT_GROUP
            def body(it, carry, pi=pi, d=d, per_res=per_res):
                res, blk0 = it // per_res, (it % per_res) * ATT_GROUP
                kv = [(res, jnp.maximum(blk0 - 1, 0))] + [(res, blk0 + u) for u in range(ATT_GROUP)]
                attend(pi, d, kv, [(1, 0, blk0 > 0)] + [(u + 1, u, None) for u in range(1, ATT_GROUP)])
                return carry
            trips = d * per_res
        lax.fori_loop(0, trips, body, 0)

    m = jnp.maximum(jnp.maximum(max_scr[0], max_scr[1]), max_scr[2])
    num = den = None
    for pi in range(len(DILATIONS)):
        w = jnp.exp2(max_scr[pi] - m)
        num = w * acc_scr[pi] if num is None else num + w * acc_scr[pi]
        den = w * sum_scr[pi] if den is None else den + w * sum_scr[pi]
    y_ref[...] = (num / den * gate_ref[...].astype(F32)).astype(y_ref.dtype)


def _dilated_attention(p_att, p_gate, batch, seq):
    per_res = seq // RESIDUES

    def tile(blk0):
        return pl.BlockSpec((RESIDUES, per_res, LANES), lambda b, g: (b, 0, blk0 + g))

    return pl.pallas_call(
        functools.partial(_attn_kernel, seq=seq),
        grid=(batch, PAIRS),
        in_specs=[tile(0), tile(PAIRS), tile(2 * PAIRS), tile(0)],
        out_specs=pl.BlockSpec((RESIDUES, per_res, LANES), lambda b, g: (b, 0, g)),
        out_shape=jax.ShapeDtypeStruct((batch * RESIDUES, per_res, ATT_W), BF16),
        scratch_shapes=[pltpu.VMEM((len(DILATIONS), RESIDUES, per_res, LANES), F32)] * 3,
        compiler_params=pltpu.CompilerParams(
            dimension_semantics=("arbitrary", "arbitrary"), vmem_limit_bytes=VMEM_LIMIT),
        name="dilated_attention",
    )(p_att, p_att, p_att, p_gate)


def _out_kernel(yr_ref, ya_ref, x_ref, wr_ref, wa_ref, fw_ref, o_ref, tmp_ref, seq_ref):
    per_res = ya_ref.shape[1]
    quarter = per_res * REGROUP_STRIDE
    for c in range(ATT_W // LANES):
        for r in range(RESIDUES):
            tmp_ref[c, pl.ds((r % REGROUP_STRIDE) * quarter + r // REGROUP_STRIDE, per_res,
                             stride=REGROUP_STRIDE), :] = ya_ref[r, :, c * LANES:(c + 1) * LANES].astype(F32)
        for g in range(REGROUP_STRIDE):
            seq_ref[c, pl.ds(g, quarter, stride=REGROUP_STRIDE), :] = tmp_ref[c, g * quarter:(g + 1) * quarter, :]
    ya = jnp.concatenate([seq_ref[c] for c in range(ATT_W // LANES)], axis=1).astype(BF16)
    h = (x_ref[...]
         + jnp.dot(yr_ref[...], wr_ref[...], preferred_element_type=F32)
         + jnp.dot(ya, wa_ref[...], preferred_element_type=F32))
    ms = jnp.mean(h * h, axis=-1, keepdims=True)
    o_ref[...] = h * lax.rsqrt(ms + RMS_EPS) * fw_ref[...]


def _out_projection(y_rwkv, y_att, x2, w_out_bf16, final_norm_w, batch, seq):
    tm = min(OUT_TM, seq)
    tiles = seq // tm
    tok = lambda width: pl.BlockSpec((tm, width), lambda i: (i, 0))
    return pl.pallas_call(
        _out_kernel,
        grid=(batch * tiles,),
        in_specs=[
            tok(RWKV_W),
            pl.BlockSpec((None, RESIDUES, tm // RESIDUES, ATT_W), lambda i: (i // tiles, 0, i % tiles, 0)),
            tok(D_MODEL),
            pl.BlockSpec((RWKV_W, D_MODEL), lambda i: (0, 0)),
            pl.BlockSpec((ATT_W, D_MODEL), lambda i: (1, 0)),
            pl.BlockSpec((1, D_MODEL), lambda i: (0, 0)),
        ],
        out_specs=tok(D_MODEL),
        out_shape=jax.ShapeDtypeStruct((batch * seq, D_MODEL), F32),
        scratch_shapes=[pltpu.VMEM((ATT_W // LANES, tm, LANES), F32)] * 2,
        compiler_params=pltpu.CompilerParams(
            dimension_semantics=("arbitrary",), vmem_limit_bytes=VMEM_LIMIT),
        name="out_projection",
    )(y_rwkv, y_att, x2, w_out_bf16, w_out_bf16, final_norm_w.reshape(1, D_MODEL))


def kernel(x, norm_w, w_in, mu_shift, w0, w_up, a0, a_up, k_k, k_a, r_k, ln_x_w, ln_x_b, w_out,
           final_norm_w):
    batch, seq, _ = x.shape
    assert seq % (ATT_BLK * max(DILATIONS)) == 0 and seq % STEP_ROWS == 0
    x2 = x.astype(F32).reshape(batch * seq, D_MODEL)
    nw = norm_w.astype(F32)
    assert w_in.shape[1] == ATT0 + 4 * ATT_W and 4 * ATT_W <= ATT0
    w_bf16 = w_in.astype(BF16)
    p_shift, gate_rwkv = _rwkv_projection(x2, nw, w_bf16, seq, mu_shift)
    p_att, gate_att = _att_projection(x2, nw, w_bf16, batch, seq)
    y_rwkv = _rwkv_time_mix(p_shift, gate_rwkv, batch, seq, w0, w_up, a0, a_up, k_k, k_a, r_k,
                            ln_x_w, ln_x_b)
    per_res = seq // RESIDUES
    y_att = _dilated_attention(p_att.reshape(batch * RESIDUES, per_res, 3 * ATT_W),
                               gate_att.reshape(batch * RESIDUES, per_res, ATT_W), batch, seq)
    out = _out_projection(y_rwkv, y_att.reshape(batch, RESIDUES, per_res, ATT_W), x2, w_out.astype(BF16),
                          final_norm_w.astype(F32), batch, seq)
    return out.reshape(batch, seq, D_MODEL).astype(x.dtype)
```
